```python
import jax, jax.numpy as jnp
from jax import lax
import numpy as np

D_MODEL = 1024
BATCH = 16
SEQ = 4096
DEPTH = 1

SB_HEADS = 8
SB_HEAD_DIM = 64
SB_WIDTH = SB_HEADS * SB_HEAD_DIM
Q_BLOCK = 128
LRU_WIDTH = D_MODEL
LRU_BLOCKS = 16
LRU_BLOCK_DIM = LRU_WIDTH // LRU_BLOCKS
CONV_WIDTH = 4
RG_C = 8.0
N_EXPERTS = 32
TOP_K = 4
D_FF = D_MODEL
SWIGLU_ALPHA = 1.702
SWIGLU_LIMIT = 7.0
MOE_BLOCK = 512
NORM_EPS = 1e-6
IN_SPLITS = [SB_WIDTH, 2 * SB_WIDTH, 3 * SB_WIDTH,
             3 * SB_WIDTH + LRU_WIDTH, 3 * SB_WIDTH + 2 * LRU_WIDTH,
             3 * SB_WIDTH + 2 * LRU_WIDTH + D_MODEL]
IN_COLS = 3 * SB_WIDTH + 2 * LRU_WIDTH + 2 * D_MODEL

kernel_name = "hybrid_stickbreak_rglru_moe_block"


def rms_norm(x, g):
    xf = x.astype(jnp.float32)
    y = xf * lax.rsqrt(jnp.mean(xf * xf, axis=-1, keepdims=True) + NORM_EPS)
    return (y * g.astype(jnp.float32)).astype(x.dtype)


def stick_breaking_attention(q, k, v):
    seq = q.shape[2]
    scale = SB_HEAD_DIM ** -0.5
    outs = []
    for blk in range(seq // Q_BLOCK):
        t0 = blk * Q_BLOCK
        t1 = t0 + Q_BLOCK
        qb = q[:, :, t0:t1]
        kb = k[:, :, :t1]
        vb = v[:, :, :t1]
        z = jnp.einsum('bhqd,bhkd->bhqk', qb, kb).astype(jnp.float32) * scale
        t_idx = t0 + jnp.arange(Q_BLOCK, dtype=jnp.int32)[:, None]
        s_idx = jnp.arange(t1, dtype=jnp.int32)[None, :]
        causal = s_idx < t_idx
        log_1m_beta = jnp.where(causal, jax.nn.log_sigmoid(-z), 0.0)
        suffix = lax.cumsum(log_1m_beta, axis=3, reverse=True) - log_1m_beta
        att = jnp.where(causal, jnp.exp(jax.nn.log_sigmoid(z) + suffix), 0.0)
        outs.append(jnp.einsum('bhqk,bhkd->bhqd', att.astype(vb.dtype), vb))
    return jnp.concatenate(outs, axis=2)


def causal_depthwise_conv(x, w, b):
    c = x.shape[-1]
    y = lax.conv_general_dilated(x, w[:, None, :].astype(x.dtype), window_strides=(1,),
                                 padding=[(CONV_WIDTH - 1, 0)],
                                 dimension_numbers=('NWC', 'WIO', 'NWC'),
                                 feature_group_count=c)
    return y + b


def rg_lru(x, w_a, b_a, w_x, b_x, lam):
    bsz, seq, c = x.shape
    xb = x.reshape(bsz, seq, LRU_BLOCKS, LRU_BLOCK_DIM)
    r = jax.nn.sigmoid(jnp.einsum('bsnc,ncd->bsnd', xb, w_a).reshape(bsz, seq, c) + b_a)
    i = jax.nn.sigmoid(jnp.einsum('bsnc,ncd->bsnd', xb, w_x).reshape(bsz, seq, c) + b_x)
    log_a = -RG_C * r.astype(jnp.float32) * jax.nn.softplus(-lam.astype(jnp.float32))
    a = jnp.exp(log_a)
    mult = jnp.sqrt(-jnp.expm1(2.0 * log_a))
    bterm = mult * (i * x).astype(jnp.float32)

    def combine(left, right):
        a_l, b_l = left
        a_r, b_r = right
        return a_l * a_r, a_r * b_l + b_r

    _, h = lax.associative_scan(combine, (a, bterm), axis=1)
    return h.astype(x.dtype)


def mixer_block(u, w_in, gate_bias, conv_w, conv_b, w_rg_a, b_rg_a, w_rg_x, b_rg_x,
                lru_lambda, w_sb_o, w_lru_o, w_out):
    bsz, seq, _ = u.shape
    proj = u @ w_in
    q, k, v, x_rnn, y_rnn, g_a, g_b = jnp.split(proj, IN_SPLITS, axis=-1)

    def heads(t):
        return t.reshape(bsz, seq, SB_HEADS, SB_HEAD_DIM).transpose(0, 2, 1, 3)

    o_sb = stick_breaking_attention(heads(q), heads(k), heads(v))
    y_sb = o_sb.transpose(0, 2, 1, 3).reshape(bsz, seq, SB_WIDTH) @ w_sb_o

    h_rnn = rg_lru(causal_depthwise_conv(x_rnn, conv_w, conv_b),
                   w_rg_a, b_rg_a, w_rg_x, b_rg_x, lru_lambda)
    y_lru = (h_rnn * jax.nn.gelu(y_rnn)) @ w_lru_o

    bias_a, bias_b = jnp.split(gate_bias, 2)
    merged = jax.nn.sigmoid(g_a + bias_a) * y_sb + jax.nn.sigmoid(g_b + bias_b) * y_lru
    return merged @ w_out


def moe_ffn(u, w_router, b_router, w1, b1, w2, b2):
    bsz, seq, d = u.shape
    n = bsz * seq
    t = u.reshape(n, d)
    logits = (t @ w_router + b_router).astype(jnp.float32)
    top_vals, top_idx = lax.top_k(logits, TOP_K)
    gates = jax.nn.softmax(top_vals, axis=-1)

    nk = n * TOP_K
    e_flat = top_idx.reshape(nk).astype(jnp.int32)
    tok_flat = jnp.repeat(jnp.arange(n, dtype=jnp.int32), TOP_K)
    g_flat = gates.reshape(nk)
    order = jnp.argsort(e_flat)
    e_sorted = e_flat[order]
    counts = jnp.bincount(e_flat, length=N_EXPERTS).astype(jnp.int32)
    start = jnp.cumsum(counts) - counts
    padded = (counts + MOE_BLOCK - 1) // MOE_BLOCK * MOE_BLOCK
    pad_end = jnp.cumsum(padded)
    pad_start = pad_end - padded
    dest = pad_start[e_sorted] + (jnp.arange(nk, dtype=jnp.int32) - start[e_sorted])
    n_blocks = (nk + N_EXPERTS * (MOE_BLOCK - 1) + MOE_BLOCK - 1) // MOE_BLOCK
    n_slots = n_blocks * MOE_BLOCK
    slot_tok = jnp.full((n_slots,), n, jnp.int32).at[dest].set(tok_flat[order])
    slot_gate = jnp.zeros((n_slots,), jnp.float32).at[dest].set(g_flat[order])
    block_start = jnp.arange(n_blocks, dtype=jnp.int32) * MOE_BLOCK
    block_e = jnp.minimum(jnp.searchsorted(pad_end, block_start, side='right'),
                          N_EXPERTS - 1).astype(jnp.int32)
    t_pad = jnp.concatenate([t, jnp.zeros((1, d), t.dtype)], axis=0)

    def step(acc, blk):
        tok, gate, e = blk
        hid = t_pad[tok] @ w1[e] + b1[e]
        x_glu, x_lin = jnp.split(hid, 2, axis=-1)
        x_glu = jnp.minimum(x_glu, SWIGLU_LIMIT)
        x_lin = jnp.clip(x_lin, -SWIGLU_LIMIT, SWIGLU_LIMIT)
        act = x_glu * jax.nn.sigmoid(SWIGLU_ALPHA * x_glu) * (x_lin + 1.0)
        y = act @ w2[e] + b2[e]
        acc = acc.at[tok].add(gate[:, None] * y.astype(jnp.float32))
        return acc, None

    acc0 = jnp.zeros((n + 1, d), jnp.float32)
    acc, _ = lax.scan(step, acc0, (slot_tok.reshape(n_blocks, MOE_BLOCK),
                                   slot_gate.reshape(n_blocks, MOE_BLOCK), block_e))
    return acc[:n].reshape(bsz, seq, d).astype(u.dtype)


def setup_inputs(seed: int = 0) -> dict:
    key = jax.random.key(seed)
    ks = jax.random.split(key, 24)
    f32 = jnp.float32

    def nrm(k, shape, scale):
        return jax.random.normal(k, shape, f32) * scale

    u = jax.random.uniform(ks[8], (DEPTH, LRU_WIDTH), f32, 0.9, 0.999)
    a0 = u ** (1.0 / RG_C)
    lru_lambda = jnp.log(a0) - jnp.log1p(-a0)
    return {
        "x": nrm(ks[0], (BATCH, SEQ, D_MODEL), 1.0),
        "norm_mix_g": 1.0 + nrm(ks[1], (DEPTH, D_MODEL), 0.02),
        "w_in": nrm(ks[2], (DEPTH, D_MODEL, IN_COLS), D_MODEL ** -0.5),
        "gate_bias": nrm(ks[3], (DEPTH, 2 * D_MODEL), 0.1),
        "conv_w": nrm(ks[4], (DEPTH, CONV_WIDTH, LRU_WIDTH), CONV_WIDTH ** -0.5),
        "conv_b": nrm(ks[5], (DEPTH, LRU_WIDTH), 0.01),
        "w_rg_a": nrm(ks[6], (DEPTH, LRU_BLOCKS, LRU_BLOCK_DIM, LRU_BLOCK_DIM), LRU_BLOCK_DIM ** -0.5),
        "b_rg_a": nrm(ks[7], (DEPTH, LRU_WIDTH), 0.01),
        "w_rg_x": nrm(ks[9], (DEPTH, LRU_BLOCKS, LRU_BLOCK_DIM, LRU_BLOCK_DIM), LRU_BLOCK_DIM ** -0.5),
        "b_rg_x": nrm(ks[10], (DEPTH, LRU_WIDTH), 0.01),
        "lru_lambda": lru_lambda,
        "w_sb_o": nrm(ks[11], (DEPTH, SB_WIDTH, D_MODEL), SB_WIDTH ** -0.5),
        "w_lru_o": nrm(ks[12], (DEPTH, LRU_WIDTH, D_MODEL), LRU_WIDTH ** -0.5),
        "w_out": nrm(ks[13], (DEPTH, D_MODEL, D_MODEL), D_MODEL ** -0.5),
        "norm_ffn_g": 1.0 + nrm(ks[14], (DEPTH, D_MODEL), 0.02),
        "w_router": nrm(ks[15], (DEPTH, D_MODEL, N_EXPERTS), D_MODEL ** -0.5),
        "b_router": nrm(ks[16], (DEPTH, N_EXPERTS), 0.01),
        "w1": nrm(ks[17], (DEPTH, N_EXPERTS, D_MODEL, 2 * D_FF), D_MODEL ** -0.5),
        "b1": nrm(ks[18], (DEPTH, N_EXPERTS, 2 * D_FF), 0.01),
        "w2": nrm(ks[19], (DEPTH, N_EXPERTS, D_FF, D_MODEL), D_FF ** -0.5),
        "b2": nrm(ks[20], (DEPTH, N_EXPERTS, D_MODEL), 0.01),
        "norm_final_g": 1.0 + nrm(ks[21], (D_MODEL,), 0.02),
    }


def reference(x, norm_mix_g, w_in, gate_bias, conv_w, conv_b, w_rg_a, b_rg_a, w_rg_x, b_rg_x,
              lru_lambda, w_sb_o, w_lru_o, w_out, norm_ffn_g, w_router, b_router,
              w1, b1, w2, b2, norm_final_g):
    h = x
    for l in range(DEPTH):
        h = h + mixer_block(rms_norm(h, norm_mix_g[l]), w_in[l], gate_bias[l], conv_w[l], conv_b[l],
                            w_rg_a[l], b_rg_a[l], w_rg_x[l], b_rg_x[l], lru_lambda[l],
                            w_sb_o[l], w_lru_o[l], w_out[l])
        h = h + moe_ffn(rms_norm(h, norm_ffn_g[l]), w_router[l], b_router[l],
                        w1[l], b1[l], w2[l], b2[l])
    return rms_norm(h, norm_final_g)
```

```python
import functools

import jax
import jax.numpy as jnp
from jax import lax
from jax.experimental import pallas as pl
from jax.experimental.pallas import tpu as pltpu

F32 = jnp.float32
BF16 = jnp.bfloat16

SB_HEADS = 8
SB_HEAD_DIM = 64
SB_WIDTH = SB_HEADS * SB_HEAD_DIM
LRU_BLOCKS = 16
CONV_WIDTH = 4
RG_C = 8.0
N_EXPERTS = 32
TOP_K = 4
SWIGLU_ALPHA = 1.702
SWIGLU_LIMIT = 7.0
NORM_EPS = 1e-6

LANES = 128
SUBLANES = 8
MXU_DIM = 256
VMEM_LIMIT = 56 * 1024 * 1024

ATTN_BLOCK = 256
LRU_TIME_BLOCK = 128
MOE_ROWS = 512
COMBINE_ROWS = 256


def _rms(x, g):
    return x * lax.rsqrt(jnp.mean(x * x, axis=-1, keepdims=True) + NORM_EPS) * g


def _sigmoid(x):
    return 1.0 / (1.0 + jnp.exp(-x))


def _softplus(x):
    return jnp.maximum(x, 0.0) + jnp.log(1.0 + jnp.exp(-jnp.abs(x)))


def _dot(a, b):
    return jnp.dot(a, b, preferred_element_type=F32)


def _params(*sem):
    return pltpu.CompilerParams(dimension_semantics=sem, vmem_limit_bytes=VMEM_LIMIT)


def _resident(shape):
    return pl.BlockSpec(shape, lambda *_: (0,) * len(shape), pipeline_mode=pl.Buffered(1))


def _inproj_body(x_ref, g_ref, w_ref, qkv_ref, xy_ref, gg_ref, *, chunk):
    u = _rms(x_ref[...], g_ref[...]).astype(BF16)
    col = 0
    for ref in (qkv_ref, xy_ref, gg_ref):
        for c in range(0, ref.shape[1], chunk):
            ref[:, c:c + chunk] = _dot(u, w_ref[:, col + c:col + c + chunk]).astype(ref.dtype)
        col += ref.shape[1]


def _inproj(x2, g, w_in_bf, *, tm):
    n, d = x2.shape
    widths = (3 * SB_WIDTH, 2 * d, 2 * d)
    dtypes = (BF16, F32, F32)
    return pl.pallas_call(
        functools.partial(_inproj_body, chunk=512),
        grid=(n // tm,),
        in_specs=[pl.BlockSpec((tm, d), lambda i: (i, 0)),
                  _resident((1, d)),
                  _resident(w_in_bf.shape)],
        out_specs=[pl.BlockSpec((tm, w), lambda i: (i, 0)) for w in widths],
        out_shape=[jax.ShapeDtypeStruct((n, w), dt) for w, dt in zip(widths, dtypes)],
        compiler_params=_params("parallel"),
        name="inproj",
    )(x2, g, w_in_bf)


def _attn_body(q_ref, k_ref, v_ref, o_ref, *, tq):
    qi = pl.program_id(2)
    lane = lax.broadcasted_iota(jnp.int32, (1, LANES), 1)
    first = lane < SB_HEAD_DIM
    q = q_ref[0] * (SB_HEAD_DIM ** -0.5)
    zq = jnp.zeros_like(q)
    q_heads = (jnp.where(first, q, zq), jnp.where(first, zq, q))
    row = lax.broadcasted_iota(jnp.int32, (tq, tq), 0)
    col = lax.broadcasted_iota(jnp.int32, (tq, tq), 1)
    causal = col < row
    tri = jnp.where(row > col, 1.0, 0.0).astype(BF16)

    def block(j, carry, diag):
        o, c0, c1 = carry
        start = pl.multiple_of(j * tq, tq)
        kj = k_ref[0, pl.ds(start, tq), :]
        vj = v_ref[0, pl.ds(start, tq), :]
        zv = jnp.zeros_like(vj)
        v_heads = (jnp.where(first, vj, zv), jnp.where(first, zv, vj))
        cs = [c0, c1]
        for h in range(2):
            z = lax.dot_general(q_heads[h], kj, (((1,), (1,)), ((), ())), preferred_element_type=F32)
            sp = _softplus(z)
            if diag:
                sp = jnp.where(causal, sp, 0.0)
            sp_hi = sp.astype(BF16)
            sp_lo = (sp - sp_hi.astype(F32)).astype(BF16)
            suffix = _dot(sp_hi, tri) + _dot(sp_lo, tri)
            att = jnp.exp(z - sp - suffix - cs[h])
            if diag:
                att = jnp.where(causal, att, 0.0)
            o = o + _dot(att.astype(BF16), v_heads[h])
            cs[h] = cs[h] + jnp.sum(sp, axis=-1, keepdims=True)
        return o, cs[0], cs[1]

    init = (jnp.zeros((tq, LANES), F32), jnp.zeros((tq, 1), F32), jnp.zeros((tq, 1), F32))
    carry = block(qi, init, True)
    carry = lax.fori_loop(0, qi, lambda it, c: block(qi - 1 - it, c, False), carry)
    o_ref[0] = carry[0].astype(o_ref.dtype)


def _attention(qkv, *, bsz, seq, tq):
    qkv3 = qkv.reshape(bsz, seq, 3 * SB_WIDTH)
    pairs = SB_WIDTH // LANES
    return pl.pallas_call(
        functools.partial(_attn_body, tq=tq),
        grid=(bsz, pairs, seq // tq),
        in_specs=[pl.BlockSpec((1, tq, LANES), lambda b, p, i: (b, i, p)),
                  pl.BlockSpec((1, seq, LANES), lambda b, p, i: (b, 0, pairs + p)),
                  pl.BlockSpec((1, seq, LANES), lambda b, p, i: (b, 0, 2 * pairs + p))],
        out_specs=pl.BlockSpec((1, tq, LANES), lambda b, p, i: (b, i, p)),
        out_shape=jax.ShapeDtypeStruct((bsz, seq, SB_WIDTH), BF16),
        compiler_params=_params("parallel", "parallel", "arbitrary"),
        name="attn",
    )(qkv3, qkv3, qkv3)


def _lru_body(x_ref, y_ref, cw_ref, cb_ref, wa_ref, ba_ref, wx_ref, bx_ref, lam_ref, o_ref,
              xpad, a_s, b_s, h_s, *, ts, nb):
    s = pl.program_id(1)
    c = x_ref.shape[2]
    halo = SUBLANES

    @pl.when(s == 0)
    def _():
        xpad[:, 0:halo, :] = jnp.zeros((nb, halo, c), F32)
        h_s[...] = jnp.zeros_like(h_s)

    @pl.when(s > 0)
    def _():
        xpad[:, 0:halo, :] = xpad[:, ts:ts + halo, :]

    xpad[:, halo:halo + ts, :] = x_ref[...]
    xc = jnp.zeros((nb, ts, c), F32) + cb_ref[...]
    for i in range(CONV_WIDTH):
        off = halo - (CONV_WIDTH - 1) + i
        xc = xc + cw_ref[i:i + 1, :] * xpad[:, off:off + ts, :]
    xc = xc.reshape(nb * ts, c)

    neg_sp_lam = -RG_C * _softplus(-lam_ref[...])
    xb = xc.astype(BF16)
    for g in range(c // MXU_DIM):
        sl = slice(g * MXU_DIM, (g + 1) * MXU_DIM)
        r = _sigmoid(_dot(xb[:, sl], wa_ref[g]) + ba_ref[:, sl])
        gate_i = _sigmoid(_dot(xb[:, sl], wx_ref[g]) + bx_ref[:, sl])
        log_a = r * neg_sp_lam[:, sl]
        a = jnp.exp(log_a)
        b = jnp.sqrt(1.0 - jnp.exp(2.0 * log_a)) * (gate_i * xc[:, sl])
        for j in range(MXU_DIM // LANES):
            a_s[g * (MXU_DIM // LANES) + j] = a[:, j * LANES:(j + 1) * LANES]
            b_s[g * (MXU_DIM // LANES) + j] = b[:, j * LANES:(j + 1) * LANES]

    def step(t, hs):
        out = []
        for j in range(c // LANES):
            h = a_s[j, pl.ds(t, nb, stride=ts), :] * hs[j] + b_s[j, pl.ds(t, nb, stride=ts), :]
            b_s[j, pl.ds(t, nb, stride=ts), :] = h
            out.append(h)
        return tuple(out)

    hs = lax.fori_loop(0, ts, step, tuple(h_s[j] for j in range(c // LANES)))
    for j in range(c // LANES):
        h_s[j] = hs[j]

    for j in range(c // LANES):
        y = y_ref[:, :, j * LANES:(j + 1) * LANES].reshape(nb * ts, LANES)
        gelu = 0.5 * y * (1.0 + jnp.tanh(0.7978845608028654 * (y + 0.044715 * (y * y * y))))
        o_ref[:, :, j * LANES:(j + 1) * LANES] = (b_s[j] * gelu).astype(o_ref.dtype).reshape(nb, ts, LANES)


def _lru(xy, conv_w, conv_b, wa_bd, b_a, wx_bd, b_x, lam, *, bsz, seq, ts):
    c = xy.shape[-1] // 2
    nb = SUBLANES
    xy3 = xy.reshape(bsz, seq, 2 * c)
    return pl.pallas_call(
        functools.partial(_lru_body, ts=ts, nb=nb),
        grid=(bsz // nb, seq // ts),
        in_specs=[pl.BlockSpec((nb, ts, c), lambda b, s: (b, s, 0)),
                  pl.BlockSpec((nb, ts, c), lambda b, s: (b, s, 1)),
                  _resident(conv_w.shape), _resident((1, c)),
                  _resident(wa_bd.shape), _resident((1, c)),
                  _resident(wx_bd.shape), _resident((1, c)),
                  _resident((1, c))],
        out_specs=pl.BlockSpec((nb, ts, c), lambda b, s: (b, s, 0)),
        out_shape=jax.ShapeDtypeStruct((bsz, seq, c), BF16),
        scratch_shapes=[pltpu.VMEM((nb, ts + SUBLANES, c), F32),
                        pltpu.VMEM((c // LANES, nb * ts, LANES), F32),
                        pltpu.VMEM((c // LANES, nb * ts, LANES), F32),
                        pltpu.VMEM((c // LANES, nb, LANES), F32)],
        compiler_params=_params("parallel", "arbitrary"),
        name="lru",
    )(xy3, xy3, conv_w, conv_b, wa_bd, b_a, wx_bd, b_x, lam)


def _block_diag(w, group):
    n, k, _ = w.shape
    w = w.reshape(n // group, group, k, k)
    eye = jnp.eye(group, dtype=w.dtype)
    return jnp.einsum('gakl,ab->gakbl', w, eye).reshape(n // group, group * k, group * k)


def _merge_body(x_ref, osb_ref, hg_ref, gg_ref, gbias_ref, wsb_ref, wlru_ref, wout_ref, g2_ref,
                wr_ref, br_ref, h1_ref, u2_ref, idx_ref, rank_ref, gate_ref, cnt_ref, base_ref):
    i = pl.program_id(0)
    tm, d = x_ref.shape

    @pl.when(i == 0)
    def _():
        base_ref[...] = jnp.zeros_like(base_ref)

    y_sb = _dot(osb_ref[...], wsb_ref[...])
    y_lru = _dot(hg_ref[...], wlru_ref[...])
    g_a = _sigmoid(gg_ref[:, 0:d] + gbias_ref[:, 0:d])
    g_b = _sigmoid(gg_ref[:, d:2 * d] + gbias_ref[:, d:2 * d])
    merged = (g_a * y_sb + g_b * y_lru).astype(BF16)
    h1 = x_ref[...] + _dot(merged, wout_ref[...])
    h1_ref[...] = h1
    u2 = _rms(h1, g2_ref[...])
    u2_ref[...] = u2
    logits = _dot(u2.astype(BF16), wr_ref[...]) + br_ref[...]

    e_iota = lax.broadcasted_iota(jnp.int32, (tm, N_EXPERTS), 1)
    k_iota = lax.broadcasted_iota(jnp.int32, (tm, TOP_K), 1)
    vals = logits
    chosen = jnp.zeros((tm, N_EXPERTS), F32)
    top_v, top_i, hots = [], [], []
    for _ in range(TOP_K):
        m = jnp.max(vals, axis=-1, keepdims=True)
        sel = jnp.min(jnp.where(vals == m, e_iota, N_EXPERTS), axis=-1, keepdims=True)
        hot = e_iota == sel
        top_v.append(m)
        top_i.append(sel)
        hots.append(hot)
        chosen = jnp.where(hot, 1.0, chosen)
        vals = jnp.where(hot, -jnp.inf, vals)

    exps = [jnp.exp(v - top_v[0]) for v in top_v]
    denom = exps[0] + exps[1] + exps[2] + exps[3]

    r_iota = lax.broadcasted_iota(jnp.int32, (tm, tm), 0)
    c_iota = lax.broadcasted_iota(jnp.int32, (tm, tm), 1)
    lower = jnp.where(c_iota < r_iota, 1.0, 0.0).astype(BF16)
    before = _dot(lower, chosen.astype(BF16)) + base_ref[...]

    idx = jnp.zeros((tm, TOP_K), jnp.int32)
    rank = jnp.zeros((tm, TOP_K), jnp.int32)
    gate = jnp.zeros((tm, TOP_K), F32)
    for k in range(TOP_K):
        rk = jnp.sum(jnp.where(hots[k], before, 0.0), axis=-1, keepdims=True).astype(jnp.int32)
        idx = jnp.where(k_iota == k, top_i[k], idx)
        rank = jnp.where(k_iota == k, rk, rank)
        gate = jnp.where(k_iota == k, exps[k] / denom, gate)
    idx_ref[...] = idx
    rank_ref[...] = rank
    gate_ref[...] = gate

    total = base_ref[...] + jnp.sum(chosen, axis=0, keepdims=True)
    base_ref[...] = total
    cnt_ref[...] = total


def _merge(x2, o_sb, hg, gg, gate_bias, wsb, wlru, wout, g2, wr, br, *, tm):
    n, d = x2.shape
    row = lambda w: pl.BlockSpec((tm, w), lambda i: (i, 0))
    outs = [(d, F32), (d, F32), (TOP_K, jnp.int32), (TOP_K, jnp.int32), (TOP_K, F32)]
    return pl.pallas_call(
        _merge_body,
        grid=(n // tm,),
        in_specs=[row(d), row(SB_WIDTH), row(d), row(2 * d), _resident((1, 2 * d)),
                  _resident(wsb.shape), _resident(wlru.shape), _resident(wout.shape),
                  _resident((1, d)), _resident(wr.shape), _resident((1, N_EXPERTS))],
        out_specs=[row(w) for w, _ in outs] + [pl.BlockSpec((1, N_EXPERTS), lambda i: (0, 0))],
        out_shape=[jax.ShapeDtypeStruct((n, w), dt) for w, dt in outs]
                  + [jax.ShapeDtypeStruct((1, N_EXPERTS), F32)],
        scratch_shapes=[pltpu.VMEM((1, N_EXPERTS), F32)],
        compiler_params=_params("arbitrary"),
        name="merge",
    )(x2, o_sb, hg, gg, gate_bias, wsb, wlru, wout, g2, wr, br)


def _row_copy(src_hbm, row, dst, slot, sem):
    return pltpu.make_async_copy(src_hbm.at[pl.ds(row, 1)], dst.at[pl.ds(slot, 1)], sem)


def _experts_body(be_ref, nused_ref, tok_ref, u_hbm, w1_ref, b1_ref, w2_ref, b2_ref, y_ref,
                  xbuf, w1b, w2b, sem):
    i = pl.program_id(0)
    rows = xbuf.shape[0]
    ff = w2_ref.shape[1]

    @pl.when(i < nused_ref[0])
    def _():
        def issue(r, carry):
            _row_copy(u_hbm, tok_ref[0, 0, r], xbuf, r, sem).start()
            return carry

        lax.fori_loop(0, rows, issue, 0)

        changed = jnp.logical_or(i == 0, be_ref[i] != be_ref[jnp.maximum(i - 1, 0)])

        @pl.when(changed)
        def _():
            w1b[...] = w1_ref[0].astype(BF16)
            w2b[...] = w2_ref[0].astype(BF16)

        pltpu.make_async_copy(u_hbm.at[pl.ds(0, rows)], xbuf, sem).wait()

        hid = _dot(xbuf[...].astype(BF16), w1b[...]) + b1_ref[0]
        x_glu = jnp.minimum(hid[:, 0:ff], SWIGLU_LIMIT)
        x_lin = jnp.clip(hid[:, ff:2 * ff], -SWIGLU_LIMIT, SWIGLU_LIMIT)
        act = x_glu * _sigmoid(SWIGLU_ALPHA * x_glu) * (x_lin + 1.0)
        y_ref[...] = _dot(act.astype(BF16), w2b[...]) + b2_ref[0]

    @pl.when(i >= nused_ref[0])
    def _():
        y_ref[...] = jnp.zeros_like(y_ref)


def _experts(block_e, n_used, slot_tok3, u2, w1, b1, w2, b2, *, rows):
    n_blocks = slot_tok3.shape[0]
    _, d, ff2 = w1.shape
    ff = ff2 // 2
    grid_spec = pltpu.PrefetchScalarGridSpec(
        num_scalar_prefetch=2,
        grid=(n_blocks,),
        in_specs=[pl.BlockSpec((1, 1, rows), lambda i, be, nu: (i, 0, 0), memory_space=pltpu.SMEM),
                  pl.BlockSpec(memory_space=pl.ANY),
                  pl.BlockSpec((1, d, ff2), lambda i, be, nu: (be[i], 0, 0)),
                  pl.BlockSpec((1, 1, ff2), lambda i, be, nu: (be[i], 0, 0)),
                  pl.BlockSpec((1, ff, d), lambda i, be, nu: (be[i], 0, 0)),
                  pl.BlockSpec((1, 1, d), lambda i, be, nu: (be[i], 0, 0))],
        out_specs=pl.BlockSpec((rows, d), lambda i, be, nu: (i, 0)),
        scratch_shapes=[pltpu.VMEM((rows, d), F32),
                        pltpu.VMEM((d, ff2), BF16),
                        pltpu.VMEM((ff, d), BF16),
                        pltpu.SemaphoreType.DMA],
    )
    return pl.pallas_call(
        _experts_body,
        grid_spec=grid_spec,
        out_shape=jax.ShapeDtypeStruct((n_blocks * rows, d), F32),
        compiler_params=_params("arbitrary"),
        name="experts",
    )(block_e, n_used, slot_tok3, u2, w1, b1.reshape(N_EXPERTS, 1, ff2), w2, b2.reshape(N_EXPERTS, 1, d))


def _combine_body(dest_ref, gate_ref, h1_ref, g_ref, y_hbm, out_ref, ybuf, sem):
    tm = h1_ref.shape[0]

    def issue(n, carry):
        for k in range(TOP_K):
            _row_copy(y_hbm, dest_ref[0, 0, n * TOP_K + k], ybuf.at[k], n, sem).start()
        return carry

    lax.fori_loop(0, tm, issue, 0)
    for k in range(TOP_K):
        pltpu.make_async_copy(y_hbm.at[pl.ds(0, tm)], ybuf.at[k], sem).wait()

    acc = h1_ref[...]
    gate = gate_ref[...]
    for k in range(TOP_K):
        acc = acc + gate[:, k:k + 1] * ybuf[k]
    out_ref[...] = _rms(acc, g_ref[...])


def _combine(dest3, gate, h1, g, y, *, tm):
    n, d = h1.shape
    return pl.pallas_call(
        _combine_body,
        grid=(n // tm,),
        in_specs=[pl.BlockSpec((1, 1, tm * TOP_K), lambda i: (i, 0, 0), memory_space=pltpu.SMEM),
                  pl.BlockSpec((tm, TOP_K), lambda i: (i, 0)),
                  pl.BlockSpec((tm, d), lambda i: (i, 0)),
                  _resident((1, d)),
                  pl.BlockSpec(memory_space=pl.ANY)],
        out_specs=pl.BlockSpec((tm, d), lambda i: (i, 0)),
        out_shape=jax.ShapeDtypeStruct((n, d), F32),
        scratch_shapes=[pltpu.VMEM((TOP_K, tm, d), F32), pltpu.SemaphoreType.DMA],
        compiler_params=_params("arbitrary"),
        name="combine",
    )(dest3, gate, h1, g, y)


def _layer(h2, bsz, seq, norm_mix_g, w_in, gate_bias, conv_w, conv_b, w_rg_a, b_rg_a, w_rg_x, b_rg_x,
           lru_lambda, w_sb_o, w_lru_o, w_out, norm_ffn_g, w_router, b_router, w1, b1, w2, b2, out_g):
    n, d = h2.shape
    tm = min(512, n)
    qkv, xy, gg = _inproj(h2, norm_mix_g.reshape(1, d), w_in.astype(BF16), tm=tm)
    o_sb = _attention(qkv, bsz=bsz, seq=seq, tq=min(ATTN_BLOCK, seq)).reshape(n, SB_WIDTH)
    group = MXU_DIM // (d // LRU_BLOCKS)
    hg = _lru(xy, conv_w, conv_b.reshape(1, d),
              _block_diag(w_rg_a, group).astype(BF16), b_rg_a.reshape(1, d),
              _block_diag(w_rg_x, group).astype(BF16), b_rg_x.reshape(1, d),
              lru_lambda.reshape(1, d), bsz=bsz, seq=seq, ts=min(LRU_TIME_BLOCK, seq)).reshape(n, d)
    h1, u2, idx, rank, gate, counts = _merge(
        h2, o_sb, hg, gg, gate_bias.reshape(1, 2 * d), w_sb_o.astype(BF16), w_lru_o.astype(BF16),
        w_out.astype(BF16), norm_ffn_g.reshape(1, d), w_router.astype(BF16),
        b_router.reshape(1, N_EXPERTS), tm=tm)

    rows = MOE_ROWS
    nk = n * TOP_K
    n_blocks = (nk + N_EXPERTS * (rows - 1) + rows - 1) // rows
    counts = counts.reshape(N_EXPERTS).astype(jnp.int32)
    padded = (counts + rows - 1) // rows * rows
    pad_end = jnp.cumsum(padded)
    pad_start = pad_end - padded
    dest = pad_start[idx] + rank
    tok = jnp.broadcast_to(jnp.arange(n, dtype=jnp.int32)[:, None], (n, TOP_K))
    slot_tok = jnp.zeros((n_blocks * rows,), jnp.int32).at[dest.reshape(nk)].set(tok.reshape(nk))
    block_start = jnp.arange(n_blocks, dtype=jnp.int32) * rows
    block_e = jnp.minimum(jnp.searchsorted(pad_end, block_start, side='right'),
                          N_EXPERTS - 1).astype(jnp.int32)
    n_used = (pad_end[-1:] // rows).astype(jnp.int32)

    y = _experts(block_e, n_used, slot_tok.reshape(n_blocks, 1, rows), u2, w1, b1, w2, b2, rows=rows)
    tc = min(COMBINE_ROWS, n)
    return _combine(dest.reshape(n // tc, 1, tc * TOP_K), gate, h1, out_g.reshape(1, d), y, tm=tc)


def kernel(x, norm_mix_g, w_in, gate_bias, conv_w, conv_b, w_rg_a, b_rg_a, w_rg_x, b_rg_x, lru_lambda,
           w_sb_o, w_lru_o, w_out, norm_ffn_g, w_router, b_router, w1, b1, w2, b2, norm_final_g):
    bsz, seq, d = x.shape
    depth = w_in.shape[0]
    assert depth == 1, "the final norm is fused into the last layer's combine"
    assert bsz % SUBLANES == 0 and d % MXU_DIM == 0
    h2 = x.reshape(bsz * seq, d)
    out = _layer(h2, bsz, seq, norm_mix_g[0], w_in[0], gate_bias[0], conv_w[0], conv_b[0], w_rg_a[0],
                 b_rg_a[0], w_rg_x[0], b_rg_x[0], lru_lambda[0], w_sb_o[0], w_lru_o[0], w_out[0],
                 norm_ffn_g[0], w_router[0], b_router[0], w1[0], b1[0], w2[0], b2[0], norm_final_g)
    return out.reshape(bsz, seq, d)
```

```python
import functools

import jax
import jax.numpy as jnp
from jax import lax
from jax.experimental import pallas as pl
from jax.experimental.pallas import tpu as pltpu

F32 = jnp.float32
BF16 = jnp.bfloat16

SB_HEADS = 8
SB_HEAD_DIM = 64
SB_WIDTH = SB_HEADS * SB_HEAD_DIM
LRU_BLOCKS = 16
CONV_WIDTH = 4
RG_C = 8.0
N_EXPERTS = 32
TOP_K = 4
SWIGLU_ALPHA = 1.702
SWIGLU_LIMIT = 7.0
NORM_EPS = 1e-6

LANES = 128
SUBLANES = 8
MXU_DIM = 256
VMEM_LIMIT = 56 * 1024 * 1024

ATTN_BLOCK = 256
ATTN_DEAD_CARRY = 104.0
LRU_TIME_BLOCK = 128
MOE_ROWS = 512
COMBINE_ROWS = 256


def _rms(x, g):
    return x * lax.rsqrt(jnp.mean(x * x, axis=-1, keepdims=True) + NORM_EPS) * g


def _sigmoid(x):
    return 1.0 / (1.0 + jnp.exp(-x))


def _softplus(x):
    return jnp.maximum(x, 0.0) + jnp.log(1.0 + jnp.exp(-jnp.abs(x)))


def _dot(a, b):
    return jnp.dot(a, b, preferred_element_type=F32)


def _params(*sem):
    return pltpu.CompilerParams(dimension_semantics=sem, vmem_limit_bytes=VMEM_LIMIT)


def _resident(shape):
    return pl.BlockSpec(shape, lambda *_: (0,) * len(shape), pipeline_mode=pl.Buffered(1))


def _inproj_body(x_ref, g_ref, w_ref, qkv_ref, xy_ref, gg_ref, *, chunk):
    u = _rms(x_ref[...], g_ref[...]).astype(BF16)
    col = 0
    for ref in (qkv_ref, xy_ref, gg_ref):
        for c in range(0, ref.shape[1], chunk):
            ref[:, c:c + chunk] = _dot(u, w_ref[:, col + c:col + c + chunk]).astype(ref.dtype)
        col += ref.shape[1]


def _inproj(x2, g, w_in_bf, *, tm):
    n, d = x2.shape
    widths = (3 * SB_WIDTH, 2 * d, 2 * d)
    dtypes = (BF16, F32, F32)
    return pl.pallas_call(
        functools.partial(_inproj_body, chunk=512),
        grid=(n // tm,),
        in_specs=[pl.BlockSpec((tm, d), lambda i: (i, 0)),
                  _resident((1, d)),
                  _resident(w_in_bf.shape)],
        out_specs=[pl.BlockSpec((tm, w), lambda i: (i, 0)) for w in widths],
        out_shape=[jax.ShapeDtypeStruct((n, w), dt) for w, dt in zip(widths, dtypes)],
        compiler_params=_params("parallel"),
        name="inproj",
    )(x2, g, w_in_bf)


def _attn_body(q_ref, k_ref, v_ref, o_ref, o_acc, c_acc, *, tq):
    qi = pl.program_id(2)
    lane = lax.broadcasted_iota(jnp.int32, (1, LANES), 1)
    first = lane < SB_HEAD_DIM
    q = q_ref[0] * (SB_HEAD_DIM ** -0.5)
    zq = jnp.zeros_like(q)
    q_heads = (jnp.where(first, q, zq), jnp.where(first, zq, q))
    row = lax.broadcasted_iota(jnp.int32, (tq, tq), 0)
    col = lax.broadcasted_iota(jnp.int32, (tq, tq), 1)
    causal = col < row
    tri = jnp.where(row > col, 1.0, 0.0).astype(BF16)

    def block(j, diag):
        start = pl.multiple_of(j * tq, tq)
        kj = k_ref[0, pl.ds(start, tq), :]
        vj = v_ref[0, pl.ds(start, tq), :]
        zv = jnp.zeros_like(vj)
        v_heads = (jnp.where(first, vj, zv), jnp.where(first, zv, vj))
        o = None if diag else o_acc[...]
        low = None
        for h in range(2):
            z = lax.dot_general(q_heads[h], kj, (((1,), (1,)), ((), ())), preferred_element_type=F32)
            sp = _softplus(z)
            if diag:
                sp = jnp.where(causal, sp, 0.0)
            sp_hi = sp.astype(BF16)
            sp_lo = (sp - sp_hi.astype(F32)).astype(BF16)
            suffix = _dot(sp_hi, tri) + _dot(sp_lo, tri)
            arg = z - sp - suffix
            if not diag:
                arg = arg - c_acc[h]
            att = jnp.exp(arg)
            if diag:
                att = jnp.where(causal, att, 0.0)
            pv = _dot(att.astype(BF16), v_heads[h])
            o = pv if o is None else o + pv
            c = jnp.sum(sp, axis=-1, keepdims=True)
            if not diag:
                c = c + c_acc[h]
            c_acc[h] = c
            m = jnp.min(c)
            low = m if low is None else jnp.minimum(low, m)
        o_acc[...] = o
        return low

    def more(state):
        j, low = state
        return jnp.logical_and(j >= 0, low < ATTN_DEAD_CARRY)

    def step(state):
        j, _ = state
        return j - 1, block(j, False)

    lax.while_loop(more, step, (qi - 1, block(qi, True)))
    o_ref[0] = o_acc[...].astype(o_ref.dtype)


def _attention(qkv, *, bsz, seq, tq):
    qkv3 = qkv.reshape(bsz, seq, 3 * SB_WIDTH)
    pairs = SB_WIDTH // LANES
    return pl.pallas_call(
        functools.partial(_attn_body, tq=tq),
        grid=(bsz, pairs, seq // tq),
        in_specs=[pl.BlockSpec((1, tq, LANES), lambda b, p, i: (b, i, p)),
                  pl.BlockSpec((1, seq, LANES), lambda b, p, i: (b, 0, pairs + p)),
                  pl.BlockSpec((1, seq, LANES), lambda b, p, i: (b, 0, 2 * pairs + p))],
        out_specs=pl.BlockSpec((1, tq, LANES), lambda b, p, i: (b, i, p)),
        out_shape=jax.ShapeDtypeStruct((bsz, seq, SB_WIDTH), BF16),
        scratch_shapes=[pltpu.VMEM((tq, LANES), F32), pltpu.VMEM((2, tq, 1), F32)],
        compiler_params=_params("parallel", "parallel", "arbitrary"),
        name="attn",
    )(qkv3, qkv3, qkv3)


def _lru_body(x_ref, y_ref, cw_ref, cb_ref, wa_ref, ba_ref, wx_ref, bx_ref, lam_ref, o_ref,
              xpad, a_s, b_s, h_s, *, ts, nb):
    s = pl.program_id(1)
    c = x_ref.shape[2]
    halo = SUBLANES

    @pl.when(s == 0)
    def _():
        xpad[:, 0:halo, :] = jnp.zeros((nb, halo, c), F32)
        h_s[...] = jnp.zeros_like(h_s)

    @pl.when(s > 0)
    def _():
        xpad[:, 0:halo, :] = xpad[:, ts:ts + halo, :]

    xpad[:, halo:halo + ts, :] = x_ref[...]
    xc = jnp.zeros((nb, ts, c), F32) + cb_ref[...]
    for i in range(CONV_WIDTH):
        off = halo - (CONV_WIDTH - 1) + i
        xc = xc + cw_ref[i:i + 1, :] * xpad[:, off:off + ts, :]
    xc = xc.reshape(nb * ts, c)

    neg_sp_lam = -RG_C * _softplus(-lam_ref[...])
    xb = xc.astype(BF16)
    for g in range(c // MXU_DIM):
        sl = slice(g * MXU_DIM, (g + 1) * MXU_DIM)
        r = _sigmoid(_dot(xb[:, sl], wa_ref[g]) + ba_ref[:, sl])
        gate_i = _sigmoid(_dot(xb[:, sl], wx_ref[g]) + bx_ref[:, sl])
        log_a = r * neg_sp_lam[:, sl]
        a = jnp.exp(log_a)
        b = jnp.sqrt(1.0 - jnp.exp(2.0 * log_a)) * (gate_i * xc[:, sl])
        for j in range(MXU_DIM // LANES):
            a_s[g * (MXU_DIM // LANES) + j] = a[:, j * LANES:(j + 1) * LANES]
            b_s[g * (MXU_DIM // LANES) + j] = b[:, j * LANES:(j + 1) * LANES]

    def step(t, hs):
        out = []
        for j in range(c // LANES):
            h = a_s[j, pl.ds(t, nb, stride=ts), :] * hs[j] + b_s[j, pl.ds(t, nb, stride=ts), :]
            b_s[j, pl.ds(t, nb, stride=ts), :] = h
            out.append(h)
        return tuple(out)

    hs = lax.fori_loop(0, ts, step, tuple(h_s[j] for j in range(c // LANES)))
    for j in range(c // LANES):
        h_s[j] = hs[j]

    for j in range(c // LANES):
        y = y_ref[:, :, j * LANES:(j + 1) * LANES].reshape(nb * ts, LANES)
        gelu = 0.5 * y * (1.0 + jnp.tanh(0.7978845608028654 * (y + 0.044715 * (y * y * y))))
        o_ref[:, :, j * LANES:(j + 1) * LANES] = (b_s[j] * gelu).astype(o_ref.dtype).reshape(nb, ts, LANES)


def _lru(xy, conv_w, conv_b, wa_bd, b_a, wx_bd, b_x, lam, *, bsz, seq, ts):
    c = xy.shape[-1] // 2
    nb = SUBLANES
    xy3 = xy.reshape(bsz, seq, 2 * c)
    return pl.pallas_call(
        functools.partial(_lru_body, ts=ts, nb=nb),
        grid=(bsz // nb, seq // ts),
        in_specs=[pl.BlockSpec((nb, ts, c), lambda b, s: (b, s, 0)),
                  pl.BlockSpec((nb, ts, c), lambda b, s: (b, s, 1)),
                  _resident(conv_w.shape), _resident((1, c)),
                  _resident(wa_bd.shape), _resident((1, c)),
                  _resident(wx_bd.shape), _resident((1, c)),
                  _resident((1, c))],
        out_specs=pl.BlockSpec((nb, ts, c), lambda b, s: (b, s, 0)),
        out_shape=jax.ShapeDtypeStruct((bsz, seq, c), BF16),
        scratch_shapes=[pltpu.VMEM((nb, ts + SUBLANES, c), F32),
                        pltpu.VMEM((c // LANES, nb * ts, LANES), F32),
                        pltpu.VMEM((c // LANES, nb * ts, LANES), F32),
                        pltpu.VMEM((c // LANES, nb, LANES), F32)],
        compiler_params=_params("parallel", "arbitrary"),
        name="lru",
    )(xy3, xy3, conv_w, conv_b, wa_bd, b_a, wx_bd, b_x, lam)


def _block_diag(w, group):
    n, k, _ = w.shape
    w = w.reshape(n // group, group, k, k)
    eye = jnp.eye(group, dtype=w.dtype)
    return jnp.einsum('gakl,ab->gakbl', w, eye).reshape(n // group, group * k, group * k)


def _merge_body(x_ref, osb_ref, hg_ref, gg_ref, gbias_ref, wsb_ref, wlru_ref, wout_ref, g2_ref,
                wr_ref, br_ref, h1_ref, u2_ref, idx_ref, rank_ref, gate_ref, cnt_ref, base_ref):
    i = pl.program_id(0)
    tm, d = x_ref.shape

    @pl.when(i == 0)
    def _():
        base_ref[...] = jnp.zeros_like(base_ref)

    y_sb = _dot(osb_ref[...], wsb_ref[...])
    y_lru = _dot(hg_ref[...], wlru_ref[...])
    g_a = _sigmoid(gg_ref[:, 0:d] + gbias_ref[:, 0:d])
    g_b = _sigmoid(gg_ref[:, d:2 * d] + gbias_ref[:, d:2 * d])
    merged = (g_a * y_sb + g_b * y_lru).astype(BF16)
    h1 = x_ref[...] + _dot(merged, wout_ref[...])
    h1_ref[...] = h1
    u2 = _rms(h1, g2_ref[...])
    u2_ref[...] = u2
    logits = _dot(u2.astype(BF16), wr_ref[...]) + br_ref[...]

    e_iota = lax.broadcasted_iota(jnp.int32, (tm, N_EXPERTS), 1)
    k_iota = lax.broadcasted_iota(jnp.int32, (tm, TOP_K), 1)
    vals = logits
    chosen = jnp.zeros((tm, N_EXPERTS), F32)
    top_v, top_i, hots = [], [], []
    for _ in range(TOP_K):
        m = jnp.max(vals, axis=-1, keepdims=True)
        sel = jnp.min(jnp.where(vals == m, e_iota, N_EXPERTS), axis=-1, keepdims=True)
        hot = e_iota == sel
        top_v.append(m)
        top_i.append(sel)
        hots.append(hot)
        chosen = jnp.where(hot, 1.0, chosen)
        vals = jnp.where(hot, -jnp.inf, vals)

    exps = [jnp.exp(v - top_v[0]) for v in top_v]
    denom = exps[0] + exps[1] + exps[2] + exps[3]

    r_iota = lax.broadcasted_iota(jnp.int32, (tm, tm), 0)
    c_iota = lax.broadcasted_iota(jnp.int32, (tm, tm), 1)
    lower = jnp.where(c_iota < r_iota, 1.0, 0.0).astype(BF16)
    before = _dot(lower, chosen.astype(BF16)) + base_ref[...]

    idx = jnp.zeros((tm, TOP_K), jnp.int32)
    rank = jnp.zeros((tm, TOP_K), jnp.int32)
    gate = jnp.zeros((tm, TOP_K), F32)
    for k in range(TOP_K):
        rk = jnp.sum(jnp.where(hots[k], before, 0.0), axis=-1, keepdims=True).astype(jnp.int32)
        idx = jnp.where(k_iota == k, top_i[k], idx)
        rank = jnp.where(k_iota == k, rk, rank)
        gate = jnp.where(k_iota == k, exps[k] / denom, gate)
    idx_ref[...] = idx
    rank_ref[...] = rank
    gate_ref[...] = gate

    total = base_ref[...] + jnp.sum(chosen, axis=0, keepdims=True)
    base_ref[...] = total
    cnt_ref[...] = total


def _merge(x2, o_sb, hg, gg, gate_bias, wsb, wlru, wout, g2, wr, br, *, tm):
    n, d = x2.shape
    row = lambda w: pl.BlockSpec((tm, w), lambda i: (i, 0))
    outs = [(d, F32), (d, F32), (TOP_K, jnp.int32), (TOP_K, jnp.int32), (TOP_K, F32)]
    return pl.pallas_call(
        _merge_body,
        grid=(n // tm,),
        in_specs=[row(d), row(SB_WIDTH), row(d), row(2 * d), _resident((1, 2 * d)),
                  _resident(wsb.shape), _resident(wlru.shape), _resident(wout.shape),
                  _resident((1, d)), _resident(wr.shape), _resident((1, N_EXPERTS))],
        out_specs=[row(w) for w, _ in outs] + [pl.BlockSpec((1, N_EXPERTS), lambda i: (0, 0))],
        out_shape=[jax.ShapeDtypeStruct((n, w), dt) for w, dt in outs]
                  + [jax.ShapeDtypeStruct((1, N_EXPERTS), F32)],
        scratch_shapes=[pltpu.VMEM((1, N_EXPERTS), F32)],
        compiler_params=_params("arbitrary"),
        name="merge",
    )(x2, o_sb, hg, gg, gate_bias, wsb, wlru, wout, g2, wr, br)


def _row_copy(src_hbm, row, dst, slot, sem):
    return pltpu.make_async_copy(src_hbm.at[pl.ds(row, 1)], dst.at[pl.ds(slot, 1)], sem)


def _experts_body(be_ref, nused_ref, tok_ref, nxt_ref, u_hbm, w1_ref, b1_ref, w2_ref, b2_ref, y_ref,
                  xbuf, w1b, w2b, sem):
    i = pl.program_id(0)
    rows = xbuf.shape[1]
    ff = w2_ref.shape[1]
    n_used = nused_ref[0]
    slot = i % 2

    def gather_wait(s):
        pltpu.make_async_copy(u_hbm.at[pl.ds(0, rows)], xbuf.at[s], sem.at[s]).wait()

    @pl.when(i == 0)
    def _():
        def issue(r, carry):
            _row_copy(u_hbm, tok_ref[0, 0, r], xbuf.at[0], r, sem.at[0]).start()
            return carry

        lax.fori_loop(0, rows, issue, 0)

    @pl.when(jnp.logical_and(i < n_used,
                             jnp.logical_or(i == 0, be_ref[i] != be_ref[jnp.maximum(i - 1, 0)])))
    def _():
        w1b[...] = w1_ref[0].astype(BF16)
        w2b[...] = w2_ref[0].astype(BF16)

    @pl.when(i < n_used)
    def _():
        for r in range(rows):
            _row_copy(u_hbm, nxt_ref[0, 0, r], xbuf.at[1 - slot], r, sem.at[1 - slot]).start()
        gather_wait(slot)
        hid = _dot(xbuf[slot].astype(BF16), w1b[...]) + b1_ref[0]
        x_glu = jnp.minimum(hid[:, 0:ff], SWIGLU_LIMIT)
        x_lin = jnp.clip(hid[:, ff:2 * ff], -SWIGLU_LIMIT, SWIGLU_LIMIT)
        act = x_glu * _sigmoid(SWIGLU_ALPHA * x_glu) * (x_lin + 1.0)
        y_ref[...] = _dot(act.astype(BF16), w2b[...]) + b2_ref[0]

    @pl.when(i >= n_used)
    def _():
        y_ref[...] = jnp.zeros_like(y_ref)

    @pl.when(i == n_used)
    def _():
        gather_wait(slot)


def _experts(block_e, n_used, slot_tok3, u2, w1, b1, w2, b2, *, rows):
    n_steps = slot_tok3.shape[0]
    _, d, ff2 = w1.shape
    ff = ff2 // 2
    last = n_steps - 1
    grid_spec = pltpu.PrefetchScalarGridSpec(
        num_scalar_prefetch=2,
        grid=(n_steps,),
        in_specs=[pl.BlockSpec((1, 1, rows), lambda i, be, nu: (i, 0, 0), memory_space=pltpu.SMEM),
                  pl.BlockSpec((1, 1, rows), lambda i, be, nu: (jnp.minimum(i + 1, last), 0, 0),
                               memory_space=pltpu.SMEM),
                  pl.BlockSpec(memory_space=pl.ANY),
                  pl.BlockSpec((1, d, ff2), lambda i, be, nu: (be[i], 0, 0)),
                  pl.BlockSpec((1, 1, ff2), lambda i, be, nu: (be[i], 0, 0)),
                  pl.BlockSpec((1, ff, d), lambda i, be, nu: (be[i], 0, 0)),
                  pl.BlockSpec((1, 1, d), lambda i, be, nu: (be[i], 0, 0))],
        out_specs=pl.BlockSpec((rows, d), lambda i, be, nu: (i, 0)),
        scratch_shapes=[pltpu.VMEM((2, rows, d), F32),
                        pltpu.VMEM((d, ff2), BF16),
                        pltpu.VMEM((ff, d), BF16),
                        pltpu.SemaphoreType.DMA((2,))],
    )
    return pl.pallas_call(
        _experts_body,
        grid_spec=grid_spec,
        out_shape=jax.ShapeDtypeStruct((n_steps * rows, d), F32),
        compiler_params=_params("arbitrary"),
        name="experts",
    )(block_e, n_used, slot_tok3, slot_tok3, u2, w1, b1.reshape(N_EXPERTS, 1, ff2), w2,
      b2.reshape(N_EXPERTS, 1, d))


def _combine_body(dest_ref, gate_ref, h1_ref, g_ref, y_hbm, out_ref, ybuf, sem):
    tm = h1_ref.shape[0]

    def issue(n, carry):
        for k in range(TOP_K):
            _row_copy(y_hbm, dest_ref[0, 0, n * TOP_K + k], ybuf.at[k], n, sem).start()
        return carry

    lax.fori_loop(0, tm, issue, 0)
    for k in range(TOP_K):
        pltpu.make_async_copy(y_hbm.at[pl.ds(0, tm)], ybuf.at[k], sem).wait()

    acc = h1_ref[...]
    gate = gate_ref[...]
    for k in range(TOP_K):
        acc = acc + gate[:, k:k + 1] * ybuf[k]
    out_ref[...] = _rms(acc, g_ref[...])


def _combine(dest3, gate, h1, g, y, *, tm):
    n, d = h1.shape
    return pl.pallas_call(
        _combine_body,
        grid=(n // tm,),
        in_specs=[pl.BlockSpec((1, 1, tm * TOP_K), lambda i: (i, 0, 0), memory_space=pltpu.SMEM),
                  pl.BlockSpec((tm, TOP_K), lambda i: (i, 0)),
                  pl.BlockSpec((tm, d), lambda i: (i, 0)),
                  _resident((1, d)),
                  pl.BlockSpec(memory_space=pl.ANY)],
        out_specs=pl.BlockSpec((tm, d), lambda i: (i, 0)),
        out_shape=jax.ShapeDtypeStruct((n, d), F32),
        scratch_shapes=[pltpu.VMEM((TOP_K, tm, d), F32), pltpu.SemaphoreType.DMA],
        compiler_params=_params("arbitrary"),
        name="combine",
    )(dest3, gate, h1, g, y)


def _layer(h2, bsz, seq, norm_mix_g, w_in, gate_bias, conv_w, conv_b, w_rg_a, b_rg_a, w_rg_x, b_rg_x,
           lru_lambda, w_sb_o, w_lru_o, w_out, norm_ffn_g, w_router, b_router, w1, b1, w2, b2, out_g):
    n, d = h2.shape
    tm = min(512, n)
    qkv, xy, gg = _inproj(h2, norm_mix_g.reshape(1, d), w_in.astype(BF16), tm=tm)
    o_sb = _attention(qkv, bsz=bsz, seq=seq, tq=min(ATTN_BLOCK, seq)).reshape(n, SB_WIDTH)
    group = MXU_DIM // (d // LRU_BLOCKS)
    hg = _lru(xy, conv_w, conv_b.reshape(1, d),
              _block_diag(w_rg_a, group).astype(BF16), b_rg_a.reshape(1, d),
              _block_diag(w_rg_x, group).astype(BF16), b_rg_x.reshape(1, d),
              lru_lambda.reshape(1, d), bsz=bsz, seq=seq, ts=min(LRU_TIME_BLOCK, seq)).reshape(n, d)
    h1, u2, idx, rank, gate, counts = _merge(
        h2, o_sb, hg, gg, gate_bias.reshape(1, 2 * d), w_sb_o.astype(BF16), w_lru_o.astype(BF16),
        w_out.astype(BF16), norm_ffn_g.reshape(1, d), w_router.astype(BF16),
        b_router.reshape(1, N_EXPERTS), tm=tm)

    rows = MOE_ROWS
    nk = n * TOP_K
    n_steps = (nk + N_EXPERTS * (rows - 1)) // rows + 1
    counts = counts.reshape(N_EXPERTS).astype(jnp.int32)
    padded = (counts + rows - 1) // rows * rows
    pad_end = jnp.cumsum(padded)
    pad_start = pad_end - padded
    dest = pad_start[idx] + rank
    tok = jnp.broadcast_to(jnp.arange(n, dtype=jnp.int32)[:, None], (n, TOP_K))
    slot_tok = jnp.zeros((n_steps * rows,), jnp.int32).at[dest.reshape(nk)].set(tok.reshape(nk))
    block_start = jnp.arange(n_steps, dtype=jnp.int32) * rows
    block_e = jnp.minimum(jnp.sum((block_start[:, None] >= pad_end[None, :]).astype(jnp.int32), axis=1),
                          N_EXPERTS - 1)
    n_used = (pad_end[-1:] // rows).astype(jnp.int32)

    y = _experts(block_e, n_used, slot_tok.reshape(n_steps, 1, rows), u2, w1, b1, w2, b2, rows=rows)
    tc = min(COMBINE_ROWS, n)
    return _combine(dest.reshape(n // tc, 1, tc * TOP_K), gate, h1, out_g.reshape(1, d), y, tm=tc)


def kernel(x, norm_mix_g, w_in, gate_bias, conv_w, conv_b, w_rg_a, b_rg_a, w_rg_x, b_rg_x, lru_lambda,
           w_sb_o, w_lru_o, w_out, norm_ffn_g, w_router, b_router, w1, b1, w2, b2, norm_final_g):
    bsz, seq, d = x.shape
    depth = w_in.shape[0]
    assert depth == 1, "the final norm is fused into the last layer's combine"
    assert bsz % SUBLANES == 0 and d % MXU_DIM == 0
    h2 = x.reshape(bsz * seq, d)
    out = _layer(h2, bsz, seq, norm_mix_g[0], w_in[0], gate_bias[0], conv_w[0], conv_b[0], w_rg_a[0],
                 b_rg_a[0], w_rg_x[0], b_rg_x[0], lru_lambda[0], w_sb_o[0], w_lru_o[0], w_out[0],
                 norm_ffn_g[0], w_router[0], b_router[0], w1[0], b1[0], w2[0], b2[0], norm_final_g)
    return out.reshape(bsz, seq, d)
```

```python
import functools

import jax
import jax.numpy as jnp
from jax import lax
from jax.experimental import pallas as pl
from jax.experimental.pallas import tpu as pltpu

F32 = jnp.float32
BF16 = jnp.bfloat16

SB_HEADS = 8
SB_HEAD_DIM = 64
SB_WIDTH = SB_HEADS * SB_HEAD_DIM
LRU_BLOCKS = 16
CONV_WIDTH = 4
RG_C = 8.0
N_EXPERTS = 32
TOP_K = 4
SWIGLU_ALPHA = 1.702
SWIGLU_LIMIT = 7.0
NORM_EPS = 1e-6

LANES = 128
SUBLANES = 8
MXU_DIM = 256
VMEM_LIMIT = 56 * 1024 * 1024

ATTN_BLOCK = 256
ATTN_DEAD_CARRY = 104.0
LRU_TIME_BLOCK = 128
LRU_PITCH_PAD = 8
MOE_ROWS = 512
COMBINE_ROWS = 512


def _rms(x, g):
    return x * lax.rsqrt(jnp.mean(x * x, axis=-1, keepdims=True) + NORM_EPS) * g


def _sigmoid(x):
    return 1.0 / (1.0 + jnp.exp(-x))


def _softplus(x):
    return jnp.maximum(x, 0.0) + jnp.log(1.0 + jnp.exp(-jnp.abs(x)))


def _dot(a, b):
    return jnp.dot(a, b, preferred_element_type=F32)


def _params(*sem):
    return pltpu.CompilerParams(dimension_semantics=sem, vmem_limit_bytes=VMEM_LIMIT)


def _resident(shape):
    return pl.BlockSpec(shape, lambda *_: (0,) * len(shape), pipeline_mode=pl.Buffered(1))


def _inproj_body(x_ref, g_ref, w_ref, qkv_ref, xy_ref, gg_ref, *, chunk):
    u = _rms(x_ref[...], g_ref[...]).astype(BF16)
    col = 0
    for ref in (qkv_ref, xy_ref, gg_ref):
        for c in range(0, ref.shape[1], chunk):
            ref[:, c:c + chunk] = _dot(u, w_ref[:, col + c:col + c + chunk]).astype(ref.dtype)
        col += ref.shape[1]


def _inproj(x2, g, w_in_bf, *, tm):
    n, d = x2.shape
    widths = (3 * SB_WIDTH, 2 * d, 2 * d)
    dtypes = (BF16, F32, F32)
    return pl.pallas_call(
        functools.partial(_inproj_body, chunk=512),
        grid=(n // tm,),
        in_specs=[pl.BlockSpec((tm, d), lambda i: (i, 0)),
                  _resident((1, d)),
                  _resident(w_in_bf.shape)],
        out_specs=[pl.BlockSpec((tm, w), lambda i: (i, 0)) for w in widths],
        out_shape=[jax.ShapeDtypeStruct((n, w), dt) for w, dt in zip(widths, dtypes)],
        compiler_params=_params("parallel"),
        name="inproj",
    )(x2, g, w_in_bf)


def _attn_body(q_ref, k_ref, v_ref, o_ref, o_acc, c_acc, *, tq):
    qi = pl.program_id(2)
    lane = lax.broadcasted_iota(jnp.int32, (1, LANES), 1)
    first = lane < SB_HEAD_DIM
    q = q_ref[0] * (SB_HEAD_DIM ** -0.5)
    zq = jnp.zeros_like(q)
    q_heads = (jnp.where(first, q, zq), jnp.where(first, zq, q))
    row = lax.broadcasted_iota(jnp.int32, (tq, tq), 0)
    col = lax.broadcasted_iota(jnp.int32, (tq, tq), 1)
    causal = col < row
    tri = jnp.where(row > col, 1.0, 0.0).astype(BF16)

    def block(j, diag):
        start = pl.multiple_of(j * tq, tq)
        kj = k_ref[0, pl.ds(start, tq), :]
        vj = v_ref[0, pl.ds(start, tq), :]
        zv = jnp.zeros_like(vj)
        v_heads = (jnp.where(first, vj, zv), jnp.where(first, zv, vj))
        o = None if diag else o_acc[...]
        low = None
        for h in range(2):
            z = lax.dot_general(q_heads[h], kj, (((1,), (1,)), ((), ())), preferred_element_type=F32)
            sp = _softplus(z)
            if diag:
                sp = jnp.where(causal, sp, 0.0)
            sp_hi = sp.astype(BF16)
            sp_lo = (sp - sp_hi.astype(F32)).astype(BF16)
            suffix = _dot(sp_hi, tri) + _dot(sp_lo, tri)
            arg = z - sp - suffix
            if not diag:
                arg = arg - c_acc[h]
            att = jnp.exp(arg)
            if diag:
                att = jnp.where(causal, att, 0.0)
            pv = _dot(att.astype(BF16), v_heads[h])
            o = pv if o is None else o + pv
            c = jnp.sum(sp, axis=-1, keepdims=True)
            if not diag:
                c = c + c_acc[h]
            c_acc[h] = c
            m = jnp.min(c)
            low = m if low is None else jnp.minimum(low, m)
        o_acc[...] = o
        return low

    def more(state):
        j, low = state
        return jnp.logical_and(j >= 0, low < ATTN_DEAD_CARRY)

    def step(state):
        j, _ = state
        return j - 1, block(j, False)

    lax.while_loop(more, step, (qi - 1, block(qi, True)))
    o_ref[0] = o_acc[...].astype(o_ref.dtype)


def _attention(qkv, *, bsz, seq, tq):
    qkv3 = qkv.reshape(bsz, seq, 3 * SB_WIDTH)
    pairs = SB_WIDTH // LANES
    return pl.pallas_call(
        functools.partial(_attn_body, tq=tq),
        grid=(bsz, pairs, seq // tq),
        in_specs=[pl.BlockSpec((1, tq, LANES), lambda b, p, i: (b, i, p)),
                  pl.BlockSpec((1, seq, LANES), lambda b, p, i: (b, 0, pairs + p)),
                  pl.BlockSpec((1, seq, LANES), lambda b, p, i: (b, 0, 2 * pairs + p))],
        out_specs=pl.BlockSpec((1, tq, LANES), lambda b, p, i: (b, i, p)),
        out_shape=jax.ShapeDtypeStruct((bsz, seq, SB_WIDTH), BF16),
        scratch_shapes=[pltpu.VMEM((tq, LANES), F32), pltpu.VMEM((2, tq, 1), F32)],
        compiler_params=_params("parallel", "parallel", "arbitrary"),
        name="attn",
    )(qkv3, qkv3, qkv3)


def _lru_body(x_ref, y_ref, cw_ref, cb_ref, wa_ref, ba_ref, wx_ref, bx_ref, lam_ref, o_ref,
              xpad, a_s, b_s, h_s, *, ts, nb):
    s = pl.program_id(1)
    c = x_ref.shape[2]
    halo = SUBLANES
    pitch = a_s.shape[1] // nb

    @pl.when(s == 0)
    def _():
        xpad[:, 0:halo, :] = jnp.zeros((nb, halo, c), F32)
        h_s[...] = jnp.zeros_like(h_s)

    @pl.when(s > 0)
    def _():
        xpad[:, 0:halo, :] = xpad[:, ts:ts + halo, :]

    xpad[:, halo:halo + ts, :] = x_ref[...]
    xc = jnp.zeros((nb, ts, c), F32) + cb_ref[...]
    for i in range(CONV_WIDTH):
        off = halo - (CONV_WIDTH - 1) + i
        xc = xc + cw_ref[i:i + 1, :] * xpad[:, off:off + ts, :]
    xc = xc.reshape(nb * ts, c)

    neg_sp_lam = -RG_C * _softplus(-lam_ref[...])
    xb = xc.astype(BF16)
    for g in range(c // MXU_DIM):
        sl = slice(g * MXU_DIM, (g + 1) * MXU_DIM)
        r = _sigmoid(_dot(xb[:, sl], wa_ref[g]) + ba_ref[:, sl])
        gate_i = _sigmoid(_dot(xb[:, sl], wx_ref[g]) + bx_ref[:, sl])
        log_a = r * neg_sp_lam[:, sl]
        a = jnp.exp(log_a)
        b = jnp.sqrt(1.0 - jnp.exp(2.0 * log_a)) * (gate_i * xc[:, sl])
        for j in range(MXU_DIM // LANES):
            for bi in range(nb):
                seg = slice(bi * pitch, bi * pitch + ts)
                a_s[g * (MXU_DIM // LANES) + j, seg, :] = a[bi * ts:(bi + 1) * ts, j * LANES:(j + 1) * LANES]
                b_s[g * (MXU_DIM // LANES) + j, seg, :] = b[bi * ts:(bi + 1) * ts, j * LANES:(j + 1) * LANES]

    def step(t, hs):
        out = []
        for j in range(c // LANES):
            h = a_s[j, pl.ds(t, nb, stride=pitch), :] * hs[j] + b_s[j, pl.ds(t, nb, stride=pitch), :]
            b_s[j, pl.ds(t, nb, stride=pitch), :] = h
            out.append(h)
        return tuple(out)

    hs = lax.fori_loop(0, ts, step, tuple(h_s[j] for j in range(c // LANES)))
    for j in range(c // LANES):
        h_s[j] = hs[j]

    for j in range(c // LANES):
        for bi in range(nb):
            y = y_ref[bi, :, j * LANES:(j + 1) * LANES]
            gelu = 0.5 * y * (1.0 + jnp.tanh(0.7978845608028654 * (y + 0.044715 * (y * y * y))))
            h = b_s[j, bi * pitch:bi * pitch + ts, :]
            o_ref[bi, :, j * LANES:(j + 1) * LANES] = (h * gelu).astype(o_ref.dtype)


def _lru(xy, conv_w, conv_b, wa_bd, b_a, wx_bd, b_x, lam, *, bsz, seq, ts):
    c = xy.shape[-1] // 2
    nb = SUBLANES
    xy3 = xy.reshape(bsz, seq, 2 * c)
    return pl.pallas_call(
        functools.partial(_lru_body, ts=ts, nb=nb),
        grid=(bsz // nb, seq // ts),
        in_specs=[pl.BlockSpec((nb, ts, c), lambda b, s: (b, s, 0)),
                  pl.BlockSpec((nb, ts, c), lambda b, s: (b, s, 1)),
                  _resident(conv_w.shape), _resident((1, c)),
                  _resident(wa_bd.shape), _resident((1, c)),
                  _resident(wx_bd.shape), _resident((1, c)),
                  _resident((1, c))],
        out_specs=pl.BlockSpec((nb, ts, c), lambda b, s: (b, s, 0)),
        out_shape=jax.ShapeDtypeStruct((bsz, seq, c), BF16),
        scratch_shapes=[pltpu.VMEM((nb, ts + SUBLANES, c), F32),
                        pltpu.VMEM((c // LANES, nb * (ts + LRU_PITCH_PAD), LANES), F32),
                        pltpu.VMEM((c // LANES, nb * (ts + LRU_PITCH_PAD), LANES), F32),
                        pltpu.VMEM((c // LANES, nb, LANES), F32)],
        compiler_params=_params("parallel", "arbitrary"),
        name="lru",
    )(xy3, xy3, conv_w, conv_b, wa_bd, b_a, wx_bd, b_x, lam)


def _block_diag(w, group):
    n, k, _ = w.shape
    w = w.reshape(n // group, group, k, k)
    eye = jnp.eye(group, dtype=w.dtype)
    return jnp.einsum('gakl,ab->gakbl', w, eye).reshape(n // group, group * k, group * k)


def _merge_body(x_ref, osb_ref, hg_ref, gg_ref, gbias_ref, wsb_ref, wlru_ref, wout_ref, g2_ref,
                wr_ref, br_ref, h1_ref, u2_ref, idx_ref, rank_ref, gate_ref, cnt_ref, base_ref):
    i = pl.program_id(0)
    tm, d = x_ref.shape

    @pl.when(i == 0)
    def _():
        base_ref[...] = jnp.zeros_like(base_ref)

    y_sb = _dot(osb_ref[...], wsb_ref[...])
    y_lru = _dot(hg_ref[...], wlru_ref[...])
    g_a = _sigmoid(gg_ref[:, 0:d] + gbias_ref[:, 0:d])
    g_b = _sigmoid(gg_ref[:, d:2 * d] + gbias_ref[:, d:2 * d])
    merged = (g_a * y_sb + g_b * y_lru).astype(BF16)
    h1 = x_ref[...] + _dot(merged, wout_ref[...])
    h1_ref[...] = h1
    u2 = _rms(h1, g2_ref[...])
    _to_row_tiles(u2_ref, u2)
    logits = _dot(u2.astype(BF16), wr_ref[...]) + br_ref[...]

    e_iota = lax.broadcasted_iota(jnp.int32, (tm, N_EXPERTS), 1)
    k_iota = lax.broadcasted_iota(jnp.int32, (tm, TOP_K), 1)
    vals = logits
    chosen = jnp.zeros((tm, N_EXPERTS), F32)
    top_v, top_i, hots = [], [], []
    for _ in range(TOP_K):
        m = jnp.max(vals, axis=-1, keepdims=True)
        sel = jnp.min(jnp.where(vals == m, e_iota, N_EXPERTS), axis=-1, keepdims=True)
        hot = e_iota == sel
        top_v.append(m)
        top_i.append(sel)
        hots.append(hot)
        chosen = jnp.where(hot, 1.0, chosen)
        vals = jnp.where(hot, -jnp.inf, vals)

    exps = [jnp.exp(v - top_v[0]) for v in top_v]
    denom = exps[0] + exps[1] + exps[2] + exps[3]

    r_iota = lax.broadcasted_iota(jnp.int32, (tm, tm), 0)
    c_iota = lax.broadcasted_iota(jnp.int32, (tm, tm), 1)
    lower = jnp.where(c_iota < r_iota, 1.0, 0.0).astype(BF16)
    before = _dot(lower, chosen.astype(BF16)) + base_ref[...]

    idx = jnp.zeros((tm, TOP_K), jnp.int32)
    rank = jnp.zeros((tm, TOP_K), jnp.int32)
    gate = jnp.zeros((tm, TOP_K), F32)
    for k in range(TOP_K):
        rk = jnp.sum(jnp.where(hots[k], before, 0.0), axis=-1, keepdims=True).astype(jnp.int32)
        idx = jnp.where(k_iota == k, top_i[k], idx)
        rank = jnp.where(k_iota == k, rk, rank)
        gate = jnp.where(k_iota == k, exps[k] / denom, gate)
    idx_ref[...] = idx
    rank_ref[...] = rank
    gate_ref[...] = gate

    total = base_ref[...] + jnp.sum(chosen, axis=0, keepdims=True)
    base_ref[...] = total
    cnt_ref[...] = total


def _merge(x2, o_sb, hg, gg, gate_bias, wsb, wlru, wout, g2, wr, br, *, tm):
    n, d = x2.shape
    row = lambda w: pl.BlockSpec((tm, w), lambda i: (i, 0))
    assert d == SUBLANES * LANES, "row tile form needs one (8, 128) tile per row"
    outs = [(n, d, F32), (n * SUBLANES, LANES, F32), (n, TOP_K, jnp.int32), (n, TOP_K, jnp.int32),
            (n, TOP_K, F32)]
    return pl.pallas_call(
        _merge_body,
        grid=(n // tm,),
        in_specs=[row(d), row(SB_WIDTH), row(d), row(2 * d), _resident((1, 2 * d)),
                  _resident(wsb.shape), _resident(wlru.shape), _resident(wout.shape),
                  _resident((1, d)), _resident(wr.shape), _resident((1, N_EXPERTS))],
        out_specs=[pl.BlockSpec((r // (n // tm), w), lambda i: (i, 0)) for r, w, _ in outs]
                  + [pl.BlockSpec((1, N_EXPERTS), lambda i: (0, 0))],
        out_shape=[jax.ShapeDtypeStruct((r, w), dt) for r, w, dt in outs]
                  + [jax.ShapeDtypeStruct((1, N_EXPERTS), F32)],
        scratch_shapes=[pltpu.VMEM((1, N_EXPERTS), F32)],
        compiler_params=_params("arbitrary"),
        name="merge",
    )(x2, o_sb, hg, gg, gate_bias, wsb, wlru, wout, g2, wr, br)


def _to_row_tiles(ref, value):
    rows = value.shape[0]
    for j in range(SUBLANES):
        ref[pl.ds(j, rows, stride=SUBLANES), :] = value[:, j * LANES:(j + 1) * LANES]


def _row_tile_chunks(ref, rows):
    return [ref[pl.ds(j, rows, stride=SUBLANES), :] for j in range(SUBLANES)]


def _tile_at(ref, first):
    if not isinstance(first, int):
        first = pl.multiple_of(first, SUBLANES)
    return ref.at[pl.ds(first, SUBLANES)]


def _gather_copy(u_hbm, first, xbuf_slot, r, sem):
    return pltpu.make_async_copy(_tile_at(u_hbm, first), _tile_at(xbuf_slot, r * SUBLANES), sem)


def _scatter_copy(ybuf_slot, r, y_hbm, first, sem):
    return pltpu.make_async_copy(_tile_at(ybuf_slot, r * SUBLANES), _tile_at(y_hbm, first), sem)


def _experts_body(be_ref, nused_ref, tok_ref, nxt_ref, dstp_ref, u_hbm, w1_ref, b1_ref, w2_ref, b2_ref,
                  y_hbm, xbuf, ybuf, xmat, w1b, w2b, gsem, ssem):
    i = pl.program_id(0)
    rows = xbuf.shape[1] // SUBLANES
    ff = w2_ref.shape[1]
    n_used = nused_ref[0]
    slot = i % 2
    whole = y_hbm.at[pl.ds(y_hbm.shape[0] - 2 * rows * SUBLANES, rows * SUBLANES)]

    def gather_wait(s):
        pltpu.make_async_copy(whole, xbuf.at[s], gsem.at[s]).wait()

    def scatter_wait(s):
        pltpu.make_async_copy(ybuf.at[0], whole, ssem.at[s]).wait()

    @pl.when(i == 0)
    def _():
        def issue(r, carry):
            _gather_copy(u_hbm, tok_ref[0, 0, r], xbuf.at[0], r, gsem.at[0]).start()
            return carry

        lax.fori_loop(0, rows, issue, 0)
        ybuf[...] = jnp.zeros_like(ybuf)
        pltpu.make_async_copy(ybuf.at[0], whole, ssem.at[1]).start()

    @pl.when(jnp.logical_and(i < n_used,
                             jnp.logical_or(i == 0, be_ref[i] != be_ref[jnp.maximum(i - 1, 0)])))
    def _():
        w1b[...] = w1_ref[0].astype(BF16)
        w2b[...] = w2_ref[0].astype(BF16)

    @pl.when(i < n_used)
    def _():
        gather_wait(slot)
        xmat[...] = jnp.concatenate(_row_tile_chunks(xbuf.at[slot], rows), axis=1).astype(BF16)
        for r in range(rows):
            _gather_copy(u_hbm, nxt_ref[0, 0, r], xbuf.at[1 - slot], r,
                         gsem.at[1 - slot]).start(priority=r % 2)
            _scatter_copy(ybuf.at[1 - slot], r, y_hbm, dstp_ref[0, 0, r],
                          ssem.at[slot]).start(priority=r % 2)
        hid = _dot(xmat[...], w1b[...]) + b1_ref[0]
        x_glu = jnp.minimum(hid[:, 0:ff], SWIGLU_LIMIT)
        x_lin = jnp.clip(hid[:, ff:2 * ff], -SWIGLU_LIMIT, SWIGLU_LIMIT)
        act = x_glu * _sigmoid(SWIGLU_ALPHA * x_glu) * (x_lin + 1.0)
        y = _dot(act.astype(BF16), w2b[...]) + b2_ref[0]
        scatter_wait(1 - slot)
        _to_row_tiles(ybuf.at[slot], y)

    @pl.when(i == n_used)
    def _():
        gather_wait(slot)
        scatter_wait(1 - slot)

        def issue(r, carry):
            _scatter_copy(ybuf.at[1 - slot], r, y_hbm, dstp_ref[0, 0, r], ssem.at[slot]).start()
            return carry

        lax.fori_loop(0, rows, issue, 0)
        scatter_wait(slot)


def _experts(block_e, n_used, slot_tok3, slot_dst3, u2, w1, b1, w2, b2, *, rows, n_out):
    n_steps = slot_tok3.shape[0]
    _, d, ff2 = w1.shape
    ff = ff2 // 2
    last = n_steps - 1
    grid_spec = pltpu.PrefetchScalarGridSpec(
        num_scalar_prefetch=2,
        grid=(n_steps,),
        in_specs=[pl.BlockSpec((1, 1, rows), lambda i, be, nu: (i, 0, 0), memory_space=pltpu.SMEM),
                  pl.BlockSpec((1, 1, rows), lambda i, be, nu: (jnp.minimum(i + 1, last), 0, 0),
                               memory_space=pltpu.SMEM),
                  pl.BlockSpec((1, 1, rows), lambda i, be, nu: (i, 0, 0), memory_space=pltpu.SMEM),
                  pl.BlockSpec(memory_space=pl.ANY),
                  pl.BlockSpec((1, d, ff2), lambda i, be, nu: (be[i], 0, 0)),
                  pl.BlockSpec((1, 1, ff2), lambda i, be, nu: (be[i], 0, 0)),
                  pl.BlockSpec((1, ff, d), lambda i, be, nu: (be[i], 0, 0)),
                  pl.BlockSpec((1, 1, d), lambda i, be, nu: (be[i], 0, 0))],
        out_specs=pl.BlockSpec(memory_space=pl.ANY),
        scratch_shapes=[pltpu.VMEM((2, rows * SUBLANES, LANES), F32),
                        pltpu.VMEM((2, rows * SUBLANES, LANES), F32),
                        pltpu.VMEM((rows, d), BF16),
                        pltpu.VMEM((d, ff2), BF16),
                        pltpu.VMEM((ff, d), BF16),
                        pltpu.SemaphoreType.DMA((2,)),
                        pltpu.SemaphoreType.DMA((2,))],
    )
    return pl.pallas_call(
        _experts_body,
        grid_spec=grid_spec,
        out_shape=jax.ShapeDtypeStruct((n_out * SUBLANES, LANES), F32),
        compiler_params=_params("arbitrary"),
        name="experts",
    )(block_e, n_used, slot_tok3, slot_tok3, slot_dst3, u2, w1, b1.reshape(N_EXPERTS, 1, ff2), w2,
      b2.reshape(N_EXPERTS, 1, d))


def _combine_body(gate_ref, h1_ref, g_ref, *rest):
    y_refs, out_ref = rest[:TOP_K], rest[TOP_K]
    tm = h1_ref.shape[0]
    gate = gate_ref[...]
    chunks = [h1_ref[:, j * LANES:(j + 1) * LANES] for j in range(SUBLANES)]
    for k in range(TOP_K):
        yk = _row_tile_chunks(y_refs[k], tm)
        chunks = [c + gate[:, k:k + 1] * y for c, y in zip(chunks, yk)]
    out_ref[...] = _rms(jnp.concatenate(chunks, axis=1), g_ref[...])


def _combine(gate, h1, g, y, *, tm):
    n, d = h1.shape
    tiles = n // tm
    plane = lambda k: pl.BlockSpec((tm * SUBLANES, LANES), lambda i: (k * tiles + i, 0))
    return pl.pallas_call(
        _combine_body,
        grid=(tiles,),
        in_specs=[pl.BlockSpec((tm, TOP_K), lambda i: (i, 0)),
                  pl.BlockSpec((tm, d), lambda i: (i, 0)),
                  _resident((1, d))] + [plane(k) for k in range(TOP_K)],
        out_specs=pl.BlockSpec((tm, d), lambda i: (i, 0)),
        out_shape=jax.ShapeDtypeStruct((n, d), F32),
        compiler_params=_params("parallel"),
        name="combine",
    )(gate, h1, g, *([y] * TOP_K))


def _layer(h2, bsz, seq, norm_mix_g, w_in, gate_bias, conv_w, conv_b, w_rg_a, b_rg_a, w_rg_x, b_rg_x,
           lru_lambda, w_sb_o, w_lru_o, w_out, norm_ffn_g, w_router, b_router, w1, b1, w2, b2, out_g):
    n, d = h2.shape
    tm = min(512, n)
    qkv, xy, gg = _inproj(h2, norm_mix_g.reshape(1, d), w_in.astype(BF16), tm=tm)
    o_sb = _attention(qkv, bsz=bsz, seq=seq, tq=min(ATTN_BLOCK, seq)).reshape(n, SB_WIDTH)
    group = MXU_DIM // (d // LRU_BLOCKS)
    hg = _lru(xy, conv_w, conv_b.reshape(1, d),
              _block_diag(w_rg_a, group).astype(BF16), b_rg_a.reshape(1, d),
              _block_diag(w_rg_x, group).astype(BF16), b_rg_x.reshape(1, d),
              lru_lambda.reshape(1, d), bsz=bsz, seq=seq, ts=min(LRU_TIME_BLOCK, seq)).reshape(n, d)
    h1, u2, idx, rank, gate, counts = _merge(
        h2, o_sb, hg, gg, gate_bias.reshape(1, 2 * d), w_sb_o.astype(BF16), w_lru_o.astype(BF16),
        w_out.astype(BF16), norm_ffn_g.reshape(1, d), w_router.astype(BF16),
        b_router.reshape(1, N_EXPERTS), tm=tm)

    rows = MOE_ROWS
    nk = n * TOP_K
    n_steps = (nk + N_EXPERTS * (rows - 1)) // rows + 1
    counts = counts.reshape(N_EXPERTS).astype(jnp.int32)
    padded = (counts + rows - 1) // rows * rows
    pad_end = jnp.cumsum(padded)
    pad_start = pad_end - padded
    dest = pad_start[idx] + rank
    slots = jnp.arange(n_steps * rows, dtype=jnp.int32)
    scratch = nk + (slots // rows) % 2 * rows + slots % rows
    out_row = jnp.arange(TOP_K, dtype=jnp.int32)[None, :] * n + jnp.arange(n, dtype=jnp.int32)[:, None]
    slot_dst = scratch.at[dest.reshape(nk)].set(out_row.reshape(nk))
    slot_tok = jnp.where(slot_dst < nk, slot_dst % n, 0) * SUBLANES
    before_first = nk + rows + jnp.arange(rows, dtype=jnp.int32)
    prev_dst = jnp.concatenate([before_first, slot_dst[:-rows]]) * SUBLANES
    block_start = jnp.arange(n_steps, dtype=jnp.int32) * rows
    block_e = jnp.minimum(jnp.sum((block_start[:, None] >= pad_end[None, :]).astype(jnp.int32), axis=1),
                          N_EXPERTS - 1)
    n_used = (pad_end[-1:] // rows).astype(jnp.int32)

    y = _experts(block_e, n_used, slot_tok.reshape(n_steps, 1, rows), prev_dst.reshape(n_steps, 1, rows),
                 u2, w1, b1, w2, b2, rows=rows, n_out=nk + 2 * rows)
    return _combine(gate, h1, out_g.reshape(1, d), y, tm=min(COMBINE_ROWS, n))


def kernel(x, norm_mix_g, w_in, gate_bias, conv_w, conv_b, w_rg_a, b_rg_a, w_rg_x, b_rg_x, lru_lambda,
           w_sb_o, w_lru_o, w_out, norm_ffn_g, w_router, b_router, w1, b1, w2, b2, norm_final_g):
    bsz, seq, d = x.shape
    depth = w_in.shape[0]
    assert depth == 1, "the final norm is fused into the last layer's combine"
    assert bsz % SUBLANES == 0 and d % MXU_DIM == 0
    h2 = x.reshape(bsz * seq, d)
    out = _layer(h2, bsz, seq, norm_mix_g[0], w_in[0], gate_bias[0], conv_w[0], conv_b[0], w_rg_a[0],
                 b_rg_a[0], w_rg_x[0], b_rg_x[0], lru_lambda[0], w_sb_o[0], w_lru_o[0], w_out[0],
                 norm_ffn_g[0], w_router[0], b_router[0], w1[0], b1[0], w2[0], b2[0], norm_final_g)
    return out.reshape(bsz, seq, d)
```

```python
import functools

import jax
import jax.numpy as jnp
from jax import lax
from jax.experimental import pallas as pl
from jax.experimental.pallas import tpu as pltpu

F32 = jnp.float32
BF16 = jnp.bfloat16
I32 = jnp.int32

SB_HEADS = 8
SB_HEAD_DIM = 64
SB_WIDTH = SB_HEADS * SB_HEAD_DIM
LRU_BLOCKS = 16
CONV_WIDTH = 4
RG_C = 8.0
N_EXPERTS = 32
TOP_K = 4
SWIGLU_ALPHA = 1.702
SWIGLU_LIMIT = 7.0
NORM_EPS = 1e-6

LANES = 128
SUBLANES = 8
MXU_DIM = 256
VMEM_LIMIT = 56 * 1024 * 1024

ROW_TILE = 512
ATTN_BLOCK = 256
ATTN_DEAD_CARRY = 104.0
LRU_TIME_BLOCK = 128
LRU_PITCH_PAD = 8
MOE_ROWS = 512


def _rms(x, g):
    return x * lax.rsqrt(jnp.mean(x * x, axis=-1, keepdims=True) + NORM_EPS) * g


def _sigmoid(x):
    return 1.0 / (1.0 + jnp.exp(-x))


def _softplus(x):
    return jnp.maximum(x, 0.0) + jnp.log(1.0 + jnp.exp(-jnp.abs(x)))


def _dot(a, b):
    return jnp.dot(a, b, preferred_element_type=F32)


def _params(*sem):
    return pltpu.CompilerParams(dimension_semantics=sem, vmem_limit_bytes=VMEM_LIMIT)


def _resident(shape):
    return pl.BlockSpec(shape, lambda *_: (0,) * len(shape), pipeline_mode=pl.Buffered(1))


def _inproj_body(x_ref, g_ref, w_ref, qkv_ref, xy_ref, gg_ref, *, chunk):
    u = _rms(x_ref[...], g_ref[...]).astype(BF16)
    col = 0
    for ref in (qkv_ref, xy_ref, gg_ref):
        for c in range(0, ref.shape[1], chunk):
            ref[:, c:c + chunk] = _dot(u, w_ref[:, col + c:col + c + chunk]).astype(ref.dtype)
        col += ref.shape[1]


def _inproj(x2, g, w_in_bf, *, tm):
    n, d = x2.shape
    widths = (3 * SB_WIDTH, 2 * d, 2 * d)
    dtypes = (BF16, F32, F32)
    return pl.pallas_call(
        functools.partial(_inproj_body, chunk=512),
        grid=(n // tm,),
        in_specs=[pl.BlockSpec((tm, d), lambda i: (i, 0)),
                  _resident((1, d)),
                  _resident(w_in_bf.shape)],
        out_specs=[pl.BlockSpec((tm, w), lambda i: (i, 0)) for w in widths],
        out_shape=[jax.ShapeDtypeStruct((n, w), dt) for w, dt in zip(widths, dtypes)],
        compiler_params=_params("parallel"),
        name="inproj",
    )(x2, g, w_in_bf)


def _attn_body(q_ref, k_ref, v_ref, o_ref, o_acc, c_acc, *, tq):
    qi = pl.program_id(2)
    lane = lax.broadcasted_iota(I32, (1, LANES), 1)
    first = lane < SB_HEAD_DIM
    q = q_ref[0] * (SB_HEAD_DIM ** -0.5)
    zq = jnp.zeros_like(q)
    q_heads = (jnp.where(first, q, zq), jnp.where(first, zq, q))
    row = lax.broadcasted_iota(I32, (tq, tq), 0)
    col = lax.broadcasted_iota(I32, (tq, tq), 1)
    causal = col < row
    tri = jnp.where(row > col, 1.0, 0.0).astype(BF16)

    def block(j, diag):
        start = pl.multiple_of(j * tq, tq)
        kj = k_ref[0, pl.ds(start, tq), :]
        vj = v_ref[0, pl.ds(start, tq), :]
        zv = jnp.zeros_like(vj)
        v_heads = (jnp.where(first, vj, zv), jnp.where(first, zv, vj))
        o = None if diag else o_acc[...]
        low = None
        for h in range(2):
            z = lax.dot_general(q_heads[h], kj, (((1,), (1,)), ((), ())), preferred_element_type=F32)
            sp = _softplus(z)
            if diag:
                sp = jnp.where(causal, sp, 0.0)
            sp_hi = sp.astype(BF16)
            sp_lo = (sp - sp_hi.astype(F32)).astype(BF16)
            suffix = _dot(sp_hi, tri) + _dot(sp_lo, tri)
            arg = z - sp - suffix
            if not diag:
                arg = arg - c_acc[h]
            att = jnp.exp(arg)
            if diag:
                att = jnp.where(causal, att, 0.0)
            pv = _dot(att.astype(BF16), v_heads[h])
            o = pv if o is None else o + pv
            c = jnp.sum(sp, axis=-1, keepdims=True)
            if not diag:
                c = c + c_acc[h]
            c_acc[h] = c
            m = jnp.min(c)
            low = m if low is None else jnp.minimum(low, m)
        o_acc[...] = o
        return low

    def more(state):
        j, low = state
        return jnp.logical_and(j >= 0, low < ATTN_DEAD_CARRY)

    def step(state):
        j, _ = state
        return j - 1, block(j, False)

    lax.while_loop(more, step, (qi - 1, block(qi, True)))
    o_ref[0] = o_acc[...].astype(o_ref.dtype)


def _attention(qkv, *, bsz, seq, tq):
    qkv3 = qkv.reshape(bsz, seq, 3 * SB_WIDTH)
    pairs = SB_WIDTH // LANES
    return pl.pallas_call(
        functools.partial(_attn_body, tq=tq),
        grid=(bsz, pairs, seq // tq),
        in_specs=[pl.BlockSpec((1, tq, LANES), lambda b, p, i: (b, i, p)),
                  pl.BlockSpec((1, seq, LANES), lambda b, p, i: (b, 0, pairs + p)),
                  pl.BlockSpec((1, seq, LANES), lambda b, p, i: (b, 0, 2 * pairs + p))],
        out_specs=pl.BlockSpec((1, tq, LANES), lambda b, p, i: (b, i, p)),
        out_shape=jax.ShapeDtypeStruct((bsz, seq, SB_WIDTH), BF16),
        scratch_shapes=[pltpu.VMEM((tq, LANES), F32), pltpu.VMEM((2, tq, 1), F32)],
        compiler_params=_params("parallel", "parallel", "arbitrary"),
        name="attn",
    )(qkv3, qkv3, qkv3)


def _lru_body(x_ref, y_ref, cw_ref, cb_ref, wa_ref, ba_ref, wx_ref, bx_ref, lam_ref, o_ref,
              xpad, a_s, b_s, h_s, *, ts, nb):
    s = pl.program_id(1)
    c = x_ref.shape[2]
    halo = SUBLANES
    pitch = a_s.shape[1] // nb

    @pl.when(s == 0)
    def _():
        xpad[:, 0:halo, :] = jnp.zeros((nb, halo, c), F32)
        h_s[...] = jnp.zeros_like(h_s)

    @pl.when(s > 0)
    def _():
        xpad[:, 0:halo, :] = xpad[:, ts:ts + halo, :]

    xpad[:, halo:halo + ts, :] = x_ref[...]
    xc = jnp.zeros((nb, ts, c), F32) + cb_ref[...]
    for i in range(CONV_WIDTH):
        off = halo - (CONV_WIDTH - 1) + i
        xc = xc + cw_ref[i:i + 1, :] * xpad[:, off:off + ts, :]
    xc = xc.reshape(nb * ts, c)

    neg_sp_lam = -RG_C * _softplus(-lam_ref[...])
    xb = xc.astype(BF16)
    for g in range(c // MXU_DIM):
        sl = slice(g * MXU_DIM, (g + 1) * MXU_DIM)
        r = _sigmoid(_dot(xb[:, sl], wa_ref[g]) + ba_ref[:, sl])
        gate_i = _sigmoid(_dot(xb[:, sl], wx_ref[g]) + bx_ref[:, sl])
        log_a = r * neg_sp_lam[:, sl]
        a = jnp.exp(log_a)
        b = jnp.sqrt(1.0 - jnp.exp(2.0 * log_a)) * (gate_i * xc[:, sl])
        for j in range(MXU_DIM // LANES):
            for bi in range(nb):
                seg = slice(bi * pitch, bi * pitch + ts)
                a_s[g * (MXU_DIM // LANES) + j, seg, :] = a[bi * ts:(bi + 1) * ts, j * LANES:(j + 1) * LANES]
                b_s[g * (MXU_DIM // LANES) + j, seg, :] = b[bi * ts:(bi + 1) * ts, j * LANES:(j + 1) * LANES]

    def step(t, hs):
        out = []
        for j in range(c // LANES):
            h = a_s[j, pl.ds(t, nb, stride=pitch), :] * hs[j] + b_s[j, pl.ds(t, nb, stride=pitch), :]
            b_s[j, pl.ds(t, nb, stride=pitch), :] = h
            out.append(h)
        return tuple(out)

    hs = lax.fori_loop(0, ts, step, tuple(h_s[j] for j in range(c // LANES)))
    for j in range(c // LANES):
        h_s[j] = hs[j]

    for j in range(c // LANES):
        for bi in range(nb):
            y = y_ref[bi, :, j * LANES:(j + 1) * LANES]
            gelu = 0.5 * y * (1.0 + jnp.tanh(0.7978845608028654 * (y + 0.044715 * (y * y * y))))
            h = b_s[j, bi * pitch:bi * pitch + ts, :]
            o_ref[bi, :, j * LANES:(j + 1) * LANES] = (h * gelu).astype(o_ref.dtype)


def _lru(xy, conv_w, conv_b, wa_bd, b_a, wx_bd, b_x, lam, *, bsz, seq, ts):
    c = xy.shape[-1] // 2
    nb = SUBLANES
    xy3 = xy.reshape(bsz, seq, 2 * c)
    return pl.pallas_call(
        functools.partial(_lru_body, ts=ts, nb=nb),
        grid=(bsz // nb, seq // ts),
        in_specs=[pl.BlockSpec((nb, ts, c), lambda b, s: (b, s, 0)),
                  pl.BlockSpec((nb, ts, c), lambda b, s: (b, s, 1)),
                  _resident(conv_w.shape), _resident((1, c)),
                  _resident(wa_bd.shape), _resident((1, c)),
                  _resident(wx_bd.shape), _resident((1, c)),
                  _resident((1, c))],
        out_specs=pl.BlockSpec((nb, ts, c), lambda b, s: (b, s, 0)),
        out_shape=jax.ShapeDtypeStruct((bsz, seq, c), BF16),
        scratch_shapes=[pltpu.VMEM((nb, ts + SUBLANES, c), F32),
                        pltpu.VMEM((c // LANES, nb * (ts + LRU_PITCH_PAD), LANES), F32),
                        pltpu.VMEM((c // LANES, nb * (ts + LRU_PITCH_PAD), LANES), F32),
                        pltpu.VMEM((c // LANES, nb, LANES), F32)],
        compiler_params=_params("parallel", "arbitrary"),
        name="lru",
    )(xy3, xy3, conv_w, conv_b, wa_bd, b_a, wx_bd, b_x, lam)


def _block_diag(w, group):
    n, k, _ = w.shape
    w = w.reshape(n // group, group, k, k)
    eye = jnp.eye(group, dtype=w.dtype)
    return jnp.einsum('gakl,ab->gakbl', w, eye).reshape(n // group, group * k, group * k)


def _to_row_tiles(ref, value, first_row=0):
    rows = value.shape[0]
    for j in range(SUBLANES):
        ref[pl.ds(first_row * SUBLANES + j, rows, stride=SUBLANES), :] = value[:, j * LANES:(j + 1) * LANES]


def _from_row_tiles(ref, rows):
    return jnp.concatenate([ref[pl.ds(j, rows, stride=SUBLANES), :] for j in range(SUBLANES)], axis=1)


def _merge_body(x_ref, osb_ref, hg_ref, gg_ref, gbias_ref, wsb_ref, wlru_ref, wout_ref, g2_ref,
                wrt_ref, br_ref, h1_ref, xs_ref, pos_ref, gate_ref, cnt_ref):
    tm, d = x_ref.shape

    y_sb = _dot(osb_ref[...], wsb_ref[...])
    y_lru = _dot(hg_ref[...], wlru_ref[...])
    g_a = _sigmoid(gg_ref[:, 0:d] + gbias_ref[:, 0:d])
    g_b = _sigmoid(gg_ref[:, d:2 * d] + gbias_ref[:, d:2 * d])
    merged = (g_a * y_sb + g_b * y_lru).astype(BF16)
    h1 = x_ref[...] + _dot(merged, wout_ref[...])
    h1_ref[...] = h1
    u2 = _rms(h1, g2_ref[...]).astype(BF16)
    logits = lax.dot_general(wrt_ref[...], u2, (((1,), (1,)), ((), ())),
                             preferred_element_type=F32) + br_ref[...]

    e_iota = lax.broadcasted_iota(I32, (N_EXPERTS, tm), 0)
    k_iota = lax.broadcasted_iota(I32, (TOP_K, tm), 0)
    vals = logits
    chosen = jnp.zeros((N_EXPERTS, tm), F32)
    top_v, hots = [], []
    for _ in range(TOP_K):
        m = jnp.max(vals, axis=0, keepdims=True)
        sel = jnp.min(jnp.where(vals == m, e_iota, N_EXPERTS), axis=0, keepdims=True)
        hot = e_iota == sel
        top_v.append(m)
        hots.append(hot)
        chosen = jnp.where(hot, 1.0, chosen)
        vals = jnp.where(hot, -jnp.inf, vals)
    exps = [jnp.exp(v - top_v[0]) for v in top_v]
    denom = exps[0] + exps[1] + exps[2] + exps[3]

    chosen_b = chosen.astype(BF16)
    t_row = lax.broadcasted_iota(I32, (tm, tm), 0)
    t_col = lax.broadcasted_iota(I32, (tm, tm), 1)
    earlier = jnp.where(t_row < t_col, 1.0, 0.0).astype(BF16)
    same_before = _dot(chosen_b, earlier)
    e_row = lax.broadcasted_iota(I32, (N_EXPERTS, N_EXPERTS), 0)
    e_col = lax.broadcasted_iota(I32, (N_EXPERTS, N_EXPERTS), 1)
    smaller = jnp.where(e_col < e_row, 1.0, 0.0).astype(BF16)
    first = jnp.sum(_dot(smaller, chosen_b), axis=1, keepdims=True)
    cnt_ref[...] = jnp.sum(chosen, axis=1, keepdims=True)

    where_to = first + same_before
    pos = jnp.zeros((TOP_K, tm), I32)
    gate = jnp.zeros((TOP_K, tm), F32)
    pos_k = []
    for k in range(TOP_K):
        pk = jnp.sum(jnp.where(hots[k], where_to, 0.0), axis=0, keepdims=True).astype(I32)
        pos_k.append(pk)
        pos = jnp.where(k_iota == k, pk, pos)
        gate = jnp.where(k_iota == k, exps[k] / denom, gate)
    pos_ref[...] = pos
    gate_ref[...] = gate

    for c in range(TOP_K):
        p_iota = lax.broadcasted_iota(I32, (tm, tm), 0) + c * tm
        hit = p_iota == pos_k[0]
        for k in range(1, TOP_K):
            hit = jnp.logical_or(hit, p_iota == pos_k[k])
        perm = jnp.where(hit, 1.0, 0.0).astype(BF16)
        _to_row_tiles(xs_ref, _dot(perm, u2), first_row=c * tm)


def _merge(x2, o_sb, hg, gg, gate_bias, wsb, wlru, wout, g2, wrt, br, *, tm):
    n, d = x2.shape
    tiles = n // tm
    assert d == SUBLANES * LANES, "row tile form needs one (8, 128) tile per row"
    row = lambda w: pl.BlockSpec((tm, w), lambda i: (i, 0))
    return pl.pallas_call(
        _merge_body,
        grid=(tiles,),
        in_specs=[row(d), row(SB_WIDTH), row(d), row(2 * d), _resident((1, 2 * d)),
                  _resident(wsb.shape), _resident(wlru.shape), _resident(wout.shape),
                  _resident((1, d)), _resident(wrt.shape), _resident((N_EXPERTS, 1))],
        out_specs=[row(d),
                   pl.BlockSpec((TOP_K * tm * SUBLANES, LANES), lambda i: (i, 0)),
                   pl.BlockSpec((TOP_K, tm), lambda i: (0, i)),
                   pl.BlockSpec((TOP_K, tm), lambda i: (0, i)),
                   pl.BlockSpec((N_EXPERTS, 1), lambda i: (i, 0))],
        out_shape=[jax.ShapeDtypeStruct((n, d), F32),
                   jax.ShapeDtypeStruct((TOP_K * n * SUBLANES, LANES), F32),
                   jax.ShapeDtypeStruct((TOP_K, n), I32),
                   jax.ShapeDtypeStruct((TOP_K, n), F32),
                   jax.ShapeDtypeStruct((tiles * N_EXPERTS, 1), F32)],
        compiler_params=_params("parallel"),
        name="merge",
    )(x2, o_sb, hg, gg, gate_bias, wsb, wlru, wout, g2, wrt, br)


def _experts_body(be_ref, r0_ref, t0_ref, nv_ref, nused_ref, cum_ref, off_ref,
                  xs_hbm, w1_ref, b1_ref, w2_ref, b2_ref, ys_hbm,
                  xbuf, ybuf, w1b, w2b, gsem, ssem, *, tiles, tile_rows):
    i = pl.program_id(0)
    rows = xbuf.shape[1] // SUBLANES
    ff = w2_ref.shape[1]
    n_used = nused_ref[0]
    slot = i % 2

    def for_runs(b, nv, fn):
        e = be_ref[b]
        r0 = r0_ref[b]
        end = r0 + nv

        def more(t):
            return jnp.logical_and(t < tiles, cum_ref[jnp.minimum(t, tiles) * N_EXPERTS + e] < end)

        def run(t):
            c0 = cum_ref[t * N_EXPERTS + e]
            c1 = cum_ref[(t + 1) * N_EXPERTS + e]
            lo = jnp.maximum(c0, r0)
            n_rows = jnp.minimum(c1, end) - lo

            @pl.when(n_rows > 0)
            def _():
                src = t * tile_rows + off_ref[t * N_EXPERTS + e] + (lo - c0)
                fn(pl.multiple_of(src * SUBLANES, SUBLANES),
                   pl.multiple_of((lo - r0) * SUBLANES, SUBLANES), n_rows * SUBLANES)

            return t + 1

        @pl.when(nv > 0)
        def _():
            lax.while_loop(more, run, t0_ref[b])

    def fetch(b, nv, s):
        def one(src, dst, size):
            pltpu.make_async_copy(xs_hbm.at[pl.ds(src, size)], xbuf.at[s, pl.ds(dst, size)],
                                  gsem.at[s]).start()
        for_runs(b, nv, one)

    def write_back(b, nv, s, sem):
        def one(src, dst, size):
            pltpu.make_async_copy(ybuf.at[s, pl.ds(dst, size)], ys_hbm.at[pl.ds(src, size)], sem).start()
        for_runs(b, nv, one)

    def wait_rows(nv, sem):
        @pl.when(nv > 0)
        def _():
            size = nv * SUBLANES
            pltpu.make_async_copy(xs_hbm.at[pl.ds(0, size)], xbuf.at[0, pl.ds(0, size)], sem).wait()

    prev = jnp.maximum(i - 1, 0)
    prev2 = jnp.maximum(i - 2, 0)
    nxt = jnp.minimum(i + 1, pl.num_programs(0) - 1)
    nv_prev = jnp.where(i >= 1, nv_ref[prev], 0)
    nv_prev2 = jnp.where(i >= 2, nv_ref[prev2], 0)

    @pl.when(i == 0)
    def _():
        xbuf[...] = jnp.zeros_like(xbuf)
        fetch(0, nv_ref[0], 0)

    @pl.when(jnp.logical_and(i < n_used, jnp.logical_or(i == 0, be_ref[i] != be_ref[prev])))
    def _():
        w1b[...] = w1_ref[0].astype(BF16)
        w2b[...] = w2_ref[0].astype(BF16)

    @pl.when(i <= n_used)
    def _():
        write_back(prev, nv_prev, 1 - slot, ssem.at[slot])

    @pl.when(i < n_used)
    def _():
        fetch(nxt, nv_ref[nxt], 1 - slot)
        wait_rows(nv_ref[i], gsem.at[slot])
        x = _from_row_tiles(xbuf.at[slot], rows).astype(BF16)
        hid = _dot(x, w1b[...]) + b1_ref[0]
        x_glu = jnp.minimum(hid[:, 0:ff], SWIGLU_LIMIT)
        x_lin = jnp.clip(hid[:, ff:2 * ff], -SWIGLU_LIMIT, SWIGLU_LIMIT)
        act = x_glu * _sigmoid(SWIGLU_ALPHA * x_glu) * (x_lin + 1.0)
        y = _dot(act.astype(BF16), w2b[...]) + b2_ref[0]
        wait_rows(nv_prev2, ssem.at[1 - slot])
        _to_row_tiles(ybuf.at[slot], y)

    @pl.when(i == n_used)
    def _():
        wait_rows(nv_prev2, ssem.at[1 - slot])
        wait_rows(nv_prev, ssem.at[slot])


def _experts(block_e, block_r0, block_t0, block_nv, n_used, cum, off, xs, w1, b1, w2, b2, *,
             rows, tiles, tile_rows):
    n_steps = block_e.shape[0]
    _, d, ff2 = w1.shape
    ff = ff2 // 2
    weights = lambda shape: pl.BlockSpec(shape, lambda i, be, *_: (be[i], 0, 0))
    grid_spec = pltpu.PrefetchScalarGridSpec(
        num_scalar_prefetch=7,
        grid=(n_steps,),
        in_specs=[pl.BlockSpec(memory_space=pl.ANY),
                  weights((1, d, ff2)), weights((1, 1, ff2)), weights((1, ff, d)), weights((1, 1, d))],
        out_specs=pl.BlockSpec(memory_space=pl.ANY),
        scratch_shapes=[pltpu.VMEM((2, rows * SUBLANES, LANES), F32),
                        pltpu.VMEM((2, rows * SUBLANES, LANES), F32),
                        pltpu.VMEM((d, ff2), BF16),
                        pltpu.VMEM((ff, d), BF16),
                        pltpu.SemaphoreType.DMA((2,)),
                        pltpu.SemaphoreType.DMA((2,))],
    )
    return pl.pallas_call(
        functools.partial(_experts_body, tiles=tiles, tile_rows=tile_rows),
        grid_spec=grid_spec,
        out_shape=jax.ShapeDtypeStruct(xs.shape, F32),
        compiler_params=_params("arbitrary"),
        name="experts",
    )(block_e, block_r0, block_t0, block_nv, n_used, cum, off, xs, w1,
      b1.reshape(N_EXPERTS, 1, ff2), w2, b2.reshape(N_EXPERTS, 1, d))


def _combine_body(pos_ref, gate_ref, h1_ref, g_ref, ys_ref, out_ref):
    tm = h1_ref.shape[0]
    p_iota = lax.broadcasted_iota(I32, (tm, TOP_K * tm), 1)
    pos = pos_ref[...]
    gate = gate_ref[...]
    g_mat = jnp.zeros((tm, TOP_K * tm), F32)
    for k in range(TOP_K):
        g_mat = jnp.where(p_iota == pos[:, k:k + 1], gate[:, k:k + 1], g_mat)
    ys = _from_row_tiles(ys_ref, TOP_K * tm).astype(BF16)
    out_ref[...] = _rms(h1_ref[...] + _dot(g_mat.astype(BF16), ys), g_ref[...])


def _combine(pos, gate, h1, g, ys, *, tm):
    n, d = h1.shape
    return pl.pallas_call(
        _combine_body,
        grid=(n // tm,),
        in_specs=[pl.BlockSpec((tm, TOP_K), lambda i: (i, 0)),
                  pl.BlockSpec((tm, TOP_K), lambda i: (i, 0)),
                  pl.BlockSpec((tm, d), lambda i: (i, 0)),
                  _resident((1, d)),
                  pl.BlockSpec((TOP_K * tm * SUBLANES, LANES), lambda i: (i, 0))],
        out_specs=pl.BlockSpec((tm, d), lambda i: (i, 0)),
        out_shape=jax.ShapeDtypeStruct((n, d), F32),
        compiler_params=_params("parallel"),
        name="combine",
    )(pos, gate, h1, g, ys)


def _block_tables(cnt, rows):
    cum = jnp.concatenate([jnp.zeros((1, N_EXPERTS), I32), jnp.cumsum(cnt, axis=0)])
    off = jnp.cumsum(cnt, axis=1) - cnt
    total = cum[-1]
    blocks = (total + rows - 1) // rows
    block_end = jnp.cumsum(blocks)
    return cum, off, total, blocks, block_end


def _layer(h2, bsz, seq, norm_mix_g, w_in, gate_bias, conv_w, conv_b, w_rg_a, b_rg_a, w_rg_x, b_rg_x,
           lru_lambda, w_sb_o, w_lru_o, w_out, norm_ffn_g, w_router, b_router, w1, b1, w2, b2, out_g):
    n, d = h2.shape
    tm = min(ROW_TILE, n)
    tiles = n // tm
    qkv, xy, gg = _inproj(h2, norm_mix_g.reshape(1, d), w_in.astype(BF16), tm=tm)
    o_sb = _attention(qkv, bsz=bsz, seq=seq, tq=min(ATTN_BLOCK, seq)).reshape(n, SB_WIDTH)
    group = MXU_DIM // (d // LRU_BLOCKS)
    hg = _lru(xy, conv_w, conv_b.reshape(1, d),
              _block_diag(w_rg_a, group).astype(BF16), b_rg_a.reshape(1, d),
              _block_diag(w_rg_x, group).astype(BF16), b_rg_x.reshape(1, d),
              lru_lambda.reshape(1, d), bsz=bsz, seq=seq, ts=min(LRU_TIME_BLOCK, seq)).reshape(n, d)
    h1, xs, pos, gate, cnt = _merge(
        h2, o_sb, hg, gg, gate_bias.reshape(1, 2 * d), w_sb_o.astype(BF16), w_lru_o.astype(BF16),
        w_out.astype(BF16), norm_ffn_g.reshape(1, d), w_router.T.astype(BF16),
        b_router.reshape(N_EXPERTS, 1), tm=tm)

    rows = MOE_ROWS
    n_steps = (n * TOP_K + N_EXPERTS * (rows - 1)) // rows + 1
    cnt = cnt.reshape(tiles, N_EXPERTS).astype(I32)
    cum, off, total, blocks, block_end = _block_tables(cnt, rows)
    step = jnp.arange(n_steps, dtype=I32)
    block_e = jnp.minimum(jnp.sum((step[:, None] >= block_end[None, :]).astype(I32), axis=1), N_EXPERTS - 1)
    block_r0 = (step - (block_end - blocks)[block_e]) * rows
    block_nv = jnp.clip(total[block_e] - block_r0, 0, rows)
    block_nv = jnp.where(step < block_end[-1], block_nv, 0)
    block_t0 = jnp.sum((cum[1:, :].T[block_e] <= block_r0[:, None]).astype(I32), axis=1)
    block_t0 = jnp.minimum(block_t0, tiles - 1)
    n_used = block_end[-1:].astype(I32)

    ys = _experts(block_e, block_r0, block_t0, block_nv, n_used, cum.reshape(-1), off.reshape(-1),
                  xs, w1, b1, w2, b2, rows=rows, tiles=tiles, tile_rows=TOP_K * tm)
    return _combine(pos.T, gate.T, h1, out_g.reshape(1, d), ys, tm=tm)


def kernel(x, norm_mix_g, w_in, gate_bias, conv_w, conv_b, w_rg_a, b_rg_a, w_rg_x, b_rg_x, lru_lambda,
           w_sb_o, w_lru_o, w_out, norm_ffn_g, w_router, b_router, w1, b1, w2, b2, norm_final_g):
    bsz, seq, d = x.shape
    depth = w_in.shape[0]
    assert depth == 1, "the final norm is fused into the last layer's combine"
    assert bsz % SUBLANES == 0 and d % MXU_DIM == 0
    h2 = x.reshape(bsz * seq, d)
    out = _layer(h2, bsz, seq, norm_mix_g[0], w_in[0], gate_bias[0], conv_w[0], conv_b[0], w_rg_a[0],
                 b_rg_a[0], w_rg_x[0], b_rg_x[0], lru_lambda[0], w_sb_o[0], w_lru_o[0], w_out[0],
                 norm_ffn_g[0], w_router[0], b_router[0], w1[0], b1[0], w2[0], b2[0], norm_final_g)
    return out.reshape(bsz, seq, d)
```

```python
import functools

import jax
import jax.numpy as jnp
from jax import lax
from jax.experimental import pallas as pl
from jax.experimental.pallas import tpu as pltpu

F32 = jnp.float32
BF16 = jnp.bfloat16
I32 = jnp.int32

SB_HEADS = 8
SB_HEAD_DIM = 64
SB_WIDTH = SB_HEADS * SB_HEAD_DIM
LRU_BLOCKS = 16
CONV_WIDTH = 4
RG_C = 8.0
N_EXPERTS = 32
TOP_K = 4
SWIGLU_ALPHA = 1.702
SWIGLU_LIMIT = 7.0
NORM_EPS = 1e-6

LANES = 128
SUBLANES = 8
MXU_DIM = 256
VMEM_LIMIT = 56 * 1024 * 1024

ROW_TILE = 512
ATTN_PAIRS = 2
ATTN_BLOCK = 256
ATTN_DEAD_CARRY = 104.0
LRU_TIME_BLOCK = 128
LRU_PITCH_PAD = 8
MOE_ROWS = 512


def _rms(x, g):
    return x * lax.rsqrt(jnp.mean(x * x, axis=-1, keepdims=True) + NORM_EPS) * g


def _sigmoid(x):
    return 1.0 / (1.0 + jnp.exp(-x))


def _sigmoid_t(x):
    return 0.5 * jnp.tanh(0.5 * x) + 0.5


def _softplus(x):
    return jnp.maximum(x, 0.0) + jnp.log(1.0 + jnp.exp(-jnp.abs(x)))


def _dot(a, b):
    return jnp.dot(a, b, preferred_element_type=F32)


def _params(*sem):
    return pltpu.CompilerParams(dimension_semantics=sem, vmem_limit_bytes=VMEM_LIMIT)


def _resident(shape):
    return pl.BlockSpec(shape, lambda *_: (0,) * len(shape), pipeline_mode=pl.Buffered(1))


def _inproj_body(x_ref, g_ref, w_ref, qkv_ref, xy_ref, gg_ref, *, chunk):
    u = _rms(x_ref[...], g_ref[...]).astype(BF16)
    col = 0
    for ref in (qkv_ref, xy_ref, gg_ref):
        for c in range(0, ref.shape[1], chunk):
            ref[:, c:c + chunk] = _dot(u, w_ref[:, col + c:col + c + chunk]).astype(ref.dtype)
        col += ref.shape[1]


def _inproj(x2, g, w_in_bf, *, tm):
    n, d = x2.shape
    widths = (3 * SB_WIDTH, 2 * d, 2 * d)
    dtypes = (BF16, F32, F32)
    return pl.pallas_call(
        functools.partial(_inproj_body, chunk=512),
        grid=(n // tm,),
        in_specs=[pl.BlockSpec((tm, d), lambda i: (i, 0)),
                  _resident((1, d)),
                  _resident(w_in_bf.shape)],
        out_specs=[pl.BlockSpec((tm, w), lambda i: (i, 0)) for w in widths],
        out_shape=[jax.ShapeDtypeStruct((n, w), dt) for w, dt in zip(widths, dtypes)],
        compiler_params=_params("parallel"),
        name="inproj",
    )(x2, g, w_in_bf)


def _attn_body(q_ref, k_ref, v_ref, o_ref, o_acc, c_acc, *, tq, npair):
    qi = pl.program_id(2)
    lane = lax.broadcasted_iota(I32, (1, LANES), 1)
    first = lane < SB_HEAD_DIM
    q_pairs = []
    for p in range(npair):
        q = q_ref[0, :, p * LANES:(p + 1) * LANES] * (SB_HEAD_DIM ** -0.5)
        zq = jnp.zeros_like(q)
        q_pairs.append(jnp.concatenate([jnp.where(first, q, zq), jnp.where(first, zq, q)], axis=0))
    row = lax.broadcasted_iota(I32, (2 * tq, tq), 0)
    row = jnp.where(row >= tq, row - tq, row)
    col = lax.broadcasted_iota(I32, (2 * tq, tq), 1)
    causal = col < row
    tri2 = jnp.where(row > col, 1.0, 0.0).astype(BF16)

    def block(j, diag):
        start = pl.multiple_of(j * tq, tq)
        zs, sps, sufs = [], [], []
        for p in range(npair):
            kj = k_ref[0, pl.ds(start, tq), p * LANES:(p + 1) * LANES]
            zs.append(lax.dot_general(q_pairs[p], kj, (((1,), (1,)), ((), ())), preferred_element_type=F32))
        for p in range(npair):
            sp = _softplus(zs[p])
            if diag:
                sp = jnp.where(causal, sp, 0.0)
            sps.append(sp)
            sp_hi = sp.astype(BF16)
            sp_lo = (sp - sp_hi.astype(F32)).astype(BF16)
            sufs.append(_dot(jnp.concatenate([sp_hi, sp_lo], axis=1), tri2))
        low = None
        for p in range(npair):
            vj = v_ref[0, pl.ds(start, tq), p * LANES:(p + 1) * LANES]
            zv = jnp.zeros_like(vj)
            v2 = jnp.concatenate([jnp.where(first, vj, zv), jnp.where(first, zv, vj)], axis=0)
            arg = zs[p] - sps[p] - sufs[p]
            if not diag:
                arg = arg - c_acc[p]
            att = jnp.exp(arg)
            if diag:
                att = jnp.where(causal, att, 0.0)
            att = att.astype(BF16)
            pv = _dot(jnp.concatenate([att[0:tq], att[tq:2 * tq]], axis=1), v2)
            if not diag:
                pv = pv + o_acc[:, p * LANES:(p + 1) * LANES]
            o_acc[:, p * LANES:(p + 1) * LANES] = pv
            c = jnp.sum(sps[p], axis=-1, keepdims=True)
            if not diag:
                c = c + c_acc[p]
            c_acc[p] = c
            m = jnp.min(c)
            low = m if low is None else jnp.minimum(low, m)
        return low

    def more(state):
        j, low = state
        return jnp.logical_and(j >= 0, low < ATTN_DEAD_CARRY)

    def step(state):
        j, _ = state
        return j - 1, block(j, False)

    lax.while_loop(more, step, (qi - 1, block(qi, True)))
    o_ref[0] = o_acc[...].astype(o_ref.dtype)


def _attention(qkv, *, bsz, seq, tq, npair=ATTN_PAIRS):
    qkv3 = qkv.reshape(bsz, seq, 3 * SB_WIDTH)
    width = npair * LANES
    groups = SB_WIDTH // width
    return pl.pallas_call(
        functools.partial(_attn_body, tq=tq, npair=npair),
        grid=(bsz, groups, seq // tq),
        in_specs=[pl.BlockSpec((1, tq, width), lambda b, p, i: (b, i, p)),
                  pl.BlockSpec((1, seq, width), lambda b, p, i: (b, 0, groups + p)),
                  pl.BlockSpec((1, seq, width), lambda b, p, i: (b, 0, 2 * groups + p))],
        out_specs=pl.BlockSpec((1, tq, width), lambda b, p, i: (b, i, p)),
        out_shape=jax.ShapeDtypeStruct((bsz, seq, SB_WIDTH), BF16),
        scratch_shapes=[pltpu.VMEM((tq, width), F32), pltpu.VMEM((npair, 2 * tq, 1), F32)],
        compiler_params=_params("parallel", "parallel", "arbitrary"),
        name="attn",
    )(qkv3, qkv3, qkv3)


def _lru_body(x_ref, y_ref, cw_ref, cb_ref, wa_ref, ba_ref, wx_ref, bx_ref, lam_ref, o_ref,
              xpad, a_s, b_s, h_s, *, ts, nb):
    s = pl.program_id(1)
    c = x_ref.shape[2]
    halo = SUBLANES
    pitch = a_s.shape[1] // nb

    @pl.when(s == 0)
    def _():
        xpad[:, 0:halo, :] = jnp.zeros((nb, halo, c), F32)
        h_s[...] = jnp.zeros_like(h_s)

    @pl.when(s > 0)
    def _():
        xpad[:, 0:halo, :] = xpad[:, ts:ts + halo, :]

    xpad[:, halo:halo + ts, :] = x_ref[...]
    xc = jnp.zeros((nb, ts, c), F32) + cb_ref[...]
    for i in range(CONV_WIDTH):
        off = halo - (CONV_WIDTH - 1) + i
        xc = xc + cw_ref[i:i + 1, :] * xpad[:, off:off + ts, :]
    xc = xc.reshape(nb * ts, c)

    neg_sp_lam = -RG_C * _softplus(-lam_ref[...])
    xb = xc.astype(BF16)
    for g in range(c // MXU_DIM):
        sl = slice(g * MXU_DIM, (g + 1) * MXU_DIM)
        r = _sigmoid_t(_dot(xb[:, sl], wa_ref[g]) + ba_ref[:, sl])
        gate_i = _sigmoid_t(_dot(xb[:, sl], wx_ref[g]) + bx_ref[:, sl])
        a = jnp.exp(r * neg_sp_lam[:, sl])
        var = (1.0 - a) * (1.0 + a)
        mult = jnp.where(var > 0.0, var * lax.rsqrt(var), 0.0)
        b = mult * (gate_i * xc[:, sl])
        for j in range(MXU_DIM // LANES):
            for bi in range(nb):
                seg = slice(bi * pitch, bi * pitch + ts)
                a_s[g * (MXU_DIM // LANES) + j, seg, :] = a[bi * ts:(bi + 1) * ts, j * LANES:(j + 1) * LANES]
                b_s[g * (MXU_DIM // LANES) + j, seg, :] = b[bi * ts:(bi + 1) * ts, j * LANES:(j + 1) * LANES]

    def step(t, hs):
        out = []
        for j in range(c // LANES):
            h = a_s[j, pl.ds(t, nb, stride=pitch), :] * hs[j] + b_s[j, pl.ds(t, nb, stride=pitch), :]
            b_s[j, pl.ds(t, nb, stride=pitch), :] = h
            out.append(h)
        return tuple(out)

    hs = lax.fori_loop(0, ts, step, tuple(h_s[j] for j in range(c // LANES)))
    for j in range(c // LANES):
        h_s[j] = hs[j]

    for j in range(c // LANES):
        for bi in range(nb):
            y = y_ref[bi, :, j * LANES:(j + 1) * LANES]
            gelu = 0.5 * y * (1.0 + jnp.tanh(0.7978845608028654 * (y + 0.044715 * (y * y * y))))
            h = b_s[j, bi * pitch:bi * pitch + ts, :]
            o_ref[bi, :, j * LANES:(j + 1) * LANES] = (h * gelu).astype(o_ref.dtype)


def _lru(xy, conv_w, conv_b, wa_bd, b_a, wx_bd, b_x, lam, *, bsz, seq, ts):
    c = xy.shape[-1] // 2
    nb = SUBLANES
    xy3 = xy.reshape(bsz, seq, 2 * c)
    return pl.pallas_call(
        functools.partial(_lru_body, ts=ts, nb=nb),
        grid=(bsz // nb, seq // ts),
        in_specs=[pl.BlockSpec((nb, ts, c), lambda b, s: (b, s, 0)),
                  pl.BlockSpec((nb, ts, c), lambda b, s: (b, s, 1)),
                  _resident(conv_w.shape), _resident((1, c)),
                  _resident(wa_bd.shape), _resident((1, c)),
                  _resident(wx_bd.shape), _resident((1, c)),
                  _resident((1, c))],
        out_specs=pl.BlockSpec((nb, ts, c), lambda b, s: (b, s, 0)),
        out_shape=jax.ShapeDtypeStruct((bsz, seq, c), BF16),
        scratch_shapes=[pltpu.VMEM((nb, ts + SUBLANES, c), F32),
                        pltpu.VMEM((c // LANES, nb * (ts + LRU_PITCH_PAD), LANES), F32),
                        pltpu.VMEM((c // LANES, nb * (ts + LRU_PITCH_PAD), LANES), F32),
                        pltpu.VMEM((c // LANES, nb, LANES), F32)],
        compiler_params=_params("parallel", "arbitrary"),
        name="lru",
    )(xy3, xy3, conv_w, conv_b, wa_bd, b_a, wx_bd, b_x, lam)


def _block_diag(w, group):
    n, k, _ = w.shape
    w = w.reshape(n // group, group, k, k)
    eye = jnp.eye(group, dtype=w.dtype)
    return jnp.einsum('gakl,ab->gakbl', w, eye).reshape(n // group, group * k, group * k)


def _to_row_tiles(ref, value, first_row=0):
    rows = value.shape[0]
    for j in range(SUBLANES):
        ref[pl.ds(first_row * SUBLANES + j, rows, stride=SUBLANES), :] = value[:, j * LANES:(j + 1) * LANES]


def _from_row_tiles(ref, rows):
    return jnp.concatenate([ref[pl.ds(j, rows, stride=SUBLANES), :] for j in range(SUBLANES)], axis=1)


def _merge_body(x_ref, osb_ref, hg_ref, gg_ref, gbias_ref, wsb_ref, wlru_ref, wout_ref, g2_ref,
                wrt_ref, br_ref, h1_ref, xs_ref, pos_ref, gate_ref, cnt_ref):
    tm, d = x_ref.shape

    y_sb = _dot(osb_ref[...], wsb_ref[...])
    y_lru = _dot(hg_ref[...], wlru_ref[...])
    g_a = _sigmoid(gg_ref[:, 0:d] + gbias_ref[:, 0:d])
    g_b = _sigmoid(gg_ref[:, d:2 * d] + gbias_ref[:, d:2 * d])
    merged = (g_a * y_sb + g_b * y_lru).astype(BF16)
    h1 = x_ref[...] + _dot(merged, wout_ref[...])
    h1_ref[...] = h1
    u2 = _rms(h1, g2_ref[...]).astype(BF16)
    logits = lax.dot_general(wrt_ref[...], u2, (((1,), (1,)), ((), ())),
                             preferred_element_type=F32) + br_ref[...]

    e_iota = lax.broadcasted_iota(I32, (N_EXPERTS, tm), 0)
    k_iota = lax.broadcasted_iota(I32, (TOP_K, tm), 0)
    vals = logits
    chosen = jnp.zeros((N_EXPERTS, tm), F32)
    top_v, hots = [], []
    for _ in range(TOP_K):
        m = jnp.max(vals, axis=0, keepdims=True)
        sel = jnp.min(jnp.where(vals == m, e_iota, N_EXPERTS), axis=0, keepdims=True)
        hot = e_iota == sel
        top_v.append(m)
        hots.append(hot)
        chosen = jnp.where(hot, 1.0, chosen)
        vals = jnp.where(hot, -jnp.inf, vals)
    exps = [jnp.exp(v - top_v[0]) for v in top_v]
    denom = exps[0] + exps[1] + exps[2] + exps[3]

    chosen_b = chosen.astype(BF16)
    t_row = lax.broadcasted_iota(I32, (tm, tm), 0)
    t_col = lax.broadcasted_iota(I32, (tm, tm), 1)
    earlier = jnp.where(t_row < t_col, 1.0, 0.0).astype(BF16)
    same_before = _dot(chosen_b, earlier)
    e_row = lax.broadcasted_iota(I32, (N_EXPERTS, N_EXPERTS), 0)
    e_col = lax.broadcasted_iota(I32, (N_EXPERTS, N_EXPERTS), 1)
    smaller = jnp.where(e_col < e_row, 1.0, 0.0).astype(BF16)
    first = jnp.sum(_dot(smaller, chosen_b), axis=1, keepdims=True)
    cnt_ref[...] = jnp.sum(chosen, axis=1, keepdims=True)

    where_to = first + same_before
    pos = jnp.zeros((TOP_K, tm), I32)
    gate = jnp.zeros((TOP_K, tm), F32)
    pos_k = []
    for k in range(TOP_K):
        pk = jnp.sum(jnp.where(hots[k], where_to, 0.0), axis=0, keepdims=True).astype(I32)
        pos_k.append(pk)
        pos = jnp.where(k_iota == k, pk, pos)
        gate = jnp.where(k_iota == k, exps[k] / denom, gate)
    pos_ref[...] = pos
    gate_ref[...] = gate

    for c in range(TOP_K):
        p_iota = lax.broadcasted_iota(I32, (tm, tm), 0) + c * tm
        hit = p_iota == pos_k[0]
        for k in range(1, TOP_K):
            hit = jnp.logical_or(hit, p_iota == pos_k[k])
        perm = jnp.where(hit, 1.0, 0.0).astype(BF16)
        _to_row_tiles(xs_ref, _dot(perm, u2), first_row=c * tm)


def _merge(x2, o_sb, hg, gg, gate_bias, wsb, wlru, wout, g2, wrt, br, *, tm):
    n, d = x2.shape
    tiles = n // tm
    assert d == SUBLANES * LANES, "row tile form needs one (8, 128) tile per row"
    row = lambda w: pl.BlockSpec((tm, w), lambda i: (i, 0))
    return pl.pallas_call(
        _merge_body,
        grid=(tiles,),
        in_specs=[row(d), row(SB_WIDTH), row(d), row(2 * d), _resident((1, 2 * d)),
                  _resident(wsb.shape), _resident(wlru.shape), _resident(wout.shape),
                  _resident((1, d)), _resident(wrt.shape), _resident((N_EXPERTS, 1))],
        out_specs=[row(d),
                   pl.BlockSpec((TOP_K * tm * SUBLANES, LANES), lambda i: (i, 0)),
                   pl.BlockSpec((TOP_K, tm), lambda i: (0, i)),
                   pl.BlockSpec((TOP_K, tm), lambda i: (0, i)),
                   pl.BlockSpec((N_EXPERTS, 1), lambda i: (i, 0))],
        out_shape=[jax.ShapeDtypeStruct((n, d), F32),
                   jax.ShapeDtypeStruct((TOP_K * n * SUBLANES, LANES), F32),
                   jax.ShapeDtypeStruct((TOP_K, n), I32),
                   jax.ShapeDtypeStruct((TOP_K, n), F32),
                   jax.ShapeDtypeStruct((tiles * N_EXPERTS, 1), F32)],
        compiler_params=_params("parallel"),
        name="merge",
    )(x2, o_sb, hg, gg, gate_bias, wsb, wlru, wout, g2, wrt, br)


def _experts_body(be_ref, r0_ref, t0_ref, nv_ref, nused_ref, cum_ref, off_ref,
                  xs_hbm, w1_ref, b1_ref, w2_ref, b2_ref, ys_hbm,
                  xbuf, ybuf, w1b, w2b, gsem, ssem, *, tiles, tile_rows):
    i = pl.program_id(0)
    rows = xbuf.shape[1] // SUBLANES
    ff = w2_ref.shape[1]
    n_used = nused_ref[0]
    slot = i % 2

    def for_runs(b, nv, fn):
        e = be_ref[b]
        r0 = r0_ref[b]
        end = r0 + nv

        def more(t):
            return jnp.logical_and(t < tiles, cum_ref[jnp.minimum(t, tiles) * N_EXPERTS + e] < end)

        def run(t):
            c0 = cum_ref[t * N_EXPERTS + e]
            c1 = cum_ref[(t + 1) * N_EXPERTS + e]
            lo = jnp.maximum(c0, r0)
            n_rows = jnp.minimum(c1, end) - lo

            @pl.when(n_rows > 0)
            def _():
                src = t * tile_rows + off_ref[t * N_EXPERTS + e] + (lo - c0)
                fn(pl.multiple_of(src * SUBLANES, SUBLANES),
                   pl.multiple_of((lo - r0) * SUBLANES, SUBLANES), n_rows * SUBLANES)

            return t + 1

        @pl.when(nv > 0)
        def _():
            lax.while_loop(more, run, t0_ref[b])

    def fetch(b, nv, s):
        def one(src, dst, size):
            pltpu.make_async_copy(xs_hbm.at[pl.ds(src, size)], xbuf.at[s, pl.ds(dst, size)],
                                  gsem.at[s]).start()
        for_runs(b, nv, one)

    def write_back(b, nv, s, sem):
        def one(src, dst, size):
            pltpu.make_async_copy(ybuf.at[s, pl.ds(dst, size)], ys_hbm.at[pl.ds(src, size)], sem).start()
        for_runs(b, nv, one)

    def wait_rows(nv, sem):
        @pl.when(nv > 0)
        def _():
            size = nv * SUBLANES
            pltpu.make_async_copy(xs_hbm.at[pl.ds(0, size)], xbuf.at[0, pl.ds(0, size)], sem).wait()

    prev = jnp.maximum(i - 1, 0)
    prev2 = jnp.maximum(i - 2, 0)
    nxt = jnp.minimum(i + 1, pl.num_programs(0) - 1)
    nv_prev = jnp.where(i >= 1, nv_ref[prev], 0)
    nv_prev2 = jnp.where(i >= 2, nv_ref[prev2], 0)

    @pl.when(i == 0)
    def _():
        xbuf[...] = jnp.zeros_like(xbuf)
        fetch(0, nv_ref[0], 0)

    @pl.when(jnp.logical_and(i < n_used, jnp.logical_or(i == 0, be_ref[i] != be_ref[prev])))
    def _():
        w1b[...] = w1_ref[0].astype(BF16)
        w2b[...] = w2_ref[0].astype(BF16)

    @pl.when(i <= n_used)
    def _():
        write_back(prev, nv_prev, 1 - slot, ssem.at[slot])

    @pl.when(i < n_used)
    def _():
        fetch(nxt, nv_ref[nxt], 1 - slot)
        wait_rows(nv_ref[i], gsem.at[slot])
        x = _from_row_tiles(xbuf.at[slot], rows).astype(BF16)
        hid = _dot(x, w1b[...]) + b1_ref[0]
        x_glu = jnp.minimum(hid[:, 0:ff], SWIGLU_LIMIT)
        x_lin = jnp.clip(hid[:, ff:2 * ff], -SWIGLU_LIMIT, SWIGLU_LIMIT)
        act = x_glu * _sigmoid(SWIGLU_ALPHA * x_glu) * (x_lin + 1.0)
        y = _dot(act.astype(BF16), w2b[...]) + b2_ref[0]
        wait_rows(nv_prev2, ssem.at[1 - slot])
        _to_row_tiles(ybuf.at[slot], y)

    @pl.when(i == n_used)
    def _():
        wait_rows(nv_prev2, ssem.at[1 - slot])
        wait_rows(nv_prev, ssem.at[slot])


def _experts(block_e, block_r0, block_t0, block_nv, n_used, cum, off, xs, w1, b1, w2, b2, *,
             rows, tiles, tile_rows):
    n_steps = block_e.shape[0]
    _, d, ff2 = w1.shape
    ff = ff2 // 2
    weights = lambda shape: pl.BlockSpec(shape, lambda i, be, *_: (be[i], 0, 0))
    grid_spec = pltpu.PrefetchScalarGridSpec(
        num_scalar_prefetch=7,
        grid=(n_steps,),
        in_specs=[pl.BlockSpec(memory_space=pl.ANY),
                  weights((1, d, ff2)), weights((1, 1, ff2)), weights((1, ff, d)), weights((1, 1, d))],
        out_specs=pl.BlockSpec(memory_space=pl.ANY),
        scratch_shapes=[pltpu.VMEM((2, rows * SUBLANES, LANES), F32),
                        pltpu.VMEM((2, rows * SUBLANES, LANES), F32),
                        pltpu.VMEM((d, ff2), BF16),
                        pltpu.VMEM((ff, d), BF16),
                        pltpu.SemaphoreType.DMA((2,)),
                        pltpu.SemaphoreType.DMA((2,))],
    )
    return pl.pallas_call(
        functools.partial(_experts_body, tiles=tiles, tile_rows=tile_rows),
        grid_spec=grid_spec,
        out_shape=jax.ShapeDtypeStruct(xs.shape, F32),
        compiler_params=_params("arbitrary"),
        name="experts",
    )(block_e, block_r0, block_t0, block_nv, n_used, cum, off, xs, w1,
      b1.reshape(N_EXPERTS, 1, ff2), w2, b2.reshape(N_EXPERTS, 1, d))


def _combine_body(pos_ref, gate_ref, h1_ref, g_ref, ys_ref, out_ref):
    tm = h1_ref.shape[0]
    p_iota = lax.broadcasted_iota(I32, (tm, TOP_K * tm), 1)
    pos = pos_ref[...]
    gate = gate_ref[...]
    g_mat = jnp.zeros((tm, TOP_K * tm), F32)
    for k in range(TOP_K):
        g_mat = jnp.where(p_iota == pos[:, k:k + 1], gate[:, k:k + 1], g_mat)
    ys = _from_row_tiles(ys_ref, TOP_K * tm).astype(BF16)
    out_ref[...] = _rms(h1_ref[...] + _dot(g_mat.astype(BF16), ys), g_ref[...])


def _combine(pos, gate, h1, g, ys, *, tm):
    n, d = h1.shape
    return pl.pallas_call(
        _combine_body,
        grid=(n // tm,),
        in_specs=[pl.BlockSpec((tm, TOP_K), lambda i: (i, 0)),
                  pl.BlockSpec((tm, TOP_K), lambda i: (i, 0)),
                  pl.BlockSpec((tm, d), lambda i: (i, 0)),
                  _resident((1, d)),
                  pl.BlockSpec((TOP_K * tm * SUBLANES, LANES), lambda i: (i, 0))],
        out_specs=pl.BlockSpec((tm, d), lambda i: (i, 0)),
        out_shape=jax.ShapeDtypeStruct((n, d), F32),
        compiler_params=_params("parallel"),
        name="combine",
    )(pos, gate, h1, g, ys)


def _block_tables(cnt, rows):
    cum = jnp.concatenate([jnp.zeros((1, N_EXPERTS), I32), jnp.cumsum(cnt, axis=0)])
    off = jnp.cumsum(cnt, axis=1) - cnt
    total = cum[-1]
    blocks = (total + rows - 1) // rows
    block_end = jnp.cumsum(blocks)
    return cum, off, total, blocks, block_end


def _layer(h2, bsz, seq, norm_mix_g, w_in, gate_bias, conv_w, conv_b, w_rg_a, b_rg_a, w_rg_x, b_rg_x,
           lru_lambda, w_sb_o, w_lru_o, w_out, norm_ffn_g, w_router, b_router, w1, b1, w2, b2, out_g):
    n, d = h2.shape
    tm = min(ROW_TILE, n)
    tiles = n // tm
    qkv, xy, gg = _inproj(h2, norm_mix_g.reshape(1, d), w_in.astype(BF16), tm=tm)
    o_sb = _attention(qkv, bsz=bsz, seq=seq, tq=min(ATTN_BLOCK, seq)).reshape(n, SB_WIDTH)
    group = MXU_DIM // (d // LRU_BLOCKS)
    hg = _lru(xy, conv_w, conv_b.reshape(1, d),
              _block_diag(w_rg_a, group).astype(BF16), b_rg_a.reshape(1, d),
              _block_diag(w_rg_x, group).astype(BF16), b_rg_x.reshape(1, d),
              lru_lambda.reshape(1, d), bsz=bsz, seq=seq, ts=min(LRU_TIME_BLOCK, seq)).reshape(n, d)
    h1, xs, pos, gate, cnt = _merge(
        h2, o_sb, hg, gg, gate_bias.reshape(1, 2 * d), w_sb_o.astype(BF16), w_lru_o.astype(BF16),
        w_out.astype(BF16), norm_ffn_g.reshape(1, d), w_router.T.astype(BF16),
        b_router.reshape(N_EXPERTS, 1), tm=tm)

    rows = MOE_ROWS
    n_steps = (n * TOP_K + N_EXPERTS * (rows - 1)) // rows + 1
    cnt = cnt.reshape(tiles, N_EXPERTS).astype(I32)
    cum, off, total, blocks, block_end = _block_tables(cnt, rows)
    step = jnp.arange(n_steps, dtype=I32)
    block_e = jnp.minimum(jnp.sum((step[:, None] >= block_end[None, :]).astype(I32), axis=1), N_EXPERTS - 1)
    block_r0 = (step - (block_end - blocks)[block_e]) * rows
    block_nv = jnp.clip(total[block_e] - block_r0, 0, rows)
    block_nv = jnp.where(step < block_end[-1], block_nv, 0)
    block_t0 = jnp.sum((cum[1:, :].T[block_e] <= block_r0[:, None]).astype(I32), axis=1)
    block_t0 = jnp.minimum(block_t0, tiles - 1)
    n_used = block_end[-1:].astype(I32)

    ys = _experts(block_e, block_r0, block_t0, block_nv, n_used, cum.reshape(-1), off.reshape(-1),
                  xs, w1, b1, w2, b2, rows=rows, tiles=tiles, tile_rows=TOP_K * tm)
    return _combine(pos.T, gate.T, h1, out_g.reshape(1, d), ys, tm=tm)


def kernel(x, norm_mix_g, w_in, gate_bias, conv_w, conv_b, w_rg_a, b_rg_a, w_rg_x, b_rg_x, lru_lambda,
           w_sb_o, w_lru_o, w_out, norm_ffn_g, w_router, b_router, w1, b1, w2, b2, norm_final_g):
    bsz, seq, d = x.shape
    depth = w_in.shape[0]
    assert depth == 1, "the final norm is fused into the last layer's combine"
    assert bsz % SUBLANES == 0 and d % MXU_DIM == 0
    h2 = x.reshape(bsz * seq, d)
    out = _layer(h2, bsz, seq, norm_mix_g[0], w_in[0], gate_bias[0], conv_w[0], conv_b[0], w_rg_a[0],
                 b_rg_a[0], w_rg_x[0], b_rg_x[0], lru_lambda[0], w_sb_o[0], w_lru_o[0], w_out[0],
                 norm_ffn_g[0], w_router[0], b_router[0], w1[0], b1[0], w2[0], b2[0], norm_final_g)
    return out.reshape(bsz, seq, d)
```

```python
import functools

import jax
import jax.numpy as jnp
from jax import lax
from jax.experimental import pallas as pl
from jax.experimental.pallas import tpu as pltpu

F32 = jnp.float32
BF16 = jnp.bfloat16
I32 = jnp.int32

SB_HEADS = 8
SB_HEAD_DIM = 64
SB_WIDTH = SB_HEADS * SB_HEAD_DIM
LRU_BLOCKS = 16
CONV_WIDTH = 4
RG_C = 8.0
N_EXPERTS = 32
TOP_K = 4
SWIGLU_ALPHA = 1.702
SWIGLU_LIMIT = 7.0
NORM_EPS = 1e-6

LANES = 128
SUBLANES = 8
MXU_DIM = 256
VMEM_LIMIT = 56 * 1024 * 1024

ROW_TILE = 512
ATTN_PAIRS = 4
ATTN_BLOCK = 256
ATTN_DEAD_CARRY = 104.0
INPROJ_TIME_BLOCK = 64
LRU_TIME_BLOCK = 128
MOE_ROWS = 512


def _rms(x, g):
    return x * lax.rsqrt(jnp.mean(x * x, axis=-1, keepdims=True) + NORM_EPS) * g


def _sigmoid(x):
    return 1.0 / (1.0 + jnp.exp(-x))


def _sigmoid_t(x):
    return 0.5 * jnp.tanh(0.5 * x) + 0.5


def _softplus(x):
    return jnp.maximum(x, 0.0) + jnp.log(1.0 + jnp.exp(-jnp.abs(x)))


def _dot(a, b):
    return jnp.dot(a, b, preferred_element_type=F32)


def _params(*sem):
    return pltpu.CompilerParams(dimension_semantics=sem, vmem_limit_bytes=VMEM_LIMIT)


def _resident(shape):
    return pl.BlockSpec(shape, lambda *_: (0,) * len(shape), pipeline_mode=pl.Buffered(1))


def _inproj_body(x_ref, g_ref, w_ref, qkv_ref, xr_ref, yr_ref, gg_ref, *, chunk):
    nb, tt, d = x_ref.shape
    u = _rms(x_ref[...].reshape(nb * tt, d), g_ref[...]).astype(BF16)
    col = 0
    for ref in (qkv_ref, None, yr_ref, gg_ref):
        width = d if ref is None else ref.shape[2]
        for c in range(0, width, chunk):
            val = _dot(u, w_ref[:, col + c:col + c + chunk])
            if ref is not None:
                ref[:, :, c:c + chunk] = val.astype(ref.dtype).reshape(nb, tt, chunk)
                continue
            for j in range(chunk // LANES):
                for b in range(nb):
                    xr_ref[c // LANES + j, pl.ds(b, tt, stride=nb), :] = (
                        val[b * tt:(b + 1) * tt, j * LANES:(j + 1) * LANES])
        col += width


def _inproj(x3, g, w_in_bf, *, tt):
    bsz, seq, d = x3.shape
    nb = SUBLANES
    steps = seq // tt
    tile = lambda w: pl.BlockSpec((nb, tt, w), lambda i: (i // steps, i % steps, 0))
    return pl.pallas_call(
        functools.partial(_inproj_body, chunk=512),
        grid=(bsz // nb * steps,),
        in_specs=[tile(d), _resident((1, d)), _resident(w_in_bf.shape)],
        out_specs=[tile(3 * SB_WIDTH),
                   pl.BlockSpec((d // LANES, nb * tt, LANES), lambda i: (0, i, 0)),
                   tile(d), tile(2 * d)],
        out_shape=[jax.ShapeDtypeStruct((bsz, seq, 3 * SB_WIDTH), BF16),
                   jax.ShapeDtypeStruct((d // LANES, bsz * seq, LANES), F32),
                   jax.ShapeDtypeStruct((bsz, seq, d), F32),
                   jax.ShapeDtypeStruct((bsz, seq, 2 * d), F32)],
        compiler_params=_params("parallel"),
        name="inproj",
    )(x3, g, w_in_bf)


def _attn_body(q_ref, k_ref, v_ref, o_ref, o_acc, c_acc, *, tq, npair):
    qi = pl.program_id(2)
    lane = lax.broadcasted_iota(I32, (1, LANES), 1)
    first = lane < SB_HEAD_DIM
    q_pairs = []
    for p in range(npair):
        q = q_ref[0, :, p * LANES:(p + 1) * LANES] * (SB_HEAD_DIM ** -0.5)
        zq = jnp.zeros_like(q)
        q_pairs.append(jnp.concatenate([jnp.where(first, q, zq), jnp.where(first, zq, q)], axis=0))
    row = lax.broadcasted_iota(I32, (2 * tq, tq), 0)
    row = jnp.where(row >= tq, row - tq, row)
    col = lax.broadcasted_iota(I32, (2 * tq, tq), 1)
    causal = col < row
    tri2 = jnp.where(row > col, 1.0, 0.0).astype(BF16)

    def block(j, diag):
        start = pl.multiple_of(j * tq, tq)
        zs, sps, sufs = [], [], []
        for p in range(npair):
            kj = k_ref[0, pl.ds(start, tq), p * LANES:(p + 1) * LANES]
            zs.append(lax.dot_general(q_pairs[p], kj, (((1,), (1,)), ((), ())), preferred_element_type=F32))
        for p in range(npair):
            sp = _softplus(zs[p])
            if diag:
                sp = jnp.where(causal, sp, 0.0)
            sps.append(sp)
            sp_hi = sp.astype(BF16)
            sp_lo = (sp - sp_hi.astype(F32)).astype(BF16)
            sufs.append(_dot(jnp.concatenate([sp_hi, sp_lo], axis=1), tri2))
        low = None
        for p in range(npair):
            vj = v_ref[0, pl.ds(start, tq), p * LANES:(p + 1) * LANES]
            zv = jnp.zeros_like(vj)
            v2 = jnp.concatenate([jnp.where(first, vj, zv), jnp.where(first, zv, vj)], axis=0)
            arg = zs[p] - sps[p] - sufs[p]
            if not diag:
                arg = arg - c_acc[p]
            att = jnp.exp(arg)
            if diag:
                att = jnp.where(causal, att, 0.0)
            att = att.astype(BF16)
            pv = _dot(jnp.concatenate([att[0:tq], att[tq:2 * tq]], axis=1), v2)
            if not diag:
                pv = pv + o_acc[:, p * LANES:(p + 1) * LANES]
            o_acc[:, p * LANES:(p + 1) * LANES] = pv
            c = jnp.sum(sps[p], axis=-1, keepdims=True)
            if not diag:
                c = c + c_acc[p]
            c_acc[p] = c
            m = jnp.min(c)
            low = m if low is None else jnp.minimum(low, m)
        return low

    def more(state):
        j, low = state
        return jnp.logical_and(j >= 0, low < ATTN_DEAD_CARRY)

    def step(state):
        j, _ = state
        return j - 1, block(j, False)

    lax.while_loop(more, step, (qi - 1, block(qi, True)))
    o_ref[0] = o_acc[...].astype(o_ref.dtype)


def _attention(qkv, *, bsz, seq, tq, npair=ATTN_PAIRS):
    qkv3 = qkv.reshape(bsz, seq, 3 * SB_WIDTH)
    width = npair * LANES
    groups = SB_WIDTH // width
    return pl.pallas_call(
        functools.partial(_attn_body, tq=tq, npair=npair),
        grid=(bsz, groups, seq // tq),
        in_specs=[pl.BlockSpec((1, tq, width), lambda b, p, i: (b, i, p)),
                  pl.BlockSpec((1, seq, width), lambda b, p, i: (b, 0, groups + p)),
                  pl.BlockSpec((1, seq, width), lambda b, p, i: (b, 0, 2 * groups + p))],
        out_specs=pl.BlockSpec((1, tq, width), lambda b, p, i: (b, i, p)),
        out_shape=jax.ShapeDtypeStruct((bsz, seq, SB_WIDTH), BF16),
        scratch_shapes=[pltpu.VMEM((tq, width), F32), pltpu.VMEM((npair, 2 * tq, 1), F32)],
        compiler_params=_params("parallel", "parallel", "arbitrary"),
        name="attn",
    )(qkv3, qkv3, qkv3)


def _lru_body(x_ref, y_ref, cw_ref, cb_ref, wa_ref, ba_ref, wx_ref, bx_ref, lam_ref, o_ref,
              xpad, a_s, b_s, h_s, *, ts, nb):
    s = pl.program_id(1)
    chunks = x_ref.shape[0]
    n_rows = ts * nb
    halo = (CONV_WIDTH - 1) * nb

    @pl.when(s == 0)
    def _():
        xpad[:, 0:halo, :] = jnp.zeros((chunks, halo, LANES), F32)
        h_s[...] = jnp.zeros_like(h_s)

    @pl.when(s > 0)
    def _():
        xpad[:, 0:halo, :] = xpad[:, n_rows:n_rows + halo, :]

    xpad[:, halo:halo + n_rows, :] = x_ref[...]

    def conv(j):
        lanes = slice(j * LANES, (j + 1) * LANES)
        xc = jnp.zeros((n_rows, LANES), F32) + cb_ref[:, lanes]
        for i in range(CONV_WIDTH):
            xc = xc + cw_ref[i:i + 1, lanes] * xpad[j, i * nb:i * nb + n_rows, :]
        return xc

    neg_sp_lam = -RG_C * _softplus(-lam_ref[...])
    per_group = MXU_DIM // LANES
    for g in range(chunks // per_group):
        sl = slice(g * MXU_DIM, (g + 1) * MXU_DIM)
        xc = jnp.concatenate([conv(g * per_group + j) for j in range(per_group)], axis=1)
        xb = xc.astype(BF16)
        r = _sigmoid_t(_dot(xb, wa_ref[g]) + ba_ref[:, sl])
        gate_i = _sigmoid_t(_dot(xb, wx_ref[g]) + bx_ref[:, sl])
        a = jnp.exp(r * neg_sp_lam[:, sl])
        var = (1.0 - a) * (1.0 + a)
        mult = jnp.where(var > 0.0, var * lax.rsqrt(var), 0.0)
        b = mult * (gate_i * xc)
        for j in range(per_group):
            a_s[g * per_group + j] = a[:, j * LANES:(j + 1) * LANES]
            b_s[g * per_group + j] = b[:, j * LANES:(j + 1) * LANES]

    def step(t, hs):
        rows = pl.ds(pl.multiple_of(t * nb, nb), nb)
        out = []
        for j in range(chunks):
            h = a_s[j, rows, :] * hs[j] + b_s[j, rows, :]
            b_s[j, rows, :] = h
            out.append(h)
        return tuple(out)

    hs = lax.fori_loop(0, ts, step, tuple(h_s[j] for j in range(chunks)))
    for j in range(chunks):
        h_s[j] = hs[j]

    for j in range(chunks):
        for bi in range(nb):
            y = y_ref[bi, :, j * LANES:(j + 1) * LANES]
            gelu = 0.5 * y * (1.0 + jnp.tanh(0.7978845608028654 * (y + 0.044715 * (y * y * y))))
            h = b_s[j, pl.ds(bi, ts, stride=nb), :]
            o_ref[bi, :, j * LANES:(j + 1) * LANES] = (h * gelu).astype(o_ref.dtype)


def _lru(xr, yr, conv_w, conv_b, wa_bd, b_a, wx_bd, b_x, lam, *, ts):
    bsz, seq, c = yr.shape
    nb = SUBLANES
    chunks = c // LANES
    steps = seq // ts
    return pl.pallas_call(
        functools.partial(_lru_body, ts=ts, nb=nb),
        grid=(bsz // nb, steps),
        in_specs=[pl.BlockSpec((chunks, ts * nb, LANES), lambda b, s: (0, b * steps + s, 0)),
                  pl.BlockSpec((nb, ts, c), lambda b, s: (b, s, 0)),
                  _resident(conv_w.shape), _resident((1, c)),
                  _resident(wa_bd.shape), _resident((1, c)),
                  _resident(wx_bd.shape), _resident((1, c)),
                  _resident((1, c))],
        out_specs=pl.BlockSpec((nb, ts, c), lambda b, s: (b, s, 0)),
        out_shape=jax.ShapeDtypeStruct((bsz, seq, c), BF16),
        scratch_shapes=[pltpu.VMEM((chunks, (ts + CONV_WIDTH - 1) * nb, LANES), F32),
                        pltpu.VMEM((chunks, ts * nb, LANES), F32),
                        pltpu.VMEM((chunks, ts * nb, LANES), F32),
                        pltpu.VMEM((chunks, nb, LANES), F32)],
        compiler_params=_params("parallel", "arbitrary"),
        name="lru",
    )(xr, yr, conv_w, conv_b, wa_bd, b_a, wx_bd, b_x, lam)


def _block_diag(w, group):
    n, k, _ = w.shape
    w = w.reshape(n // group, group, k, k)
    eye = jnp.eye(group, dtype=w.dtype)
    return jnp.einsum('gakl,ab->gakbl', w, eye).reshape(n // group, group * k, group * k)


def _to_row_tiles(ref, value, first_row=0):
    rows = value.shape[0]
    for j in range(SUBLANES):
        ref[pl.ds(first_row * SUBLANES + j, rows, stride=SUBLANES), :] = value[:, j * LANES:(j + 1) * LANES]


def _from_row_tiles(ref, rows):
    return jnp.concatenate([ref[pl.ds(j, rows, stride=SUBLANES), :] for j in range(SUBLANES)], axis=1)


def _merge_body(x_ref, osb_ref, hg_ref, gg_ref, gbias_ref, wsb_ref, wlru_ref, wout_ref, g2_ref,
                wrt_ref, br_ref, h1_ref, xs_ref, pos_ref, gate_ref, cnt_ref):
    tm, d = x_ref.shape

    y_sb = _dot(osb_ref[...], wsb_ref[...])
    y_lru = _dot(hg_ref[...], wlru_ref[...])
    g_a = _sigmoid(gg_ref[:, 0:d] + gbias_ref[:, 0:d])
    g_b = _sigmoid(gg_ref[:, d:2 * d] + gbias_ref[:, d:2 * d])
    merged = (g_a * y_sb + g_b * y_lru).astype(BF16)
    h1 = x_ref[...] + _dot(merged, wout_ref[...])
    h1_ref[...] = h1
    u2 = _rms(h1, g2_ref[...]).astype(BF16)
    logits = lax.dot_general(wrt_ref[...], u2, (((1,), (1,)), ((), ())),
                             preferred_element_type=F32) + br_ref[...]

    e_iota = lax.broadcasted_iota(I32, (N_EXPERTS, tm), 0)
    k_iota = lax.broadcasted_iota(I32, (TOP_K, tm), 0)
    vals = logits
    chosen = jnp.zeros((N_EXPERTS, tm), F32)
    top_v, hots = [], []
    for _ in range(TOP_K):
        m = jnp.max(vals, axis=0, keepdims=True)
        sel = jnp.min(jnp.where(vals == m, e_iota, N_EXPERTS), axis=0, keepdims=True)
        hot = e_iota == sel
        top_v.append(m)
        hots.append(hot)
        chosen = jnp.where(hot, 1.0, chosen)
        vals = jnp.where(hot, -jnp.inf, vals)
    exps = [jnp.exp(v - top_v[0]) for v in top_v]
    denom = exps[0] + exps[1] + exps[2] + exps[3]

    chosen_b = chosen.astype(BF16)
    t_row = lax.broadcasted_iota(I32, (tm, tm), 0)
    t_col = lax.broadcasted_iota(I32, (tm, tm), 1)
    earlier = jnp.where(t_row < t_col, 1.0, 0.0).astype(BF16)
    same_before = _dot(chosen_b, earlier)
    e_row = lax.broadcasted_iota(I32, (N_EXPERTS, N_EXPERTS), 0)
    e_col = lax.broadcasted_iota(I32, (N_EXPERTS, N_EXPERTS), 1)
    smaller = jnp.where(e_col < e_row, 1.0, 0.0).astype(BF16)
    first = jnp.sum(_dot(smaller, chosen_b), axis=1, keepdims=True)
    cnt_ref[...] = jnp.sum(chosen, axis=1, keepdims=True)

    where_to = first + same_before
    pos = jnp.zeros((TOP_K, tm), I32)
    gate = jnp.zeros((TOP_K, tm), F32)
    pos_k = []
    for k in range(TOP_K):
        pk = jnp.sum(jnp.where(hots[k], where_to, 0.0), axis=0, keepdims=True).astype(I32)
        pos_k.append(pk)
        pos = jnp.where(k_iota == k, pk, pos)
        gate = jnp.where(k_iota == k, exps[k] / denom, gate)
    pos_ref[...] = pos
    gate_ref[...] = gate

    for c in range(TOP_K):
        p_iota = lax.broadcasted_iota(I32, (tm, tm), 0) + c * tm
        hit = p_iota == pos_k[0]
        for k in range(1, TOP_K):
            hit = jnp.logical_or(hit, p_iota == pos_k[k])
        perm = jnp.where(hit, 1.0, 0.0).astype(BF16)
        _to_row_tiles(xs_ref, _dot(perm, u2), first_row=c * tm)


def _merge(x2, o_sb, hg, gg, gate_bias, wsb, wlru, wout, g2, wrt, br, *, tm):
    n, d = x2.shape
    tiles = n // tm
    assert d == SUBLANES * LANES, "row tile form needs one (8, 128) tile per row"
    row = lambda w: pl.BlockSpec((tm, w), lambda i: (i, 0))
    return pl.pallas_call(
        _merge_body,
        grid=(tiles,),
        in_specs=[row(d), row(SB_WIDTH), row(d), row(2 * d), _resident((1, 2 * d)),
                  _resident(wsb.shape), _resident(wlru.shape), _resident(wout.shape),
                  _resident((1, d)), _resident(wrt.shape), _resident((N_EXPERTS, 1))],
        out_specs=[row(d),
                   pl.BlockSpec((TOP_K * tm * SUBLANES, LANES), lambda i: (i, 0)),
                   pl.BlockSpec((TOP_K, tm), lambda i: (0, i)),
                   pl.BlockSpec((TOP_K, tm), lambda i: (0, i)),
                   pl.BlockSpec((N_EXPERTS, 1), lambda i: (i, 0))],
        out_shape=[jax.ShapeDtypeStruct((n, d), F32),
                   jax.ShapeDtypeStruct((TOP_K * n * SUBLANES, LANES), F32),
                   jax.ShapeDtypeStruct((TOP_K, n), I32),
                   jax.ShapeDtypeStruct((TOP_K, n), F32),
                   jax.ShapeDtypeStruct((tiles * N_EXPERTS, 1), F32)],
        compiler_params=_params("parallel"),
        name="merge",
    )(x2, o_sb, hg, gg, gate_bias, wsb, wlru, wout, g2, wrt, br)


def _experts_body(be_ref, r0_ref, t0_ref, nv_ref, nused_ref, cum_ref, off_ref,
                  xs_hbm, w1_ref, b1_ref, w2_ref, b2_ref, ys_hbm,
                  xbuf, ybuf, w1b, w2b, gsem, ssem, *, tiles, tile_rows):
    i = pl.program_id(0)
    rows = xbuf.shape[1] // SUBLANES
    ff = w2_ref.shape[1]
    n_used = nused_ref[0]
    slot = i % 2

    def for_runs(b, nv, fn):
        e = be_ref[b]
        r0 = r0_ref[b]
        end = r0 + nv

        def more(t):
            return jnp.logical_and(t < tiles, cum_ref[jnp.minimum(t, tiles) * N_EXPERTS + e] < end)

        def run(t):
            c0 = cum_ref[t * N_EXPERTS + e]
            c1 = cum_ref[(t + 1) * N_EXPERTS + e]
            lo = jnp.maximum(c0, r0)
            n_rows = jnp.minimum(c1, end) - lo

            @pl.when(n_rows > 0)
            def _():
                src = t * tile_rows + off_ref[t * N_EXPERTS + e] + (lo - c0)
                fn(pl.multiple_of(src * SUBLANES, SUBLANES),
                   pl.multiple_of((lo - r0) * SUBLANES, SUBLANES), n_rows * SUBLANES)

            return t + 1

        @pl.when(nv > 0)
        def _():
            lax.while_loop(more, run, t0_ref[b])

    def fetch(b, nv, s):
        def one(src, dst, size):
            pltpu.make_async_copy(xs_hbm.at[pl.ds(src, size)], xbuf.at[s, pl.ds(dst, size)],
                                  gsem.at[s]).start()
        for_runs(b, nv, one)

    def write_back(b, nv, s, sem):
        def one(src, dst, size):
            pltpu.make_async_copy(ybuf.at[s, pl.ds(dst, size)], ys_hbm.at[pl.ds(src, size)], sem).start()
        for_runs(b, nv, one)

    def wait_rows(nv, sem):
        @pl.when(nv > 0)
        def _():
            size = nv * SUBLANES
            pltpu.make_async_copy(xs_hbm.at[pl.ds(0, size)], xbuf.at[0, pl.ds(0, size)], sem).wait()

    prev = jnp.maximum(i - 1, 0)
    prev2 = jnp.maximum(i - 2, 0)
    nxt = jnp.minimum(i + 1, pl.num_programs(0) - 1)
    nv_prev = jnp.where(i >= 1, nv_ref[prev], 0)
    nv_prev2 = jnp.where(i >= 2, nv_ref[prev2], 0)

    @pl.when(i == 0)
    def _():
        xbuf[...] = jnp.zeros_like(xbuf)
        fetch(0, nv_ref[0], 0)

    @pl.when(jnp.logical_and(i < n_used, jnp.logical_or(i == 0, be_ref[i] != be_ref[prev])))
    def _():
        w1b[...] = w1_ref[0].astype(BF16)
        w2b[...] = w2_ref[0].astype(BF16)

    @pl.when(i <= n_used)
    def _():
        write_back(prev, nv_prev, 1 - slot, ssem.at[slot])

    @pl.when(i < n_used)
    def _():
        fetch(nxt, nv_ref[nxt], 1 - slot)
        wait_rows(nv_ref[i], gsem.at[slot])
        x = _from_row_tiles(xbuf.at[slot], rows).astype(BF16)
        hid = _dot(x, w1b[...]) + b1_ref[0]
        x_glu = jnp.minimum(hid[:, 0:ff], SWIGLU_LIMIT)
        x_lin = jnp.clip(hid[:, ff:2 * ff], -SWIGLU_LIMIT, SWIGLU_LIMIT)
        act = x_glu * _sigmoid(SWIGLU_ALPHA * x_glu) * (x_lin + 1.0)
        y = _dot(act.astype(BF16), w2b[...]) + b2_ref[0]
        wait_rows(nv_prev2, ssem.at[1 - slot])
        _to_row_tiles(ybuf.at[slot], y)

    @pl.when(i == n_used)
    def _():
        wait_rows(nv_prev2, ssem.at[1 - slot])
        wait_rows(nv_prev, ssem.at[slot])


def _experts(block_e, block_r0, block_t0, block_nv, n_used, cum, off, xs, w1, b1, w2, b2, *,
             rows, tiles, tile_rows):
    n_steps = block_e.shape[0]
    _, d, ff2 = w1.shape
    ff = ff2 // 2
    weights = lambda shape: pl.BlockSpec(shape, lambda i, be, *_: (be[i], 0, 0))
    grid_spec = pltpu.PrefetchScalarGridSpec(
        num_scalar_prefetch=7,
        grid=(n_steps,),
        in_specs=[pl.BlockSpec(memory_space=pl.ANY),
                  weights((1, d, ff2)), weights((1, 1, ff2)), weights((1, ff, d)), weights((1, 1, d))],
        out_specs=pl.BlockSpec(memory_space=pl.ANY),
        scratch_shapes=[pltpu.VMEM((2, rows * SUBLANES, LANES), F32),
                        pltpu.VMEM((2, rows * SUBLANES, LANES), F32),
                        pltpu.VMEM((d, ff2), BF16),
                        pltpu.VMEM((ff, d), BF16),
                        pltpu.SemaphoreType.DMA((2,)),
                        pltpu.SemaphoreType.DMA((2,))],
    )
    return pl.pallas_call(
        functools.partial(_experts_body, tiles=tiles, tile_rows=tile_rows),
        grid_spec=grid_spec,
        out_shape=jax.ShapeDtypeStruct(xs.shape, F32),
        compiler_params=_params("arbitrary"),
        name="experts",
    )(block_e, block_r0, block_t0, block_nv, n_used, cum, off, xs, w1,
      b1.reshape(N_EXPERTS, 1, ff2), w2, b2.reshape(N_EXPERTS, 1, d))


def _combine_body(pos_ref, gate_ref, h1_ref, g_ref, ys_ref, out_ref):
    tm = h1_ref.shape[0]
    p_iota = lax.broadcasted_iota(I32, (tm, TOP_K * tm), 1)
    pos = pos_ref[...]
    gate = gate_ref[...]
    g_mat = jnp.zeros((tm, TOP_K * tm), F32)
    for k in range(TOP_K):
        g_mat = jnp.where(p_iota == pos[:, k:k + 1], gate[:, k:k + 1], g_mat)
    ys = _from_row_tiles(ys_ref, TOP_K * tm).astype(BF16)
    out_ref[...] = _rms(h1_ref[...] + _dot(g_mat.astype(BF16), ys), g_ref[...])


def _combine(pos, gate, h1, g, ys, *, tm):
    n, d = h1.shape
    return pl.pallas_call(
        _combine_body,
        grid=(n // tm,),
        in_specs=[pl.BlockSpec((tm, TOP_K), lambda i: (i, 0)),
                  pl.BlockSpec((tm, TOP_K), lambda i: (i, 0)),
                  pl.BlockSpec((tm, d), lambda i: (i, 0)),
                  _resident((1, d)),
                  pl.BlockSpec((TOP_K * tm * SUBLANES, LANES), lambda i: (i, 0))],
        out_specs=pl.BlockSpec((tm, d), lambda i: (i, 0)),
        out_shape=jax.ShapeDtypeStruct((n, d), F32),
        compiler_params=_params("parallel"),
        name="combine",
    )(pos, gate, h1, g, ys)


def _block_tables(cnt, rows):
    cum = jnp.concatenate([jnp.zeros((1, N_EXPERTS), I32), jnp.cumsum(cnt, axis=0)])
    off = jnp.cumsum(cnt, axis=1) - cnt
    total = cum[-1]
    blocks = (total + rows - 1) // rows
    block_end = jnp.cumsum(blocks)
    return cum, off, total, blocks, block_end


def _layer(h2, bsz, seq, norm_mix_g, w_in, gate_bias, conv_w, conv_b, w_rg_a, b_rg_a, w_rg_x, b_rg_x,
           lru_lambda, w_sb_o, w_lru_o, w_out, norm_ffn_g, w_router, b_router, w1, b1, w2, b2, out_g):
    n, d = h2.shape
    tm = min(ROW_TILE, n)
    tiles = n // tm
    qkv, xr, yr, gg = _inproj(h2.reshape(bsz, seq, d), norm_mix_g.reshape(1, d), w_in.astype(BF16),
                              tt=min(INPROJ_TIME_BLOCK, seq))
    o_sb = _attention(qkv, bsz=bsz, seq=seq, tq=min(ATTN_BLOCK, seq)).reshape(n, SB_WIDTH)
    group = MXU_DIM // (d // LRU_BLOCKS)
    hg = _lru(xr, yr, conv_w, conv_b.reshape(1, d),
              _block_diag(w_rg_a, group).astype(BF16), b_rg_a.reshape(1, d),
              _block_diag(w_rg_x, group).astype(BF16), b_rg_x.reshape(1, d),
              lru_lambda.reshape(1, d), ts=min(LRU_TIME_BLOCK, seq)).reshape(n, d)
    h1, xs, pos, gate, cnt = _merge(
        h2, o_sb, hg, gg.reshape(n, 2 * d), gate_bias.reshape(1, 2 * d), w_sb_o.astype(BF16),
        w_lru_o.astype(BF16),
        w_out.astype(BF16), norm_ffn_g.reshape(1, d), w_router.T.astype(BF16),
        b_router.reshape(N_EXPERTS, 1), tm=tm)

    rows = MOE_ROWS
    n_steps = (n * TOP_K + N_EXPERTS * (rows - 1)) // rows + 1
    cnt = cnt.reshape(tiles, N_EXPERTS).astype(I32)
    cum, off, total, blocks, block_end = _block_tables(cnt, rows)
    step = jnp.arange(n_steps, dtype=I32)
    block_e = jnp.minimum(jnp.sum((step[:, None] >= block_end[None, :]).astype(I32), axis=1), N_EXPERTS - 1)
    block_r0 = (step - (block_end - blocks)[block_e]) * rows
    block_nv = jnp.clip(total[block_e] - block_r0, 0, rows)
    block_nv = jnp.where(step < block_end[-1], block_nv, 0)
    block_t0 = jnp.sum((cum[1:, :].T[block_e] <= block_r0[:, None]).astype(I32), axis=1)
    block_t0 = jnp.minimum(block_t0, tiles - 1)
    n_used = block_end[-1:].astype(I32)

    ys = _experts(block_e, block_r0, block_t0, block_nv, n_used, cum.reshape(-1), off.reshape(-1),
                  xs, w1, b1, w2, b2, rows=rows, tiles=tiles, tile_rows=TOP_K * tm)
    return _combine(pos.T, gate.T, h1, out_g.reshape(1, d), ys, tm=tm)


def kernel(x, norm_mix_g, w_in, gate_bias, conv_w, conv_b, w_rg_a, b_rg_a, w_rg_x, b_rg_x, lru_lambda,
           w_sb_o, w_lru_o, w_out, norm_ffn_g, w_router, b_router, w1, b1, w2, b2, norm_final_g):
    bsz, seq, d = x.shape
    depth = w_in.shape[0]
    assert depth == 1, "the final norm is fused into the last layer's combine"
    assert bsz % SUBLANES == 0 and d % MXU_DIM == 0
    h2 = x.reshape(bsz * seq, d)
    out = _layer(h2, bsz, seq, norm_mix_g[0], w_in[0], gate_bias[0], conv_w[0], conv_b[0], w_rg_a[0],
                 b_rg_a[0], w_rg_x[0], b_rg_x[0], lru_lambda[0], w_sb_o[0], w_lru_o[0], w_out[0],
                 norm_ffn_g[0], w_router[0], b_router[0], w1[0], b1[0], w2[0], b2[0], norm_final_g)
    return out.reshape(bsz, seq, d)
```

```python
import functools

import jax
import jax.numpy as jnp
from jax import lax
from jax.experimental import pallas as pl
from jax.experimental.pallas import tpu as pltpu

F32 = jnp.float32
BF16 = jnp.bfloat16
I32 = jnp.int32

SB_HEADS = 8
SB_HEAD_DIM = 64
SB_WIDTH = SB_HEADS * SB_HEAD_DIM
LRU_BLOCKS = 16
CONV_WIDTH = 4
RG_C = 8.0
N_EXPERTS = 32
TOP_K = 4
SWIGLU_ALPHA = 1.702
SWIGLU_LIMIT = 7.0
NORM_EPS = 1e-6

LANES = 128
SUBLANES = 8
MXU_DIM = 256
VMEM_LIMIT = 56 * 1024 * 1024

ROW_TILE = 512
ATTN_PAIRS = 4
ATTN_BLOCK = 256
ATTN_DEAD_CARRY = 104.0
INPROJ_TIME_BLOCK = 64
MOE_ROWS = 512


def _rms(x, g):
    return x * lax.rsqrt(jnp.mean(x * x, axis=-1, keepdims=True) + NORM_EPS) * g


def _sigmoid(x):
    return 1.0 / (1.0 + jnp.exp(-x))


def _sigmoid_t(x):
    return 0.5 * jnp.tanh(0.5 * x) + 0.5


def _softplus(x):
    return jnp.maximum(x, 0.0) + jnp.log(1.0 + jnp.exp(-jnp.abs(x)))


def _dot(a, b):
    return jnp.dot(a, b, preferred_element_type=F32)


def _params(*sem):
    return pltpu.CompilerParams(dimension_semantics=sem, vmem_limit_bytes=VMEM_LIMIT)


def _resident(shape):
    return pl.BlockSpec(shape, lambda *_: (0,) * len(shape), pipeline_mode=pl.Buffered(1))


def _inproj_body(x_ref, g_ref, w_ref, cw_ref, cb_ref, wa_ref, ba_ref, wx_ref, bx_ref, lam_ref,
                 qkv_ref, gg_ref, hg_ref, xpad, y_s, a_s, b_s, h_s, *, chunk):
    s = pl.program_id(1)
    nb, tt, d = x_ref.shape
    chunks = d // LANES
    n_rows = tt * nb
    halo = (CONV_WIDTH - 1) * nb
    u = _rms(x_ref[...].reshape(n_rows, d), g_ref[...]).astype(BF16)

    def project(col, width, store):
        for c in range(0, width, chunk):
            store(c, _dot(u, w_ref[:, col + c:col + c + chunk]))

    @pl.when(s == 0)
    def _():
        xpad[:, 0:halo, :] = jnp.zeros((chunks, halo, LANES), F32)
        h_s[...] = jnp.zeros_like(h_s)

    @pl.when(s > 0)
    def _():
        xpad[:, 0:halo, :] = xpad[:, n_rows:n_rows + halo, :]

    def store_x(c, val):
        for j in range(chunk // LANES):
            for b in range(nb):
                xpad[c // LANES + j, pl.ds(halo + b, tt, stride=nb), :] = (
                    val[b * tt:(b + 1) * tt, j * LANES:(j + 1) * LANES])

    def store_y(c, val):
        y_s[:, c:c + chunk] = val

    def store_to(ref):
        def store(c, val):
            ref[:, :, c:c + chunk] = val.astype(ref.dtype).reshape(nb, tt, chunk)
        return store

    x_col = qkv_ref.shape[2]
    project(x_col, d, store_x)
    todo = ([(x_col + d + c, store_y, c) for c in range(0, d, chunk)]
            + [(c, store_to(qkv_ref), c) for c in range(0, x_col, chunk)]
            + [(x_col + 2 * d + c, store_to(gg_ref), c) for c in range(0, 2 * d, chunk)])

    def project_next(count=1):
        for _ in range(count):
            if todo:
                col, store, c = todo.pop(0)
                store(c, _dot(u, w_ref[:, col:col + chunk]))

    def conv(j):
        lanes = slice(j * LANES, (j + 1) * LANES)
        xc = jnp.zeros((n_rows, LANES), F32) + cb_ref[:, lanes]
        for i in range(CONV_WIDTH):
            xc = xc + cw_ref[i:i + 1, lanes] * xpad[j, i * nb:i * nb + n_rows, :]
        return xc

    neg_sp_lam = -RG_C * _softplus(-lam_ref[...])
    per_group = MXU_DIM // LANES
    for g in range(chunks // per_group):
        project_next(2)
        sl = slice(g * MXU_DIM, (g + 1) * MXU_DIM)
        xc = jnp.concatenate([conv(g * per_group + j) for j in range(per_group)], axis=1)
        xb = xc.astype(BF16)
        r = _sigmoid_t(_dot(xb, wa_ref[g]) + ba_ref[:, sl])
        gate_i = _sigmoid_t(_dot(xb, wx_ref[g]) + bx_ref[:, sl])
        a = jnp.exp(r * neg_sp_lam[:, sl])
        var = (1.0 - a) * (1.0 + a)
        mult = jnp.where(var > 0.0, var * lax.rsqrt(var), 0.0)
        b = mult * (gate_i * xc)
        for j in range(per_group):
            a_s[g * per_group + j] = a[:, j * LANES:(j + 1) * LANES]
            b_s[g * per_group + j] = b[:, j * LANES:(j + 1) * LANES]

    def step(t, hs):
        rows = pl.ds(pl.multiple_of(t * nb, nb), nb)
        out = []
        for j in range(chunks):
            h = a_s[j, rows, :] * hs[j] + b_s[j, rows, :]
            b_s[j, rows, :] = h
            out.append(h)
        return tuple(out)

    hs = lax.fori_loop(0, tt, step, tuple(h_s[j] for j in range(chunks)))
    for j in range(chunks):
        h_s[j] = hs[j]

    for j in range(chunks):
        project_next(len(todo) if j == chunks - 1 else 1)
        for bi in range(nb):
            y = y_s[bi * tt:(bi + 1) * tt, j * LANES:(j + 1) * LANES]
            gelu = 0.5 * y * (1.0 + jnp.tanh(0.7978845608028654 * (y + 0.044715 * (y * y * y))))
            h = b_s[j, pl.ds(bi, tt, stride=nb), :]
            hg_ref[bi, :, j * LANES:(j + 1) * LANES] = (h * gelu).astype(hg_ref.dtype)


def _inproj(x3, g, w_in_bf, conv_w, conv_b, wa_bd, b_a, wx_bd, b_x, lam, *, tt):
    bsz, seq, d = x3.shape
    nb = SUBLANES
    chunks = d // LANES
    tile = lambda w: pl.BlockSpec((nb, tt, w), lambda b, s: (b, s, 0))
    return pl.pallas_call(
        functools.partial(_inproj_body, chunk=MXU_DIM),
        grid=(bsz // nb, seq // tt),
        in_specs=[tile(d), _resident((1, d)), _resident(w_in_bf.shape),
                  _resident(conv_w.shape), _resident((1, d)),
                  _resident(wa_bd.shape), _resident((1, d)),
                  _resident(wx_bd.shape), _resident((1, d)),
                  _resident((1, d))],
        out_specs=[tile(3 * SB_WIDTH), tile(2 * d), tile(d)],
        out_shape=[jax.ShapeDtypeStruct((bsz, seq, 3 * SB_WIDTH), BF16),
                   jax.ShapeDtypeStruct((bsz, seq, 2 * d), F32),
                   jax.ShapeDtypeStruct((bsz, seq, d), BF16)],
        scratch_shapes=[pltpu.VMEM((chunks, (tt + CONV_WIDTH - 1) * nb, LANES), F32),
                        pltpu.VMEM((nb * tt, d), F32),
                        pltpu.VMEM((chunks, tt * nb, LANES), F32),
                        pltpu.VMEM((chunks, tt * nb, LANES), F32),
                        pltpu.VMEM((chunks, nb, LANES), F32)],
        compiler_params=_params("parallel", "arbitrary"),
        name="inproj",
    )(x3, g, w_in_bf, conv_w, conv_b, wa_bd, b_a, wx_bd, b_x, lam)


def _attn_body(q_ref, k_ref, v_ref, o_ref, o_acc, c_acc, *, tq, npair):
    qi = pl.program_id(2)
    lane = lax.broadcasted_iota(I32, (1, LANES), 1)
    first = lane < SB_HEAD_DIM
    q_pairs = []
    for p in range(npair):
        q = q_ref[0, :, p * LANES:(p + 1) * LANES] * (SB_HEAD_DIM ** -0.5)
        zq = jnp.zeros_like(q)
        q_pairs.append(jnp.concatenate([jnp.where(first, q, zq), jnp.where(first, zq, q)], axis=0))
    row = lax.broadcasted_iota(I32, (2 * tq, tq), 0)
    row = jnp.where(row >= tq, row - tq, row)
    col = lax.broadcasted_iota(I32, (2 * tq, tq), 1)
    causal = col < row
    tri2 = jnp.where(row > col, 1.0, 0.0).astype(BF16)

    def block(j, diag):
        start = pl.multiple_of(j * tq, tq)
        zs, sps, sufs = [], [], []
        for p in range(npair):
            kj = k_ref[0, pl.ds(start, tq), p * LANES:(p + 1) * LANES]
            zs.append(lax.dot_general(q_pairs[p], kj, (((1,), (1,)), ((), ())), preferred_element_type=F32))
        for p in range(npair):
            sp = _softplus(zs[p])
            if diag:
                sp = jnp.where(causal, sp, 0.0)
            sps.append(sp)
            sp_hi = sp.astype(BF16)
            sp_lo = (sp - sp_hi.astype(F32)).astype(BF16)
            sufs.append(_dot(jnp.concatenate([sp_hi, sp_lo], axis=1), tri2))
        low = None
        for p in range(npair):
            vj = v_ref[0, pl.ds(start, tq), p * LANES:(p + 1) * LANES]
            zv = jnp.zeros_like(vj)
            v2 = jnp.concatenate([jnp.where(first, vj, zv), jnp.where(first, zv, vj)], axis=0)
            arg = zs[p] - sps[p] - sufs[p]
            if not diag:
                arg = arg - c_acc[p]
            att = jnp.exp(arg)
            if diag:
                att = jnp.where(causal, att, 0.0)
            att = att.astype(BF16)
            pv = _dot(jnp.concatenate([att[0:tq], att[tq:2 * tq]], axis=1), v2)
            if not diag:
                pv = pv + o_acc[:, p * LANES:(p + 1) * LANES]
            o_acc[:, p * LANES:(p + 1) * LANES] = pv
            c = jnp.sum(sps[p], axis=-1, keepdims=True)
            if not diag:
                c = c + c_acc[p]
            c_acc[p] = c
            m = jnp.min(c)
            low = m if low is None else jnp.minimum(low, m)
        return low

    def more(state):
        j, low = state
        return jnp.logical_and(j >= 0, low < ATTN_DEAD_CARRY)

    def step(state):
        j, _ = state
        return j - 1, block(j, False)

    lax.while_loop(more, step, (qi - 1, block(qi, True)))
    o_ref[0] = o_acc[...].astype(o_ref.dtype)


def _attention(qkv, *, bsz, seq, tq, npair=ATTN_PAIRS):
    qkv3 = qkv.reshape(bsz, seq, 3 * SB_WIDTH)
    width = npair * LANES
    groups = SB_WIDTH // width
    return pl.pallas_call(
        functools.partial(_attn_body, tq=tq, npair=npair),
        grid=(bsz, groups, seq // tq),
        in_specs=[pl.BlockSpec((1, tq, width), lambda b, p, i: (b, i, p)),
                  pl.BlockSpec((1, seq, width), lambda b, p, i: (b, 0, groups + p)),
                  pl.BlockSpec((1, seq, width), lambda b, p, i: (b, 0, 2 * groups + p))],
        out_specs=pl.BlockSpec((1, tq, width), lambda b, p, i: (b, i, p)),
        out_shape=jax.ShapeDtypeStruct((bsz, seq, SB_WIDTH), BF16),
        scratch_shapes=[pltpu.VMEM((tq, width), F32), pltpu.VMEM((npair, 2 * tq, 1), F32)],
        compiler_params=_params("parallel", "parallel", "arbitrary"),
        name="attn",
    )(qkv3, qkv3, qkv3)


def _block_diag(w, group):
    n, k, _ = w.shape
    w = w.reshape(n // group, group, k, k)
    eye = jnp.eye(group, dtype=w.dtype)
    return jnp.einsum('gakl,ab->gakbl', w, eye).reshape(n // group, group * k, group * k)


def _to_row_tiles(ref, value, first_row=0):
    rows = value.shape[0]
    for j in range(SUBLANES):
        ref[pl.ds(first_row * SUBLANES + j, rows, stride=SUBLANES), :] = value[:, j * LANES:(j + 1) * LANES]


def _from_row_tiles(ref, rows):
    return jnp.concatenate([ref[pl.ds(j, rows, stride=SUBLANES), :] for j in range(SUBLANES)], axis=1)


def _merge_body(x_ref, osb_ref, hg_ref, gg_ref, gbias_ref, wsb_ref, wlru_ref, wout_ref, g2_ref,
                wrt_ref, br_ref, h1_ref, xs_ref, pos_ref, gate_ref, cnt_ref):
    tm, d = x_ref.shape

    y_sb = _dot(osb_ref[...], wsb_ref[...])
    y_lru = _dot(hg_ref[...], wlru_ref[...])
    g_a = _sigmoid(gg_ref[:, 0:d] + gbias_ref[:, 0:d])
    g_b = _sigmoid(gg_ref[:, d:2 * d] + gbias_ref[:, d:2 * d])
    merged = (g_a * y_sb + g_b * y_lru).astype(BF16)
    h1 = x_ref[...] + _dot(merged, wout_ref[...])
    h1_ref[...] = h1
    u2 = _rms(h1, g2_ref[...]).astype(BF16)
    logits = lax.dot_general(wrt_ref[...], u2, (((1,), (1,)), ((), ())),
                             preferred_element_type=F32) + br_ref[...]

    e_iota = lax.broadcasted_iota(I32, (N_EXPERTS, tm), 0)
    k_iota = lax.broadcasted_iota(I32, (TOP_K, tm), 0)
    vals = logits
    chosen = jnp.zeros((N_EXPERTS, tm), F32)
    top_v, hots = [], []
    for _ in range(TOP_K):
        m = jnp.max(vals, axis=0, keepdims=True)
        sel = jnp.min(jnp.where(vals == m, e_iota, N_EXPERTS), axis=0, keepdims=True)
        hot = e_iota == sel
        top_v.append(m)
        hots.append(hot)
        chosen = jnp.where(hot, 1.0, chosen)
        vals = jnp.where(hot, -jnp.inf, vals)
    exps = [jnp.exp(v - top_v[0]) for v in top_v]
    denom = exps[0] + exps[1] + exps[2] + exps[3]

    chosen_b = chosen.astype(BF16)
    t_row = lax.broadcasted_iota(I32, (tm, tm), 0)
    t_col = lax.broadcasted_iota(I32, (tm, tm), 1)
    earlier = jnp.where(t_row < t_col, 1.0, 0.0).astype(BF16)
    same_before = _dot(chosen_b, earlier)
    e_row = lax.broadcasted_iota(I32, (N_EXPERTS, N_EXPERTS), 0)
    e_col = lax.broadcasted_iota(I32, (N_EXPERTS, N_EXPERTS), 1)
    smaller = jnp.where(e_col < e_row, 1.0, 0.0).astype(BF16)
    first = jnp.sum(_dot(smaller, chosen_b), axis=1, keepdims=True)
    cnt_ref[...] = jnp.sum(chosen, axis=1, keepdims=True)

    where_to = first + same_before
    pos = jnp.zeros((TOP_K, tm), I32)
    gate = jnp.zeros((TOP_K, tm), F32)
    pos_k = []
    for k in range(TOP_K):
        pk = jnp.sum(jnp.where(hots[k], where_to, 0.0), axis=0, keepdims=True).astype(I32)
        pos_k.append(pk)
        pos = jnp.where(k_iota == k, pk, pos)
        gate = jnp.where(k_iota == k, exps[k] / denom, gate)
    pos_ref[...] = pos
    gate_ref[...] = gate

    for c in range(TOP_K):
        p_iota = lax.broadcasted_iota(I32, (tm, tm), 0) + c * tm
        hit = p_iota == pos_k[0]
        for k in range(1, TOP_K):
            hit = jnp.logical_or(hit, p_iota == pos_k[k])
        perm = jnp.where(hit, 1.0, 0.0).astype(BF16)
        _to_row_tiles(xs_ref, _dot(perm, u2), first_row=c * tm)


def _merge(x2, o_sb, hg, gg, gate_bias, wsb, wlru, wout, g2, wrt, br, *, tm):
    n, d = x2.shape
    tiles = n // tm
    assert d == SUBLANES * LANES, "row tile form needs one (8, 128) tile per row"
    row = lambda w: pl.BlockSpec((tm, w), lambda i: (i, 0))
    return pl.pallas_call(
        _merge_body,
        grid=(tiles,),
        in_specs=[row(d), row(SB_WIDTH), row(d), row(2 * d), _resident((1, 2 * d)),
                  _resident(wsb.shape), _resident(wlru.shape), _resident(wout.shape),
                  _resident((1, d)), _resident(wrt.shape), _resident((N_EXPERTS, 1))],
        out_specs=[row(d),
                   pl.BlockSpec((TOP_K * tm * SUBLANES, LANES), lambda i: (i, 0)),
                   pl.BlockSpec((TOP_K, tm), lambda i: (0, i)),
                   pl.BlockSpec((TOP_K, tm), lambda i: (0, i)),
                   pl.BlockSpec((N_EXPERTS, 1), lambda i: (i, 0))],
        out_shape=[jax.ShapeDtypeStruct((n, d), F32),
                   jax.ShapeDtypeStruct((TOP_K * n * SUBLANES, LANES), F32),
                   jax.ShapeDtypeStruct((TOP_K, n), I32),
                   jax.ShapeDtypeStruct((TOP_K, n), F32),
                   jax.ShapeDtypeStruct((tiles * N_EXPERTS, 1), F32)],
        compiler_params=_params("parallel"),
        name="merge",
    )(x2, o_sb, hg, gg, gate_bias, wsb, wlru, wout, g2, wrt, br)


def _experts_body(be_ref, r0_ref, t0_ref, nv_ref, nused_ref, cum_ref, off_ref,
                  xs_hbm, w1_ref, b1_ref, w2_ref, b2_ref, ys_hbm,
                  xbuf, ybuf, w1b, w2b, gsem, ssem, *, tiles, tile_rows):
    i = pl.program_id(0)
    rows = xbuf.shape[1] // SUBLANES
    ff = w2_ref.shape[1]
    n_used = nused_ref[0]
    slot = i % 2

    def for_runs(b, nv, fn):
        e = be_ref[b]
        r0 = r0_ref[b]
        end = r0 + nv

        def more(t):
            return jnp.logical_and(t < tiles, cum_ref[jnp.minimum(t, tiles) * N_EXPERTS + e] < end)

        def run(t):
            c0 = cum_ref[t * N_EXPERTS + e]
            c1 = cum_ref[(t + 1) * N_EXPERTS + e]
            lo = jnp.maximum(c0, r0)
            n_rows = jnp.minimum(c1, end) - lo

            @pl.when(n_rows > 0)
            def _():
                src = t * tile_rows + off_ref[t * N_EXPERTS + e] + (lo - c0)
                fn(pl.multiple_of(src * SUBLANES, SUBLANES),
                   pl.multiple_of((lo - r0) * SUBLANES, SUBLANES), n_rows * SUBLANES)

            return t + 1

        @pl.when(nv > 0)
        def _():
            lax.while_loop(more, run, t0_ref[b])

    def fetch(b, nv, s):
        def one(src, dst, size):
            pltpu.make_async_copy(xs_hbm.at[pl.ds(src, size)], xbuf.at[s, pl.ds(dst, size)],
                                  gsem.at[s]).start()
        for_runs(b, nv, one)

    def write_back(b, nv, s, sem):
        def one(src, dst, size):
            pltpu.make_async_copy(ybuf.at[s, pl.ds(dst, size)], ys_hbm.at[pl.ds(src, size)], sem).start()
        for_runs(b, nv, one)

    def wait_rows(nv, sem):
        @pl.when(nv > 0)
        def _():
            size = nv * SUBLANES
            pltpu.make_async_copy(xs_hbm.at[pl.ds(0, size)], xbuf.at[0, pl.ds(0, size)], sem).wait()

    prev = jnp.maximum(i - 1, 0)
    prev2 = jnp.maximum(i - 2, 0)
    nxt = jnp.minimum(i + 1, pl.num_programs(0) - 1)
    nv_prev = jnp.where(i >= 1, nv_ref[prev], 0)
    nv_prev2 = jnp.where(i >= 2, nv_ref[prev2], 0)

    @pl.when(i == 0)
    def _():
        xbuf[...] = jnp.zeros_like(xbuf)
        fetch(0, nv_ref[0], 0)

    @pl.when(jnp.logical_and(i < n_used, jnp.logical_or(i == 0, be_ref[i] != be_ref[prev])))
    def _():
        w1b[...] = w1_ref[0].astype(BF16)
        w2b[...] = w2_ref[0].astype(BF16)

    @pl.when(i <= n_used)
    def _():
        write_back(prev, nv_prev, 1 - slot, ssem.at[slot])

    @pl.when(i < n_used)
    def _():
        fetch(nxt, nv_ref[nxt], 1 - slot)
        wait_rows(nv_ref[i], gsem.at[slot])
        x = _from_row_tiles(xbuf.at[slot], rows).astype(BF16)
        hid = _dot(x, w1b[...]) + b1_ref[0]
        x_glu = jnp.minimum(hid[:, 0:ff], SWIGLU_LIMIT)
        x_lin = jnp.clip(hid[:, ff:2 * ff], -SWIGLU_LIMIT, SWIGLU_LIMIT)
        act = x_glu * _sigmoid(SWIGLU_ALPHA * x_glu) * (x_lin + 1.0)
        y = _dot(act.astype(BF16), w2b[...]) + b2_ref[0]
        wait_rows(nv_prev2, ssem.at[1 - slot])
        _to_row_tiles(ybuf.at[slot], y)

    @pl.when(i == n_used)
    def _():
        wait_rows(nv_prev2, ssem.at[1 - slot])
        wait_rows(nv_prev, ssem.at[slot])


def _experts(block_e, block_r0, block_t0, block_nv, n_used, cum, off, xs, w1, b1, w2, b2, *,
             rows, tiles, tile_rows):
    n_steps = block_e.shape[0]
    _, d, ff2 = w1.shape
    ff = ff2 // 2
    weights = lambda shape: pl.BlockSpec(shape, lambda i, be, *_: (be[i], 0, 0))
    grid_spec = pltpu.PrefetchScalarGridSpec(
        num_scalar_prefetch=7,
        grid=(n_steps,),
        in_specs=[pl.BlockSpec(memory_space=pl.ANY),
                  weights((1, d, ff2)), weights((1, 1, ff2)), weights((1, ff, d)), weights((1, 1, d))],
        out_specs=pl.BlockSpec(memory_space=pl.ANY),
        scratch_shapes=[pltpu.VMEM((2, rows * SUBLANES, LANES), F32),
                        pltpu.VMEM((2, rows * SUBLANES, LANES), F32),
                        pltpu.VMEM((d, ff2), BF16),
                        pltpu.VMEM((ff, d), BF16),
                        pltpu.SemaphoreType.DMA((2,)),
                        pltpu.SemaphoreType.DMA((2,))],
    )
    return pl.pallas_call(
        functools.partial(_experts_body, tiles=tiles, tile_rows=tile_rows),
        grid_spec=grid_spec,
        out_shape=jax.ShapeDtypeStruct(xs.shape, F32),
        compiler_params=_params("arbitrary"),
        name="experts",
    )(block_e, block_r0, block_t0, block_nv, n_used, cum, off, xs, w1,
      b1.reshape(N_EXPERTS, 1, ff2), w2, b2.reshape(N_EXPERTS, 1, d))


def _combine_body(pos_ref, gate_ref, h1_ref, g_ref, ys_ref, out_ref):
    tm = h1_ref.shape[0]
    p_iota = lax.broadcasted_iota(I32, (tm, TOP_K * tm), 1)
    pos = pos_ref[...]
    gate = gate_ref[...]
    g_mat = jnp.zeros((tm, TOP_K * tm), F32)
    for k in range(TOP_K):
        g_mat = jnp.where(p_iota == pos[:, k:k + 1], gate[:, k:k + 1], g_mat)
    ys = _from_row_tiles(ys_ref, TOP_K * tm).astype(BF16)
    out_ref[...] = _rms(h1_ref[...] + _dot(g_mat.astype(BF16), ys), g_ref[...])


def _combine(pos, gate, h1, g, ys, *, tm):
    n, d = h1.shape
    return pl.pallas_call(
        _combine_body,
        grid=(n // tm,),
        in_specs=[pl.BlockSpec((tm, TOP_K), lambda i: (i, 0)),
                  pl.BlockSpec((tm, TOP_K), lambda i: (i, 0)),
                  pl.BlockSpec((tm, d), lambda i: (i, 0)),
                  _resident((1, d)),
                  pl.BlockSpec((TOP_K * tm * SUBLANES, LANES), lambda i: (i, 0))],
        out_specs=pl.BlockSpec((tm, d), lambda i: (i, 0)),
        out_shape=jax.ShapeDtypeStruct((n, d), F32),
        compiler_params=_params("parallel"),
        name="combine",
    )(pos, gate, h1, g, ys)


def _block_tables(cnt, rows):
    cum = jnp.concatenate([jnp.zeros((1, N_EXPERTS), I32), jnp.cumsum(cnt, axis=0)])
    off = jnp.cumsum(cnt, axis=1) - cnt
    total = cum[-1]
    blocks = (total + rows - 1) // rows
    block_end = jnp.cumsum(blocks)
    return cum, off, total, blocks, block_end


def _layer(h2, bsz, seq, norm_mix_g, w_in, gate_bias, conv_w, conv_b, w_rg_a, b_rg_a, w_rg_x, b_rg_x,
           lru_lambda, w_sb_o, w_lru_o, w_out, norm_ffn_g, w_router, b_router, w1, b1, w2, b2, out_g):
    n, d = h2.shape
    tm = min(ROW_TILE, n)
    tiles = n // tm
    group = MXU_DIM // (d // LRU_BLOCKS)
    qkv, gg, hg = _inproj(h2.reshape(bsz, seq, d), norm_mix_g.reshape(1, d), w_in.astype(BF16),
                          conv_w, conv_b.reshape(1, d),
                          _block_diag(w_rg_a, group).astype(BF16), b_rg_a.reshape(1, d),
                          _block_diag(w_rg_x, group).astype(BF16), b_rg_x.reshape(1, d),
                          lru_lambda.reshape(1, d), tt=min(INPROJ_TIME_BLOCK, seq))
    o_sb = _attention(qkv, bsz=bsz, seq=seq, tq=min(ATTN_BLOCK, seq)).reshape(n, SB_WIDTH)
    h1, xs, pos, gate, cnt = _merge(
        h2, o_sb, hg.reshape(n, d), gg.reshape(n, 2 * d), gate_bias.reshape(1, 2 * d),
        w_sb_o.astype(BF16), w_lru_o.astype(BF16),
        w_out.astype(BF16), norm_ffn_g.reshape(1, d), w_router.T.astype(BF16),
        b_router.reshape(N_EXPERTS, 1), tm=tm)

    rows = MOE_ROWS
    n_steps = (n * TOP_K + N_EXPERTS * (rows - 1)) // rows + 1
    cnt = cnt.reshape(tiles, N_EXPERTS).astype(I32)
    cum, off, total, blocks, block_end = _block_tables(cnt, rows)
    step = jnp.arange(n_steps, dtype=I32)
    block_e = jnp.minimum(jnp.sum((step[:, None] >= block_end[None, :]).astype(I32), axis=1), N_EXPERTS - 1)
    block_r0 = (step - (block_end - blocks)[block_e]) * rows
    block_nv = jnp.clip(total[block_e] - block_r0, 0, rows)
    block_nv = jnp.where(step < block_end[-1], block_nv, 0)
    block_t0 = jnp.sum((cum[1:, :].T[block_e] <= block_r0[:, None]).astype(I32), axis=1)
    block_t0 = jnp.minimum(block_t0, tiles - 1)
    n_used = block_end[-1:].astype(I32)

    ys = _experts(block_e, block_r0, block_t0, block_nv, n_used, cum.reshape(-1), off.reshape(-1),
                  xs, w1, b1, w2, b2, rows=rows, tiles=tiles, tile_rows=TOP_K * tm)
    return _combine(pos.T, gate.T, h1, out_g.reshape(1, d), ys, tm=tm)


def kernel(x, norm_mix_g, w_in, gate_bias, conv_w, conv_b, w_rg_a, b_rg_a, w_rg_x, b_rg_x, lru_lambda,
           w_sb_o, w_lru_o, w_out, norm_ffn_g, w_router, b_router, w1, b1, w2, b2, norm_final_g):
    bsz, seq, d = x.shape
    depth = w_in.shape[0]
    assert depth == 1, "the final norm is fused into the last layer's combine"
    assert bsz % SUBLANES == 0 and d % MXU_DIM == 0
    h2 = x.reshape(bsz * seq, d)
    out = _layer(h2, bsz, seq, norm_mix_g[0], w_in[0], gate_bias[0], conv_w[0], conv_b[0], w_rg_a[0],
                 b_rg_a[0], w_rg_x[0], b_rg_x[0], lru_lambda[0], w_sb_o[0], w_lru_o[0], w_out[0],
                 norm_ffn_g[0], w_router[0], b_router[0], w1[0], b1[0], w2[0], b2[0], norm_final_g)
    return out.reshape(bsz, seq, d)
```

```python
import functools

import jax
import jax.numpy as jnp
from jax import lax
from jax.experimental import pallas as pl
from jax.experimental.pallas import tpu as pltpu

F32 = jnp.float32
BF16 = jnp.bfloat16
I32 = jnp.int32

SB_HEADS = 8
SB_HEAD_DIM = 64
SB_WIDTH = SB_HEADS * SB_HEAD_DIM
LRU_BLOCKS = 16
CONV_WIDTH = 4
RG_C = 8.0
N_EXPERTS = 32
TOP_K = 4
SWIGLU_ALPHA = 1.702
SWIGLU_LIMIT = 7.0
NORM_EPS = 1e-6

LANES = 128
SUBLANES = 8
MXU_DIM = 256
VMEM_LIMIT = 56 * 1024 * 1024

ROW_TILE = 512
ATTN_PAIRS = 4
ATTN_BLOCK = 256
ATTN_DEAD_CARRY = 104.0
INPROJ_TIME_BLOCK = 64
MOE_ROWS = 512


def _rms(x, g):
    return x * lax.rsqrt(jnp.mean(x * x, axis=-1, keepdims=True) + NORM_EPS) * g


def _sigmoid(x):
    return 1.0 / (1.0 + jnp.exp(-x))


def _sigmoid_t(x):
    return 0.5 * jnp.tanh(0.5 * x) + 0.5


def _softplus(x):
    return jnp.maximum(x, 0.0) + jnp.log(1.0 + jnp.exp(-jnp.abs(x)))


def _dot(a, b):
    return jnp.dot(a, b, preferred_element_type=F32)


def _params(*sem):
    return pltpu.CompilerParams(dimension_semantics=sem, vmem_limit_bytes=VMEM_LIMIT)


def _resident(shape):
    return pl.BlockSpec(shape, lambda *_: (0,) * len(shape), pipeline_mode=pl.Buffered(1))


def _inproj_body(x_ref, g_ref, w_ref, cw_ref, cb_ref, wa_ref, ba_ref, wx_ref, bx_ref, lam_ref,
                 qkv_ref, gg_ref, hg_ref, xpad, y_s, a_s, b_s, h_s, *, chunk):
    s = pl.program_id(1)
    nb, tt, d = x_ref.shape
    chunks = d // LANES
    n_rows = tt * nb
    halo = (CONV_WIDTH - 1) * nb
    u = _rms(x_ref[...].reshape(n_rows, d), g_ref[...]).astype(BF16)

    def project(col, width, store):
        for c in range(0, width, chunk):
            store(c, _dot(u, w_ref[:, col + c:col + c + chunk]))

    @pl.when(s == 0)
    def _():
        xpad[:, 0:halo, :] = jnp.zeros((chunks, halo, LANES), F32)
        h_s[...] = jnp.zeros_like(h_s)

    @pl.when(s > 0)
    def _():
        xpad[:, 0:halo, :] = xpad[:, n_rows:n_rows + halo, :]

    def store_x(c, val):
        for j in range(chunk // LANES):
            for b in range(nb):
                xpad[c // LANES + j, pl.ds(halo + b, tt, stride=nb), :] = (
                    val[b * tt:(b + 1) * tt, j * LANES:(j + 1) * LANES])

    def store_y(c, val):
        y_s[:, c:c + chunk] = val

    def store_to(ref):
        def store(c, val):
            ref[:, :, c:c + chunk] = val.astype(ref.dtype).reshape(nb, tt, chunk)
        return store

    x_col = qkv_ref.shape[2]
    project(x_col, d, store_x)
    todo = ([(x_col + d + c, store_y, c) for c in range(0, d, chunk)]
            + [(c, store_to(qkv_ref), c) for c in range(0, x_col, chunk)]
            + [(x_col + 2 * d + c, store_to(gg_ref), c) for c in range(0, 2 * d, chunk)])

    def project_next(count=1):
        for _ in range(count):
            if todo:
                col, store, c = todo.pop(0)
                store(c, _dot(u, w_ref[:, col:col + chunk]))

    def conv(j):
        lanes = slice(j * LANES, (j + 1) * LANES)
        xc = jnp.zeros((n_rows, LANES), F32) + cb_ref[:, lanes]
        for i in range(CONV_WIDTH):
            xc = xc + cw_ref[i:i + 1, lanes] * xpad[j, i * nb:i * nb + n_rows, :]
        return xc

    neg_sp_lam = -RG_C * _softplus(-lam_ref[...])
    per_group = MXU_DIM // LANES
    for g in range(chunks // per_group):
        project_next(2)
        sl = slice(g * MXU_DIM, (g + 1) * MXU_DIM)
        xc = jnp.concatenate([conv(g * per_group + j) for j in range(per_group)], axis=1)
        xb = xc.astype(BF16)
        r = _sigmoid_t(_dot(xb, wa_ref[g]) + ba_ref[:, sl])
        gate_i = _sigmoid_t(_dot(xb, wx_ref[g]) + bx_ref[:, sl])
        a = jnp.exp(r * neg_sp_lam[:, sl])
        var = (1.0 - a) * (1.0 + a)
        mult = jnp.where(var > 0.0, var * lax.rsqrt(var), 0.0)
        b = mult * (gate_i * xc)
        for j in range(per_group):
            a_s[g * per_group + j] = a[:, j * LANES:(j + 1) * LANES]
            b_s[g * per_group + j] = b[:, j * LANES:(j + 1) * LANES]

    def step(t, hs):
        rows = pl.ds(pl.multiple_of(t * nb, nb), nb)
        out = []
        for j in range(chunks):
            h = a_s[j, rows, :] * hs[j] + b_s[j, rows, :]
            b_s[j, rows, :] = h
            out.append(h)
        return tuple(out)

    hs = lax.fori_loop(0, tt, step, tuple(h_s[j] for j in range(chunks)))
    for j in range(chunks):
        h_s[j] = hs[j]

    for j in range(chunks):
        project_next(len(todo) if j == chunks - 1 else 1)
        for bi in range(nb):
            y = y_s[bi * tt:(bi + 1) * tt, j * LANES:(j + 1) * LANES]
            gelu = 0.5 * y * (1.0 + jnp.tanh(0.7978845608028654 * (y + 0.044715 * (y * y * y))))
            h = b_s[j, pl.ds(bi, tt, stride=nb), :]
            hg_ref[bi, :, j * LANES:(j + 1) * LANES] = (h * gelu).astype(hg_ref.dtype)


def _inproj(x3, g, w_in_bf, conv_w, conv_b, wa_bd, b_a, wx_bd, b_x, lam, *, tt):
    bsz, seq, d = x3.shape
    nb = SUBLANES
    chunks = d // LANES
    tile = lambda w: pl.BlockSpec((nb, tt, w), lambda b, s: (b, s, 0))
    return pl.pallas_call(
        functools.partial(_inproj_body, chunk=MXU_DIM),
        grid=(bsz // nb, seq // tt),
        in_specs=[tile(d), _resident((1, d)), _resident(w_in_bf.shape),
                  _resident(conv_w.shape), _resident((1, d)),
                  _resident(wa_bd.shape), _resident((1, d)),
                  _resident(wx_bd.shape), _resident((1, d)),
                  _resident((1, d))],
        out_specs=[tile(3 * SB_WIDTH), tile(2 * d), tile(d)],
        out_shape=[jax.ShapeDtypeStruct((bsz, seq, 3 * SB_WIDTH), BF16),
                   jax.ShapeDtypeStruct((bsz, seq, 2 * d), F32),
                   jax.ShapeDtypeStruct((bsz, seq, d), BF16)],
        scratch_shapes=[pltpu.VMEM((chunks, (tt + CONV_WIDTH - 1) * nb, LANES), F32),
                        pltpu.VMEM((nb * tt, d), F32),
                        pltpu.VMEM((chunks, tt * nb, LANES), F32),
                        pltpu.VMEM((chunks, tt * nb, LANES), F32),
                        pltpu.VMEM((chunks, nb, LANES), F32)],
        compiler_params=_params("parallel", "arbitrary"),
        name="inproj",
    )(x3, g, w_in_bf, conv_w, conv_b, wa_bd, b_a, wx_bd, b_x, lam)


def _attn_body(q_ref, k_ref, v_ref, o_ref, o_acc, c_acc, *, tq, npair):
    qi = pl.program_id(2)
    lane = lax.broadcasted_iota(I32, (1, LANES), 1)
    first = lane < SB_HEAD_DIM
    q_pairs = []
    for p in range(npair):
        q = q_ref[0, :, p * LANES:(p + 1) * LANES] * (SB_HEAD_DIM ** -0.5)
        zq = jnp.zeros_like(q)
        q_pairs.append(jnp.concatenate([jnp.where(first, q, zq), jnp.where(first, zq, q)], axis=0))
    row = lax.broadcasted_iota(I32, (2 * tq, tq), 0)
    row = jnp.where(row >= tq, row - tq, row)
    col = lax.broadcasted_iota(I32, (2 * tq, tq), 1)
    causal = col < row
    tri2 = jnp.where(row > col, 1.0, 0.0).astype(BF16)

    def block(j, diag):
        start = pl.multiple_of(j * tq, tq)
        zs, sps, sufs = [], [], []
        for p in range(npair):
            kj = k_ref[0, pl.ds(start, tq), p * LANES:(p + 1) * LANES]
            zs.append(lax.dot_general(q_pairs[p], kj, (((1,), (1,)), ((), ())), preferred_element_type=F32))
        for p in range(npair):
            sp = _softplus(zs[p])
            if diag:
                sp = jnp.where(causal, sp, 0.0)
            sps.append(sp)
            sp_hi = sp.astype(BF16)
            sp_lo = (sp - sp_hi.astype(F32)).astype(BF16)
            sufs.append(_dot(jnp.concatenate([sp_hi, sp_lo], axis=1), tri2))
        low = None
        for p in range(npair):
            vj = v_ref[0, pl.ds(start, tq), p * LANES:(p + 1) * LANES]
            zv = jnp.zeros_like(vj)
            v2 = jnp.concatenate([jnp.where(first, vj, zv), jnp.where(first, zv, vj)], axis=0)
            arg = zs[p] - sps[p] - sufs[p]
            if not diag:
                arg = arg - c_acc[p]
            att = jnp.exp(arg)
            if diag:
                att = jnp.where(causal, att, 0.0)
            att = att.astype(BF16)
            pv = _dot(jnp.concatenate([att[0:tq], att[tq:2 * tq]], axis=1), v2)
            if not diag:
                pv = pv + o_acc[:, p * LANES:(p + 1) * LANES]
            o_acc[:, p * LANES:(p + 1) * LANES] = pv
            c = jnp.sum(sps[p], axis=-1, keepdims=True)
            if not diag:
                c = c + c_acc[p]
            c_acc[p] = c
            m = jnp.min(c)
            low = m if low is None else jnp.minimum(low, m)
        return low

    def more(state):
        j, low = state
        return jnp.logical_and(j >= 0, low < ATTN_DEAD_CARRY)

    def step(state):
        j, _ = state
        return j - 1, block(j, False)

    lax.while_loop(more, step, (qi - 1, block(qi, True)))
    o_ref[0] = o_acc[...].astype(o_ref.dtype)


def _attention(qkv, *, bsz, seq, tq, npair=ATTN_PAIRS):
    qkv3 = qkv.reshape(bsz, seq, 3 * SB_WIDTH)
    width = npair * LANES
    groups = SB_WIDTH // width
    return pl.pallas_call(
        functools.partial(_attn_body, tq=tq, npair=npair),
        grid=(bsz, groups, seq // tq),
        in_specs=[pl.BlockSpec((1, tq, width), lambda b, p, i: (b, i, p)),
                  pl.BlockSpec((1, seq, width), lambda b, p, i: (b, 0, groups + p)),
                  pl.BlockSpec((1, seq, width), lambda b, p, i: (b, 0, 2 * groups + p))],
        out_specs=pl.BlockSpec((1, tq, width), lambda b, p, i: (b, i, p)),
        out_shape=jax.ShapeDtypeStruct((bsz, seq, SB_WIDTH), BF16),
        scratch_shapes=[pltpu.VMEM((tq, width), F32), pltpu.VMEM((npair, 2 * tq, 1), F32)],
        compiler_params=_params("parallel", "parallel", "arbitrary"),
        name="attn",
    )(qkv3, qkv3, qkv3)


def _block_diag(w, group):
    n, k, _ = w.shape
    w = w.reshape(n // group, group, k, k)
    eye = jnp.eye(group, dtype=w.dtype)
    return jnp.einsum('gakl,ab->gakbl', w, eye).reshape(n // group, group * k, group * k)


ROW_WORDS = 4


def _sorted_rows(tm):
    cap = -(-(TOP_K * tm + N_EXPERTS) // LANES) * LANES
    assert (cap // TOP_K) % 16 == 0
    return cap


def _to_row_tiles(ref, value, first_row=0):
    rows, d = value.shape
    bits = lax.bitcast_convert_type(value, I32)
    for s in range(ROW_WORDS):
        low = lax.shift_right_logical(bits[:, s * LANES:(s + 1) * LANES], 16)
        high = bits[:, d // 2 + s * LANES:d // 2 + (s + 1) * LANES]
        ref[pl.ds(first_row * ROW_WORDS + s, rows, stride=ROW_WORDS), :] = high | low


def _from_row_tiles(ref, rows):
    words = [ref[pl.ds(s, rows, stride=ROW_WORDS), :] for s in range(ROW_WORDS)]
    low = [lax.bitcast_convert_type(lax.shift_left(w, 16), F32).astype(BF16) for w in words]
    high = [lax.bitcast_convert_type(w & jnp.int32(-65536), F32).astype(BF16) for w in words]
    return jnp.concatenate(low + high, axis=1)


def _merge_body(x_ref, osb_ref, hg_ref, gg_ref, gbias_ref, wsb_ref, wlru_ref, wout_ref, g2_ref,
                wrt_ref, br_ref, h1_ref, xs_ref, pos_ref, gate_ref, cnt_ref):
    tm, d = x_ref.shape

    y_sb = _dot(osb_ref[...], wsb_ref[...])
    y_lru = _dot(hg_ref[...], wlru_ref[...])
    g_a = _sigmoid(gg_ref[:, 0:d] + gbias_ref[:, 0:d])
    g_b = _sigmoid(gg_ref[:, d:2 * d] + gbias_ref[:, d:2 * d])
    merged = (g_a * y_sb + g_b * y_lru).astype(BF16)
    h1 = x_ref[...] + _dot(merged, wout_ref[...])
    h1_ref[...] = h1
    u2 = _rms(h1, g2_ref[...]).astype(BF16)
    logits = lax.dot_general(wrt_ref[...], u2, (((1,), (1,)), ((), ())),
                             preferred_element_type=F32) + br_ref[...]

    e_iota = lax.broadcasted_iota(I32, (N_EXPERTS, tm), 0)
    k_iota = lax.broadcasted_iota(I32, (TOP_K, tm), 0)
    vals = logits
    chosen = jnp.zeros((N_EXPERTS, tm), F32)
    top_v, hots = [], []
    for _ in range(TOP_K):
        m = jnp.max(vals, axis=0, keepdims=True)
        sel = jnp.min(jnp.where(vals == m, e_iota, N_EXPERTS), axis=0, keepdims=True)
        hot = e_iota == sel
        top_v.append(m)
        hots.append(hot)
        chosen = jnp.where(hot, 1.0, chosen)
        vals = jnp.where(hot, -jnp.inf, vals)
    exps = [jnp.exp(v - top_v[0]) for v in top_v]
    denom = exps[0] + exps[1] + exps[2] + exps[3]

    chosen_b = chosen.astype(BF16)
    t_row = lax.broadcasted_iota(I32, (tm, tm), 0)
    t_col = lax.broadcasted_iota(I32, (tm, tm), 1)
    earlier = jnp.where(t_row < t_col, 1.0, 0.0).astype(BF16)
    same_before = _dot(chosen_b, earlier)
    e_row = lax.broadcasted_iota(I32, (N_EXPERTS, N_EXPERTS), 0)
    e_col = lax.broadcasted_iota(I32, (N_EXPERTS, N_EXPERTS), 1)
    smaller = jnp.where(e_col < e_row, 1.0, 0.0).astype(BF16)
    cnt = jnp.sum(chosen, axis=1, keepdims=True)
    odd = cnt - 2.0 * jnp.floor(0.5 * cnt)
    cnt_ref[...] = cnt + odd
    odd_smaller = _dot(smaller, jnp.broadcast_to(odd, (N_EXPERTS, LANES)).astype(BF16))[:, 0:1]
    first = jnp.sum(_dot(smaller, chosen_b), axis=1, keepdims=True) + odd_smaller

    where_to = first + same_before
    pos = jnp.zeros((TOP_K, tm), I32)
    gate = jnp.zeros((TOP_K, tm), F32)
    pos_k = []
    for k in range(TOP_K):
        pk = jnp.sum(jnp.where(hots[k], where_to, 0.0), axis=0, keepdims=True).astype(I32)
        pos_k.append(pk)
        pos = jnp.where(k_iota == k, pk, pos)
        gate = jnp.where(k_iota == k, exps[k] / denom, gate)
    pos_ref[...] = pos
    gate_ref[...] = gate

    cap = xs_ref.shape[0] // ROW_WORDS
    piece = cap // TOP_K
    for c in range(TOP_K):
        p_iota = lax.broadcasted_iota(I32, (piece, tm), 0) + c * piece
        hit = p_iota == pos_k[0]
        for k in range(1, TOP_K):
            hit = jnp.logical_or(hit, p_iota == pos_k[k])
        perm = jnp.where(hit, 1.0, 0.0).astype(BF16)
        _to_row_tiles(xs_ref, _dot(perm, u2), first_row=c * piece)


def _merge(x2, o_sb, hg, gg, gate_bias, wsb, wlru, wout, g2, wrt, br, *, tm):
    n, d = x2.shape
    tiles = n // tm
    assert d == 2 * ROW_WORDS * LANES, "row tile form packs 1024 columns into 4 x 128 words"
    cap = _sorted_rows(tm)
    row = lambda w: pl.BlockSpec((tm, w), lambda i: (i, 0))
    return pl.pallas_call(
        _merge_body,
        grid=(tiles,),
        in_specs=[row(d), row(SB_WIDTH), row(d), row(2 * d), _resident((1, 2 * d)),
                  _resident(wsb.shape), _resident(wlru.shape), _resident(wout.shape),
                  _resident((1, d)), _resident(wrt.shape), _resident((N_EXPERTS, 1))],
        out_specs=[row(d),
                   pl.BlockSpec((cap * ROW_WORDS, LANES), lambda i: (i, 0)),
                   pl.BlockSpec((TOP_K, tm), lambda i: (0, i)),
                   pl.BlockSpec((TOP_K, tm), lambda i: (0, i)),
                   pl.BlockSpec((N_EXPERTS, 1), lambda i: (i, 0))],
        out_shape=[jax.ShapeDtypeStruct((n, d), F32),
                   jax.ShapeDtypeStruct((tiles * cap * ROW_WORDS, LANES), I32),
                   jax.ShapeDtypeStruct((TOP_K, n), I32),
                   jax.ShapeDtypeStruct((TOP_K, n), F32),
                   jax.ShapeDtypeStruct((tiles * N_EXPERTS, 1), F32)],
        compiler_params=_params("parallel"),
        name="merge",
    )(x2, o_sb, hg, gg, gate_bias, wsb, wlru, wout, g2, wrt, br)


def _experts_body(be_ref, r0_ref, t0_ref, nv_ref, nused_ref, cum_ref, off_ref,
                  xs_hbm, w1_ref, b1_ref, w2_ref, b2_ref, ys_hbm,
                  xbuf, ybuf, w1b, w2b, gsem, ssem, *, tiles, tile_rows):
    i = pl.program_id(0)
    rows = xbuf.shape[1] // ROW_WORDS
    ff = w2_ref.shape[1]
    n_used = nused_ref[0]
    slot = i % 2

    def for_runs(b, nv, fn):
        e = be_ref[b]
        r0 = r0_ref[b]
        end = r0 + nv

        def more(t):
            return jnp.logical_and(t < tiles, cum_ref[jnp.minimum(t, tiles) * N_EXPERTS + e] < end)

        def run(t):
            c0 = cum_ref[t * N_EXPERTS + e]
            c1 = cum_ref[(t + 1) * N_EXPERTS + e]
            lo = jnp.maximum(c0, r0)
            n_rows = jnp.minimum(c1, end) - lo

            @pl.when(n_rows > 0)
            def _():
                src = t * tile_rows + off_ref[t * N_EXPERTS + e] + (lo - c0)
                fn(pl.multiple_of(src * ROW_WORDS, SUBLANES),
                   pl.multiple_of((lo - r0) * ROW_WORDS, SUBLANES), n_rows * ROW_WORDS)

            return t + 1

        @pl.when(nv > 0)
        def _():
            lax.while_loop(more, run, t0_ref[b])

    def fetch(b, nv, s):
        def one(src, dst, size):
            pltpu.make_async_copy(xs_hbm.at[pl.ds(src, size)], xbuf.at[s, pl.ds(dst, size)],
                                  gsem.at[s]).start()
        for_runs(b, nv, one)

    def write_back(b, nv, s, sem):
        def one(src, dst, size):
            pltpu.make_async_copy(ybuf.at[s, pl.ds(dst, size)], ys_hbm.at[pl.ds(src, size)], sem).start()
        for_runs(b, nv, one)

    def wait_rows(nv, sem):
        @pl.when(nv > 0)
        def _():
            size = nv * ROW_WORDS
            pltpu.make_async_copy(xs_hbm.at[pl.ds(0, size)], xbuf.at[0, pl.ds(0, size)], sem).wait()

    prev = jnp.maximum(i - 1, 0)
    prev2 = jnp.maximum(i - 2, 0)
    nxt = jnp.minimum(i + 1, pl.num_programs(0) - 1)
    nv_prev = jnp.where(i >= 1, nv_ref[prev], 0)
    nv_prev2 = jnp.where(i >= 2, nv_ref[prev2], 0)

    @pl.when(i == 0)
    def _():
        xbuf[...] = jnp.zeros_like(xbuf)
        fetch(0, nv_ref[0], 0)

    @pl.when(jnp.logical_and(i < n_used, jnp.logical_or(i == 0, be_ref[i] != be_ref[prev])))
    def _():
        w1b[...] = w1_ref[0].astype(BF16)
        w2b[...] = w2_ref[0].astype(BF16)

    @pl.when(i <= n_used)
    def _():
        write_back(prev, nv_prev, 1 - slot, ssem.at[slot])

    @pl.when(i < n_used)
    def _():
        fetch(nxt, nv_ref[nxt], 1 - slot)
        wait_rows(nv_ref[i], gsem.at[slot])
        x = _from_row_tiles(xbuf.at[slot], rows)
        hid = _dot(x, w1b[...]) + b1_ref[0]
        x_glu = jnp.minimum(hid[:, 0:ff], SWIGLU_LIMIT)
        x_lin = jnp.clip(hid[:, ff:2 * ff], -SWIGLU_LIMIT, SWIGLU_LIMIT)
        act = x_glu * _sigmoid(SWIGLU_ALPHA * x_glu) * (x_lin + 1.0)
        y = _dot(act.astype(BF16), w2b[...]) + b2_ref[0]
        wait_rows(nv_prev2, ssem.at[1 - slot])
        _to_row_tiles(ybuf.at[slot], y.astype(BF16).astype(F32))

    @pl.when(i == n_used)
    def _():
        wait_rows(nv_prev2, ssem.at[1 - slot])
        wait_rows(nv_prev, ssem.at[slot])


def _experts(block_e, block_r0, block_t0, block_nv, n_used, cum, off, xs, w1, b1, w2, b2, *,
             rows, tiles, tile_rows):
    n_steps = block_e.shape[0]
    _, d, ff2 = w1.shape
    ff = ff2 // 2
    weights = lambda shape: pl.BlockSpec(shape, lambda i, be, *_: (be[i], 0, 0))
    grid_spec = pltpu.PrefetchScalarGridSpec(
        num_scalar_prefetch=7,
        grid=(n_steps,),
        in_specs=[pl.BlockSpec(memory_space=pl.ANY),
                  weights((1, d, ff2)), weights((1, 1, ff2)), weights((1, ff, d)), weights((1, 1, d))],
        out_specs=pl.BlockSpec(memory_space=pl.ANY),
        scratch_shapes=[pltpu.VMEM((2, rows * ROW_WORDS, LANES), I32),
                        pltpu.VMEM((2, rows * ROW_WORDS, LANES), I32),
                        pltpu.VMEM((d, ff2), BF16),
                        pltpu.VMEM((ff, d), BF16),
                        pltpu.SemaphoreType.DMA((2,)),
                        pltpu.SemaphoreType.DMA((2,))],
    )
    return pl.pallas_call(
        functools.partial(_experts_body, tiles=tiles, tile_rows=tile_rows),
        grid_spec=grid_spec,
        out_shape=jax.ShapeDtypeStruct(xs.shape, I32),
        input_output_aliases={7: 0},
        compiler_params=_params("arbitrary"),
        name="experts",
    )(block_e, block_r0, block_t0, block_nv, n_used, cum, off, xs, w1,
      b1.reshape(N_EXPERTS, 1, ff2), w2, b2.reshape(N_EXPERTS, 1, d))


def _combine_body(pos_ref, gate_ref, h1_ref, g_ref, ys_ref, out_ref):
    tm = h1_ref.shape[0]
    cap = ys_ref.shape[0] // ROW_WORDS
    p_iota = lax.broadcasted_iota(I32, (tm, cap), 1)
    pos = pos_ref[...]
    gate = gate_ref[...]
    g_mat = jnp.zeros((tm, cap), F32)
    for k in range(TOP_K):
        g_mat = jnp.where(p_iota == pos[:, k:k + 1], gate[:, k:k + 1], g_mat)
    ys = _from_row_tiles(ys_ref, cap)
    out_ref[...] = _rms(h1_ref[...] + _dot(g_mat.astype(BF16), ys), g_ref[...])


def _combine(pos, gate, h1, g, ys, *, tm):
    n, d = h1.shape
    return pl.pallas_call(
        _combine_body,
        grid=(n // tm,),
        in_specs=[pl.BlockSpec((tm, TOP_K), lambda i: (i, 0)),
                  pl.BlockSpec((tm, TOP_K), lambda i: (i, 0)),
                  pl.BlockSpec((tm, d), lambda i: (i, 0)),
                  _resident((1, d)),
                  pl.BlockSpec((_sorted_rows(tm) * ROW_WORDS, LANES), lambda i: (i, 0))],
        out_specs=pl.BlockSpec((tm, d), lambda i: (i, 0)),
        out_shape=jax.ShapeDtypeStruct((n, d), F32),
        compiler_params=_params("parallel"),
        name="combine",
    )(pos, gate, h1, g, ys)


def _block_tables(cnt, rows):
    cum = jnp.concatenate([jnp.zeros((1, N_EXPERTS), I32), jnp.cumsum(cnt, axis=0)])
    off = jnp.cumsum(cnt, axis=1) - cnt
    total = cum[-1]
    blocks = (total + rows - 1) // rows
    block_end = jnp.cumsum(blocks)
    return cum, off, total, blocks, block_end


def _layer(h2, bsz, seq, norm_mix_g, w_in, gate_bias, conv_w, conv_b, w_rg_a, b_rg_a, w_rg_x, b_rg_x,
           lru_lambda, w_sb_o, w_lru_o, w_out, norm_ffn_g, w_router, b_router, w1, b1, w2, b2, out_g):
    n, d = h2.shape
    tm = min(ROW_TILE, n)
    tiles = n // tm
    group = MXU_DIM // (d // LRU_BLOCKS)
    qkv, gg, hg = _inproj(h2.reshape(bsz, seq, d), norm_mix_g.reshape(1, d), w_in.astype(BF16),
                          conv_w, conv_b.reshape(1, d),
                          _block_diag(w_rg_a, group).astype(BF16), b_rg_a.reshape(1, d),
                          _block_diag(w_rg_x, group).astype(BF16), b_rg_x.reshape(1, d),
                          lru_lambda.reshape(1, d), tt=min(INPROJ_TIME_BLOCK, seq))
    o_sb = _attention(qkv, bsz=bsz, seq=seq, tq=min(ATTN_BLOCK, seq)).reshape(n, SB_WIDTH)
    h1, xs, pos, gate, cnt = _merge(
        h2, o_sb, hg.reshape(n, d), gg.reshape(n, 2 * d), gate_bias.reshape(1, 2 * d),
        w_sb_o.astype(BF16), w_lru_o.astype(BF16),
        w_out.astype(BF16), norm_ffn_g.reshape(1, d), w_router.T.astype(BF16),
        b_router.reshape(N_EXPERTS, 1), tm=tm)

    rows = MOE_ROWS
    n_steps = (n * TOP_K + tiles * N_EXPERTS + N_EXPERTS * (rows - 1)) // rows + 1
    cnt = cnt.reshape(tiles, N_EXPERTS).astype(I32)
    cum, off, total, blocks, block_end = _block_tables(cnt, rows)
    step = jnp.arange(n_steps, dtype=I32)
    block_e = jnp.minimum(jnp.sum((step[:, None] >= block_end[None, :]).astype(I32), axis=1), N_EXPERTS - 1)
    block_r0 = (step - (block_end - blocks)[block_e]) * rows
    block_nv = jnp.clip(total[block_e] - block_r0, 0, rows)
    block_nv = jnp.where(step < block_end[-1], block_nv, 0)
    block_t0 = jnp.sum((cum[1:, :].T[block_e] <= block_r0[:, None]).astype(I32), axis=1)
    block_t0 = jnp.minimum(block_t0, tiles - 1)
    n_used = block_end[-1:].astype(I32)

    ys = _experts(block_e, block_r0, block_t0, block_nv, n_used, cum.reshape(-1), off.reshape(-1),
                  xs, w1, b1, w2, b2, rows=rows, tiles=tiles, tile_rows=_sorted_rows(tm))
    return _combine(pos.T, gate.T, h1, out_g.reshape(1, d), ys, tm=tm)


def kernel(x, norm_mix_g, w_in, gate_bias, conv_w, conv_b, w_rg_a, b_rg_a, w_rg_x, b_rg_x, lru_lambda,
           w_sb_o, w_lru_o, w_out, norm_ffn_g, w_router, b_router, w1, b1, w2, b2, norm_final_g):
    bsz, seq, d = x.shape
    depth = w_in.shape[0]
    assert depth == 1, "the final norm is fused into the last layer's combine"
    assert bsz % SUBLANES == 0 and d % MXU_DIM == 0
    h2 = x.reshape(bsz * seq, d)
    out = _layer(h2, bsz, seq, norm_mix_g[0], w_in[0], gate_bias[0], conv_w[0], conv_b[0], w_rg_a[0],
                 b_rg_a[0], w_rg_x[0], b_rg_x[0], lru_lambda[0], w_sb_o[0], w_lru_o[0], w_out[0],
                 norm_ffn_g[0], w_router[0], b_router[0], w1[0], b1[0], w2[0], b2[0], norm_final_g)
    return out.reshape(bsz, seq, d)
```

```python
import functools

import jax
import jax.numpy as jnp
from jax import lax
from jax.experimental import pallas as pl
from jax.experimental.pallas import tpu as pltpu

F32 = jnp.float32
BF16 = jnp.bfloat16
I32 = jnp.int32

SB_HEADS = 8
SB_HEAD_DIM = 64
SB_WIDTH = SB_HEADS * SB_HEAD_DIM
LRU_BLOCKS = 16
CONV_WIDTH = 4
RG_C = 8.0
N_EXPERTS = 32
TOP_K = 4
SWIGLU_ALPHA = 1.702
SWIGLU_LIMIT = 7.0
NORM_EPS = 1e-6

LANES = 128
SUBLANES = 8
MXU_DIM = 256
VMEM_LIMIT = 56 * 1024 * 1024

ROW_TILE = 512
ATTN_PAIRS = 4
ATTN_BLOCK = 256
ATTN_DEAD_CARRY = 104.0
INPROJ_TIME_BLOCK = 64
MOE_ROWS = 512


def _rms(x, g):
    return x * lax.rsqrt(jnp.mean(x * x, axis=-1, keepdims=True) + NORM_EPS) * g


def _sigmoid(x):
    return 1.0 / (1.0 + jnp.exp(-x))


def _sigmoid_t(x):
    return 0.5 * jnp.tanh(0.5 * x) + 0.5


def _softplus(x):
    return jnp.maximum(x, 0.0) + jnp.log(1.0 + jnp.exp(-jnp.abs(x)))


def _dot(a, b):
    return jnp.dot(a, b, preferred_element_type=F32)


def _params(*sem):
    return pltpu.CompilerParams(dimension_semantics=sem, vmem_limit_bytes=VMEM_LIMIT)


def _resident(shape):
    return pl.BlockSpec(shape, lambda *_: (0,) * len(shape), pipeline_mode=pl.Buffered(1))


def _inproj_body(x_ref, g_ref, w_ref, cw_ref, cb_ref, wa_ref, ba_ref, wx_ref, bx_ref, lam_ref,
                 qkv_ref, gg_ref, hg_ref, xpad, y_s, a_s, b_s, h_s, *, chunk):
    s = pl.program_id(1)
    nb, tt, d = x_ref.shape
    chunks = d // LANES
    n_rows = tt * nb
    halo = (CONV_WIDTH - 1) * nb
    u = _rms(x_ref[...].reshape(n_rows, d), g_ref[...]).astype(BF16)

    def project(col, width, store):
        for c in range(0, width, chunk):
            store(c, _dot(u, w_ref[:, col + c:col + c + chunk]))

    @pl.when(s == 0)
    def _():
        xpad[:, 0:halo, :] = jnp.zeros((chunks, halo, LANES), F32)
        h_s[...] = jnp.zeros_like(h_s)

    @pl.when(s > 0)
    def _():
        xpad[:, 0:halo, :] = xpad[:, n_rows:n_rows + halo, :]

    def store_x(c, val):
        for j in range(chunk // LANES):
            for b in range(nb):
                xpad[c // LANES + j, pl.ds(halo + b, tt, stride=nb), :] = (
                    val[b * tt:(b + 1) * tt, j * LANES:(j + 1) * LANES])

    def store_y(c, val):
        y_s[:, c:c + chunk] = val

    def store_to(ref):
        def store(c, val):
            ref[:, :, c:c + chunk] = val.astype(ref.dtype).reshape(nb, tt, chunk)
        return store

    x_col = qkv_ref.shape[2]
    project(x_col, d, store_x)
    todo = ([(x_col + d + c, store_y, c) for c in range(0, d, chunk)]
            + [(c, store_to(qkv_ref), c) for c in range(0, x_col, chunk)]
            + [(x_col + 2 * d + c, store_to(gg_ref), c) for c in range(0, 2 * d, chunk)])

    def project_next(count=1):
        for _ in range(count):
            if todo:
                col, store, c = todo.pop(0)
                store(c, _dot(u, w_ref[:, col:col + chunk]))

    def conv(j):
        lanes = slice(j * LANES, (j + 1) * LANES)
        xc = jnp.zeros((n_rows, LANES), F32) + cb_ref[:, lanes]
        for i in range(CONV_WIDTH):
            xc = xc + cw_ref[i:i + 1, lanes] * xpad[j, i * nb:i * nb + n_rows, :]
        return xc

    neg_sp_lam = -RG_C * _softplus(-lam_ref[...])
    per_group = MXU_DIM // LANES
    for g in range(chunks // per_group):
        project_next(2)
        sl = slice(g * MXU_DIM, (g + 1) * MXU_DIM)
        xc = jnp.concatenate([conv(g * per_group + j) for j in range(per_group)], axis=1)
        xb = xc.astype(BF16)
        r = _sigmoid_t(_dot(xb, wa_ref[g]) + ba_ref[:, sl])
        gate_i = _sigmoid_t(_dot(xb, wx_ref[g]) + bx_ref[:, sl])
        a = jnp.exp(r * neg_sp_lam[:, sl])
        var = (1.0 - a) * (1.0 + a)
        mult = jnp.where(var > 0.0, var * lax.rsqrt(var), 0.0)
        b = mult * (gate_i * xc)
        for j in range(per_group):
            a_s[g * per_group + j] = a[:, j * LANES:(j + 1) * LANES]
            b_s[g * per_group + j] = b[:, j * LANES:(j + 1) * LANES]

    def step(t, hs):
        rows = pl.ds(pl.multiple_of(t * nb, nb), nb)
        out = []
        for j in range(chunks):
            h = a_s[j, rows, :] * hs[j] + b_s[j, rows, :]
            b_s[j, rows, :] = h
            out.append(h)
        return tuple(out)

    hs = lax.fori_loop(0, tt, step, tuple(h_s[j] for j in range(chunks)))
    for j in range(chunks):
        h_s[j] = hs[j]

    for j in range(chunks):
        project_next(len(todo) if j == chunks - 1 else 1)
        for bi in range(nb):
            y = y_s[bi * tt:(bi + 1) * tt, j * LANES:(j + 1) * LANES]
            gelu = 0.5 * y * (1.0 + jnp.tanh(0.7978845608028654 * (y + 0.044715 * (y * y * y))))
            h = b_s[j, pl.ds(bi, tt, stride=nb), :]
            hg_ref[bi, :, j * LANES:(j + 1) * LANES] = (h * gelu).astype(hg_ref.dtype)


def _inproj(x3, g, w_in_bf, conv_w, conv_b, wa_bd, b_a, wx_bd, b_x, lam, *, tt):
    bsz, seq, d = x3.shape
    nb = SUBLANES
    chunks = d // LANES
    tile = lambda w: pl.BlockSpec((nb, tt, w), lambda b, s: (b, s, 0))
    return pl.pallas_call(
        functools.partial(_inproj_body, chunk=MXU_DIM),
        grid=(bsz // nb, seq // tt),
        in_specs=[tile(d), _resident((1, d)), _resident(w_in_bf.shape),
                  _resident(conv_w.shape), _resident((1, d)),
                  _resident(wa_bd.shape), _resident((1, d)),
                  _resident(wx_bd.shape), _resident((1, d)),
                  _resident((1, d))],
        out_specs=[tile(3 * SB_WIDTH), tile(2 * d), tile(d)],
        out_shape=[jax.ShapeDtypeStruct((bsz, seq, 3 * SB_WIDTH), BF16),
                   jax.ShapeDtypeStruct((bsz, seq, 2 * d), F32),
                   jax.ShapeDtypeStruct((bsz, seq, d), BF16)],
        scratch_shapes=[pltpu.VMEM((chunks, (tt + CONV_WIDTH - 1) * nb, LANES), F32),
                        pltpu.VMEM((nb * tt, d), F32),
                        pltpu.VMEM((chunks, tt * nb, LANES), F32),
                        pltpu.VMEM((chunks, tt * nb, LANES), F32),
                        pltpu.VMEM((chunks, nb, LANES), F32)],
        compiler_params=_params("parallel", "arbitrary"),
        name="inproj",
    )(x3, g, w_in_bf, conv_w, conv_b, wa_bd, b_a, wx_bd, b_x, lam)


def _attn_body(q_ref, k_ref, v_ref, o_ref, o_acc, c_acc, *, tq, npair):
    qi = pl.program_id(2)
    lane = lax.broadcasted_iota(I32, (1, LANES), 1)
    first = lane < SB_HEAD_DIM
    q_pairs = []
    for p in range(npair):
        q = q_ref[0, :, p * LANES:(p + 1) * LANES] * (SB_HEAD_DIM ** -0.5)
        zq = jnp.zeros_like(q)
        q_pairs.append(jnp.concatenate([jnp.where(first, q, zq), jnp.where(first, zq, q)], axis=0))
    row = lax.broadcasted_iota(I32, (2 * tq, tq), 0)
    row = jnp.where(row >= tq, row - tq, row)
    col = lax.broadcasted_iota(I32, (2 * tq, tq), 1)
    causal = col < row
    tri2 = jnp.where(row > col, 1.0, 0.0).astype(BF16)

    def block(j, diag):
        start = pl.multiple_of(j * tq, tq)
        zs, sps, sufs = [], [], []
        for p in range(npair):
            kj = k_ref[0, pl.ds(start, tq), p * LANES:(p + 1) * LANES]
            zs.append(lax.dot_general(q_pairs[p], kj, (((1,), (1,)), ((), ())), preferred_element_type=F32))
        for p in range(npair):
            sp = _softplus(zs[p])
            if diag:
                sp = jnp.where(causal, sp, 0.0)
            sps.append(sp)
            sp_hi = sp.astype(BF16)
            sp_lo = (sp - sp_hi.astype(F32)).astype(BF16)
            sufs.append(_dot(jnp.concatenate([sp_hi, sp_lo], axis=1), tri2))
        low = None
        for p in range(npair):
            vj = v_ref[0, pl.ds(start, tq), p * LANES:(p + 1) * LANES]
            zv = jnp.zeros_like(vj)
            v2 = jnp.concatenate([jnp.where(first, vj, zv), jnp.where(first, zv, vj)], axis=0)
            arg = zs[p] - sps[p] - sufs[p]
            if not diag:
                arg = arg - c_acc[p]
            att = jnp.exp(arg)
            if diag:
                att = jnp.where(causal, att, 0.0)
            att = att.astype(BF16)
            pv = _dot(jnp.concatenate([att[0:tq], att[tq:2 * tq]], axis=1), v2)
            if not diag:
                pv = pv + o_acc[:, p * LANES:(p + 1) * LANES]
            o_acc[:, p * LANES:(p + 1) * LANES] = pv
            c = jnp.sum(sps[p], axis=-1, keepdims=True)
            if not diag:
                c = c + c_acc[p]
            c_acc[p] = c
            m = jnp.min(c)
            low = m if low is None else jnp.minimum(low, m)
        return low

    def more(state):
        j, low = state
        return jnp.logical_and(j >= 0, low < ATTN_DEAD_CARRY)

    def step(state):
        j, _ = state
        return j - 1, block(j, False)

    lax.while_loop(more, step, (qi - 1, block(qi, True)))
    o_ref[0] = o_acc[...].astype(o_ref.dtype)


def _attention(qkv, *, bsz, seq, tq, npair=ATTN_PAIRS):
    qkv3 = qkv.reshape(bsz, seq, 3 * SB_WIDTH)
    width = npair * LANES
    groups = SB_WIDTH // width
    return pl.pallas_call(
        functools.partial(_attn_body, tq=tq, npair=npair),
        grid=(bsz, groups, seq // tq),
        in_specs=[pl.BlockSpec((1, tq, width), lambda b, p, i: (b, i, p)),
                  pl.BlockSpec((1, seq, width), lambda b, p, i: (b, 0, groups + p)),
                  pl.BlockSpec((1, seq, width), lambda b, p, i: (b, 0, 2 * groups + p))],
        out_specs=pl.BlockSpec((1, tq, width), lambda b, p, i: (b, i, p)),
        out_shape=jax.ShapeDtypeStruct((bsz, seq, SB_WIDTH), BF16),
        scratch_shapes=[pltpu.VMEM((tq, width), F32), pltpu.VMEM((npair, 2 * tq, 1), F32)],
        compiler_params=_params("parallel", "parallel", "arbitrary"),
        name="attn",
    )(qkv3, qkv3, qkv3)


def _block_diag(w, group):
    n, k, _ = w.shape
    w = w.reshape(n // group, group, k, k)
    eye = jnp.eye(group, dtype=w.dtype)
    return jnp.einsum('gakl,ab->gakbl', w, eye).reshape(n // group, group * k, group * k)


ROW_WORDS = 4


def _sorted_rows(tm):
    cap = -(-(TOP_K * tm + N_EXPERTS) // LANES) * LANES
    assert (cap // TOP_K) % 16 == 0
    return cap


def _to_row_tiles(ref, value, first_row=0):
    rows, d = value.shape
    bits = lax.bitcast_convert_type(value, I32)
    for s in range(ROW_WORDS):
        low = lax.shift_right_logical(bits[:, s * LANES:(s + 1) * LANES], 16)
        high = bits[:, d // 2 + s * LANES:d // 2 + (s + 1) * LANES]
        ref[pl.ds(first_row * ROW_WORDS + s, rows, stride=ROW_WORDS), :] = high | low


def _from_row_tiles(ref, rows):
    words = [ref[pl.ds(s, rows, stride=ROW_WORDS), :] for s in range(ROW_WORDS)]
    low = [lax.bitcast_convert_type(lax.shift_left(w, 16), F32).astype(BF16) for w in words]
    high = [lax.bitcast_convert_type(w & jnp.int32(-65536), F32).astype(BF16) for w in words]
    return jnp.concatenate(low + high, axis=1)


def _merge_body(x_ref, osb_ref, hg_ref, gg_ref, gbias_ref, wsb_ref, wlru_ref, wout_ref, g2_ref,
                wrt_ref, br_ref, h1_ref, xs_ref, pos_ref, gate_ref, cnt_ref):
    tm, d = x_ref.shape

    y_sb = _dot(osb_ref[...], wsb_ref[...])
    y_lru = _dot(hg_ref[...], wlru_ref[...])
    g_a = _sigmoid(gg_ref[:, 0:d] + gbias_ref[:, 0:d])
    g_b = _sigmoid(gg_ref[:, d:2 * d] + gbias_ref[:, d:2 * d])
    merged = (g_a * y_sb + g_b * y_lru).astype(BF16)
    h1 = x_ref[...] + _dot(merged, wout_ref[...])
    h1_ref[...] = h1
    u2 = _rms(h1, g2_ref[...]).astype(BF16)
    logits = lax.dot_general(wrt_ref[...], u2, (((1,), (1,)), ((), ())),
                             preferred_element_type=F32) + br_ref[...]

    e_iota = lax.broadcasted_iota(I32, (N_EXPERTS, tm), 0)
    k_iota = lax.broadcasted_iota(I32, (TOP_K, tm), 0)
    vals = logits
    chosen = jnp.zeros((N_EXPERTS, tm), F32)
    top_v, hots = [], []
    for _ in range(TOP_K):
        m = jnp.max(vals, axis=0, keepdims=True)
        sel = jnp.min(jnp.where(vals == m, e_iota, N_EXPERTS), axis=0, keepdims=True)
        hot = e_iota == sel
        top_v.append(m)
        hots.append(hot)
        chosen = jnp.where(hot, 1.0, chosen)
        vals = jnp.where(hot, -jnp.inf, vals)
    exps = [jnp.exp(v - top_v[0]) for v in top_v]
    denom = exps[0] + exps[1] + exps[2] + exps[3]

    chosen_b = chosen.astype(BF16)
    t_row = lax.broadcasted_iota(I32, (tm, tm), 0)
    t_col = lax.broadcasted_iota(I32, (tm, tm), 1)
    earlier = jnp.where(t_row < t_col, 1.0, 0.0).astype(BF16)
    same_before = _dot(chosen_b, earlier)
    e_row = lax.broadcasted_iota(I32, (N_EXPERTS, N_EXPERTS), 0)
    e_col = lax.broadcasted_iota(I32, (N_EXPERTS, N_EXPERTS), 1)
    smaller = jnp.where(e_col < e_row, 1.0, 0.0).astype(BF16)
    cnt = jnp.sum(chosen, axis=1, keepdims=True)
    odd = cnt - 2.0 * jnp.floor(0.5 * cnt)
    cnt_ref[...] = cnt + odd
    odd_smaller = _dot(smaller, jnp.broadcast_to(odd, (N_EXPERTS, LANES)).astype(BF16))[:, 0:1]
    first = jnp.sum(_dot(smaller, chosen_b), axis=1, keepdims=True) + odd_smaller

    where_to = first + same_before
    pos = jnp.zeros((TOP_K, tm), I32)
    gate = jnp.zeros((TOP_K, tm), F32)
    pos_k = []
    for k in range(TOP_K):
        pk = jnp.sum(jnp.where(hots[k], where_to, 0.0), axis=0, keepdims=True).astype(I32)
        pos_k.append(pk)
        pos = jnp.where(k_iota == k, pk, pos)
        gate = jnp.where(k_iota == k, exps[k] / denom, gate)
    pos_ref[...] = pos
    gate_ref[...] = gate

    cap = xs_ref.shape[0] // ROW_WORDS
    piece = cap // TOP_K
    for c in range(TOP_K):
        p_iota = lax.broadcasted_iota(I32, (piece, tm), 0) + c * piece
        hit = p_iota == pos_k[0]
        for k in range(1, TOP_K):
            hit = jnp.logical_or(hit, p_iota == pos_k[k])
        perm = jnp.where(hit, 1.0, 0.0).astype(BF16)
        _to_row_tiles(xs_ref, _dot(perm, u2), first_row=c * piece)


def _merge(x2, o_sb, hg, gg, gate_bias, wsb, wlru, wout, g2, wrt, br, *, tm):
    n, d = x2.shape
    tiles = n // tm
    assert d == 2 * ROW_WORDS * LANES, "row tile form packs 1024 columns into 4 x 128 words"
    cap = _sorted_rows(tm)
    row = lambda w: pl.BlockSpec((tm, w), lambda i: (i, 0))
    return pl.pallas_call(
        _merge_body,
        grid=(tiles,),
        in_specs=[row(d), row(SB_WIDTH), row(d), row(2 * d), _resident((1, 2 * d)),
                  _resident(wsb.shape), _resident(wlru.shape), _resident(wout.shape),
                  _resident((1, d)), _resident(wrt.shape), _resident((N_EXPERTS, 1))],
        out_specs=[row(d),
                   pl.BlockSpec((cap * ROW_WORDS, LANES), lambda i: (i, 0)),
                   pl.BlockSpec((TOP_K, tm), lambda i: (0, i)),
                   pl.BlockSpec((TOP_K, tm), lambda i: (0, i)),
                   pl.BlockSpec((N_EXPERTS, 1), lambda i: (i, 0))],
        out_shape=[jax.ShapeDtypeStruct((n, d), F32),
                   jax.ShapeDtypeStruct((tiles * cap * ROW_WORDS, LANES), I32),
                   jax.ShapeDtypeStruct((TOP_K, n), I32),
                   jax.ShapeDtypeStruct((TOP_K, n), F32),
                   jax.ShapeDtypeStruct((tiles * N_EXPERTS, 1), F32)],
        compiler_params=_params("parallel"),
        name="merge",
    )(x2, o_sb, hg, gg, gate_bias, wsb, wlru, wout, g2, wrt, br)


def _experts_body(eb_ref, be_ref, r0_ref, t0_ref, nv_ref, cum_ref, off_ref,
                  xs_hbm, w1_ref, b1_ref, w2_ref, b2_ref, ys_hbm,
                  xbuf, ybuf, w1b, w2b, gsem, ssem, *, tiles, tile_rows):
    expert = pl.program_id(0)
    rows = xbuf.shape[1] // ROW_WORDS
    ff = w2_ref.shape[1]
    n_used = eb_ref[N_EXPERTS]
    last_entry = be_ref.shape[0] - 1

    def for_runs(b, nv, fn):
        e = be_ref[b]
        r0 = r0_ref[b]
        end = r0 + nv

        def more(t):
            return jnp.logical_and(t < tiles, cum_ref[jnp.minimum(t, tiles) * N_EXPERTS + e] < end)

        def run(t):
            c0 = cum_ref[t * N_EXPERTS + e]
            c1 = cum_ref[(t + 1) * N_EXPERTS + e]
            lo = jnp.maximum(c0, r0)
            n_rows = jnp.minimum(c1, end) - lo

            @pl.when(n_rows > 0)
            def _():
                src = t * tile_rows + off_ref[t * N_EXPERTS + e] + (lo - c0)
                fn(pl.multiple_of(src * ROW_WORDS, SUBLANES),
                   pl.multiple_of((lo - r0) * ROW_WORDS, SUBLANES), n_rows * ROW_WORDS)

            return t + 1

        @pl.when(nv > 0)
        def _():
            lax.while_loop(more, run, t0_ref[b])

    def fetch(b, nv, s):
        def one(src, dst, size):
            pltpu.make_async_copy(xs_hbm.at[pl.ds(src, size)], xbuf.at[s, pl.ds(dst, size)],
                                  gsem.at[s]).start()
        for_runs(b, nv, one)

    def write_back(b, nv, s, sem):
        def one(src, dst, size):
            pltpu.make_async_copy(ybuf.at[s, pl.ds(dst, size)], ys_hbm.at[pl.ds(src, size)], sem).start()
        for_runs(b, nv, one)

    def wait_rows(nv, sem):
        @pl.when(nv > 0)
        def _():
            size = nv * ROW_WORDS
            pltpu.make_async_copy(xs_hbm.at[pl.ds(0, size)], xbuf.at[0, pl.ds(0, size)], sem).wait()

    def neighbours(b):
        prev = jnp.maximum(b - 1, 0)
        nv_prev = jnp.where(b >= 1, nv_ref[prev], 0)
        nv_prev2 = jnp.where(b >= 2, nv_ref[jnp.maximum(b - 2, 0)], 0)
        return prev, nv_prev, nv_prev2

    @pl.when(expert == 0)
    def _():
        xbuf[...] = jnp.zeros_like(xbuf)
        fetch(0, nv_ref[0], 0)

    @pl.when(eb_ref[expert + 1] > eb_ref[expert])
    def _():
        w1b[...] = w1_ref[0].astype(BF16)
        w2b[...] = w2_ref[0].astype(BF16)

    def run_block(b, carry):
        slot = b % 2
        prev, nv_prev, nv_prev2 = neighbours(b)
        nxt = jnp.minimum(b + 1, last_entry)
        write_back(prev, nv_prev, 1 - slot, ssem.at[slot])
        fetch(nxt, nv_ref[nxt], 1 - slot)
        wait_rows(nv_ref[b], gsem.at[slot])
        x = _from_row_tiles(xbuf.at[slot], rows)
        hid = _dot(x, w1b[...]) + b1_ref[0]
        x_glu = jnp.minimum(hid[:, 0:ff], SWIGLU_LIMIT)
        x_lin = jnp.clip(hid[:, ff:2 * ff], -SWIGLU_LIMIT, SWIGLU_LIMIT)
        act = x_glu * _sigmoid(SWIGLU_ALPHA * x_glu) * (x_lin + 1.0)
        y = _dot(act.astype(BF16), w2b[...]) + b2_ref[0]
        wait_rows(nv_prev2, ssem.at[1 - slot])
        _to_row_tiles(ybuf.at[slot], y.astype(BF16).astype(F32))
        return carry

    lax.fori_loop(eb_ref[expert], eb_ref[expert + 1], run_block, 0)

    @pl.when(expert == pl.num_programs(0) - 1)
    def _():
        slot = n_used % 2
        prev, nv_prev, nv_prev2 = neighbours(n_used)
        write_back(prev, nv_prev, 1 - slot, ssem.at[slot])
        wait_rows(nv_prev2, ssem.at[1 - slot])
        wait_rows(nv_prev, ssem.at[slot])


def _experts(expert_block, block_e, block_r0, block_t0, block_nv, cum, off, xs, w1, b1, w2, b2, *,
             rows, tiles, tile_rows):
    _, d, ff2 = w1.shape
    ff = ff2 // 2
    weights = lambda shape: pl.BlockSpec(shape, lambda e, *_: (e, 0, 0))
    grid_spec = pltpu.PrefetchScalarGridSpec(
        num_scalar_prefetch=7,
        grid=(N_EXPERTS,),
        in_specs=[pl.BlockSpec(memory_space=pl.ANY),
                  weights((1, d, ff2)), weights((1, 1, ff2)), weights((1, ff, d)), weights((1, 1, d))],
        out_specs=pl.BlockSpec(memory_space=pl.ANY),
        scratch_shapes=[pltpu.VMEM((2, rows * ROW_WORDS, LANES), I32),
                        pltpu.VMEM((2, rows * ROW_WORDS, LANES), I32),
                        pltpu.VMEM((d, ff2), BF16),
                        pltpu.VMEM((ff, d), BF16),
                        pltpu.SemaphoreType.DMA((2,)),
                        pltpu.SemaphoreType.DMA((2,))],
    )
    return pl.pallas_call(
        functools.partial(_experts_body, tiles=tiles, tile_rows=tile_rows),
        grid_spec=grid_spec,
        out_shape=jax.ShapeDtypeStruct(xs.shape, I32),
        input_output_aliases={7: 0},
        compiler_params=_params("arbitrary"),
        name="experts",
    )(expert_block, block_e, block_r0, block_t0, block_nv, cum, off, xs, w1,
      b1.reshape(N_EXPERTS, 1, ff2), w2, b2.reshape(N_EXPERTS, 1, d))


def _combine_body(pos_ref, gate_ref, h1_ref, g_ref, ys_ref, out_ref):
    tm = h1_ref.shape[0]
    cap = ys_ref.shape[0] // ROW_WORDS
    p_iota = lax.broadcasted_iota(I32, (tm, cap), 1)
    pos = pos_ref[...]
    gate = gate_ref[...]
    g_mat = jnp.zeros((tm, cap), F32)
    for k in range(TOP_K):
        g_mat = jnp.where(p_iota == pos[:, k:k + 1], gate[:, k:k + 1], g_mat)
    ys = _from_row_tiles(ys_ref, cap)
    out_ref[...] = _rms(h1_ref[...] + _dot(g_mat.astype(BF16), ys), g_ref[...])


def _combine(pos, gate, h1, g, ys, *, tm):
    n, d = h1.shape
    return pl.pallas_call(
        _combine_body,
        grid=(n // tm,),
        in_specs=[pl.BlockSpec((tm, TOP_K), lambda i: (i, 0)),
                  pl.BlockSpec((tm, TOP_K), lambda i: (i, 0)),
                  pl.BlockSpec((tm, d), lambda i: (i, 0)),
                  _resident((1, d)),
                  pl.BlockSpec((_sorted_rows(tm) * ROW_WORDS, LANES), lambda i: (i, 0))],
        out_specs=pl.BlockSpec((tm, d), lambda i: (i, 0)),
        out_shape=jax.ShapeDtypeStruct((n, d), F32),
        compiler_params=_params("parallel"),
        name="combine",
    )(pos, gate, h1, g, ys)


def _block_tables(cnt, rows):
    cum = jnp.concatenate([jnp.zeros((1, N_EXPERTS), I32), jnp.cumsum(cnt, axis=0)])
    off = jnp.cumsum(cnt, axis=1) - cnt
    total = cum[-1]
    blocks = (total + rows - 1) // rows
    block_end = jnp.cumsum(blocks)
    return cum, off, total, blocks, block_end


def _layer(h2, bsz, seq, norm_mix_g, w_in, gate_bias, conv_w, conv_b, w_rg_a, b_rg_a, w_rg_x, b_rg_x,
           lru_lambda, w_sb_o, w_lru_o, w_out, norm_ffn_g, w_router, b_router, w1, b1, w2, b2, out_g):
    n, d = h2.shape
    tm = min(ROW_TILE, n)
    tiles = n // tm
    group = MXU_DIM // (d // LRU_BLOCKS)
    qkv, gg, hg = _inproj(h2.reshape(bsz, seq, d), norm_mix_g.reshape(1, d), w_in.astype(BF16),
                          conv_w, conv_b.reshape(1, d),
                          _block_diag(w_rg_a, group).astype(BF16), b_rg_a.reshape(1, d),
                          _block_diag(w_rg_x, group).astype(BF16), b_rg_x.reshape(1, d),
                          lru_lambda.reshape(1, d), tt=min(INPROJ_TIME_BLOCK, seq))
    o_sb = _attention(qkv, bsz=bsz, seq=seq, tq=min(ATTN_BLOCK, seq)).reshape(n, SB_WIDTH)
    h1, xs, pos, gate, cnt = _merge(
        h2, o_sb, hg.reshape(n, d), gg.reshape(n, 2 * d), gate_bias.reshape(1, 2 * d),
        w_sb_o.astype(BF16), w_lru_o.astype(BF16),
        w_out.astype(BF16), norm_ffn_g.reshape(1, d), w_router.T.astype(BF16),
        b_router.reshape(N_EXPERTS, 1), tm=tm)

    rows = MOE_ROWS
    n_steps = (n * TOP_K + tiles * N_EXPERTS + N_EXPERTS * (rows - 1)) // rows + 1
    cnt = cnt.reshape(tiles, N_EXPERTS).astype(I32)
    cum, off, total, blocks, block_end = _block_tables(cnt, rows)
    step = jnp.arange(n_steps, dtype=I32)
    block_e = jnp.minimum(jnp.sum((step[:, None] >= block_end[None, :]).astype(I32), axis=1), N_EXPERTS - 1)
    block_r0 = (step - (block_end - blocks)[block_e]) * rows
    block_nv = jnp.clip(total[block_e] - block_r0, 0, rows)
    block_nv = jnp.where(step < block_end[-1], block_nv, 0)
    block_t0 = jnp.sum((cum[1:, :].T[block_e] <= block_r0[:, None]).astype(I32), axis=1)
    block_t0 = jnp.minimum(block_t0, tiles - 1)
    expert_block = jnp.concatenate([jnp.zeros((1,), I32), block_end.astype(I32)])

    ys = _experts(expert_block, block_e, block_r0, block_t0, block_nv, cum.reshape(-1), off.reshape(-1),
                  xs, w1, b1, w2, b2, rows=rows, tiles=tiles, tile_rows=_sorted_rows(tm))
    return _combine(pos.T, gate.T, h1, out_g.reshape(1, d), ys, tm=tm)


def kernel(x, norm_mix_g, w_in, gate_bias, conv_w, conv_b, w_rg_a, b_rg_a, w_rg_x, b_rg_x, lru_lambda,
           w_sb_o, w_lru_o, w_out, norm_ffn_g, w_router, b_router, w1, b1, w2, b2, norm_final_g):
    bsz, seq, d = x.shape
    depth = w_in.shape[0]
    assert depth == 1, "the final norm is fused into the last layer's combine"
    assert bsz % SUBLANES == 0 and d % MXU_DIM == 0
    h2 = x.reshape(bsz * seq, d)
    out = _layer(h2, bsz, seq, norm_mix_g[0], w_in[0], gate_bias[0], conv_w[0], conv_b[0], w_rg_a[0],
                 b_rg_a[0], w_rg_x[0], b_rg_x[0], lru_lambda[0], w_sb_o[0], w_lru_o[0], w_out[0],
                 norm_ffn_g[0], w_router[0], b_router[0], w1[0], b1[0], w2[0], b2[0], norm_final_g)
    return out.reshape(bsz, seq, d)
```

```python
import functools

import jax
import jax.numpy as jnp
from jax import lax
from jax.experimental import pallas as pl
from jax.experimental.pallas import tpu as pltpu

F32 = jnp.float32
BF16 = jnp.bfloat16
I32 = jnp.int32

SB_HEADS = 8
SB_HEAD_DIM = 64
SB_WIDTH = SB_HEADS * SB_HEAD_DIM
LRU_BLOCKS = 16
CONV_WIDTH = 4
RG_C = 8.0
N_EXPERTS = 32
TOP_K = 4
SWIGLU_ALPHA = 1.702
SWIGLU_LIMIT = 7.0
NORM_EPS = 1e-6

LANES = 128
SUBLANES = 8
MXU_DIM = 256
VMEM_LIMIT = 56 * 1024 * 1024

ROW_TILE = 512
ATTN_PAIRS = 4
ATTN_BLOCK = 256
ATTN_DEAD_CARRY = 104.0
INPROJ_TIME_BLOCK = 64
MOE_ROWS = 512


def _rms(x, g):
    return x * lax.rsqrt(jnp.mean(x * x, axis=-1, keepdims=True) + NORM_EPS) * g


def _sigmoid(x):
    return 1.0 / (1.0 + jnp.exp(-x))


def _sigmoid_t(x):
    return 0.5 * jnp.tanh(0.5 * x) + 0.5


def _softplus(x):
    return jnp.maximum(x, 0.0) + jnp.log(1.0 + jnp.exp(-jnp.abs(x)))


def _dot(a, b):
    return jnp.dot(a, b, preferred_element_type=F32)


def _params(*sem):
    return pltpu.CompilerParams(dimension_semantics=sem, vmem_limit_bytes=VMEM_LIMIT)


def _resident(shape):
    return pl.BlockSpec(shape, lambda *_: (0,) * len(shape), pipeline_mode=pl.Buffered(1))


def _inproj_body(x_ref, g_ref, w_ref, cw_ref, cb_ref, wa_ref, ba_ref, wx_ref, bx_ref, lam_ref,
                 qkv_ref, gg_ref, hg_ref, xpad, y_s, a_s, b_s, h_s, *, chunk):
    s = pl.program_id(1)
    nb, tt, d = x_ref.shape
    chunks = d // LANES
    n_rows = tt * nb
    halo = (CONV_WIDTH - 1) * nb
    u = _rms(x_ref[...].reshape(n_rows, d), g_ref[...]).astype(BF16)

    def project(col, width, store):
        for c in range(0, width, chunk):
            store(c, _dot(u, w_ref[:, col + c:col + c + chunk]))

    @pl.when(s == 0)
    def _():
        xpad[:, 0:halo, :] = jnp.zeros((chunks, halo, LANES), F32)
        h_s[...] = jnp.zeros_like(h_s)

    @pl.when(s > 0)
    def _():
        xpad[:, 0:halo, :] = xpad[:, n_rows:n_rows + halo, :]

    def store_x(c, val):
        for j in range(chunk // LANES):
            for b in range(nb):
                xpad[c // LANES + j, pl.ds(halo + b, tt, stride=nb), :] = (
                    val[b * tt:(b + 1) * tt, j * LANES:(j + 1) * LANES])

    def store_y(c, val):
        y_s[:, c:c + chunk] = val

    def store_to(ref):
        def store(c, val):
            ref[:, :, c:c + chunk] = val.astype(ref.dtype).reshape(nb, tt, chunk)
        return store

    x_col = qkv_ref.shape[2]
    project(x_col, d, store_x)
    todo = ([(x_col + d + c, store_y, c) for c in range(0, d, chunk)]
            + [(c, store_to(qkv_ref), c) for c in range(0, x_col, chunk)]
            + [(x_col + 2 * d + c, store_to(gg_ref), c) for c in range(0, 2 * d, chunk)])

    def project_next(count=1):
        for _ in range(count):
            if todo:
                col, store, c = todo.pop(0)
                store(c, _dot(u, w_ref[:, col:col + chunk]))

    def conv(j):
        lanes = slice(j * LANES, (j + 1) * LANES)
        xc = jnp.zeros((n_rows, LANES), F32) + cb_ref[:, lanes]
        for i in range(CONV_WIDTH):
            xc = xc + cw_ref[i:i + 1, lanes] * xpad[j, i * nb:i * nb + n_rows, :]
        return xc

    neg_sp_lam = -RG_C * _softplus(-lam_ref[...])
    per_group = MXU_DIM // LANES
    for g in range(chunks // per_group):
        project_next(2)
        sl = slice(g * MXU_DIM, (g + 1) * MXU_DIM)
        xc = jnp.concatenate([conv(g * per_group + j) for j in range(per_group)], axis=1)
        xb = xc.astype(BF16)
        r = _sigmoid_t(_dot(xb, wa_ref[g]) + ba_ref[:, sl])
        gate_i = _sigmoid_t(_dot(xb, wx_ref[g]) + bx_ref[:, sl])
        a = jnp.exp(r * neg_sp_lam[:, sl])
        var = (1.0 - a) * (1.0 + a)
        mult = jnp.where(var > 0.0, var * lax.rsqrt(var), 0.0)
        b = mult * (gate_i * xc)
        for j in range(per_group):
            a_s[g * per_group + j] = a[:, j * LANES:(j + 1) * LANES]
            b_s[g * per_group + j] = b[:, j * LANES:(j + 1) * LANES]

    hs = [h_s[j] for j in range(chunks)]
    for t in range(tt):
        if t % (tt // 8) == 0 and t > 0:
            project_next()
        rows = slice(t * nb, (t + 1) * nb)
        for j in range(chunks):
            hs[j] = a_s[j, rows, :] * hs[j] + b_s[j, rows, :]
            b_s[j, rows, :] = hs[j]
    for j in range(chunks):
        h_s[j] = hs[j]

    for j in range(chunks):
        project_next(len(todo) if j == chunks - 1 else 1)
        for bi in range(nb):
            y = y_s[bi * tt:(bi + 1) * tt, j * LANES:(j + 1) * LANES]
            gelu = 0.5 * y * (1.0 + jnp.tanh(0.7978845608028654 * (y + 0.044715 * (y * y * y))))
            h = b_s[j, pl.ds(bi, tt, stride=nb), :]
            hg_ref[bi, :, j * LANES:(j + 1) * LANES] = (h * gelu).astype(hg_ref.dtype)


def _inproj(x3, g, w_in_bf, conv_w, conv_b, wa_bd, b_a, wx_bd, b_x, lam, *, tt):
    bsz, seq, d = x3.shape
    nb = SUBLANES
    chunks = d // LANES
    tile = lambda w: pl.BlockSpec((nb, tt, w), lambda b, s: (b, s, 0))
    return pl.pallas_call(
        functools.partial(_inproj_body, chunk=MXU_DIM),
        grid=(bsz // nb, seq // tt),
        in_specs=[tile(d), _resident((1, d)), _resident(w_in_bf.shape),
                  _resident(conv_w.shape), _resident((1, d)),
                  _resident(wa_bd.shape), _resident((1, d)),
                  _resident(wx_bd.shape), _resident((1, d)),
                  _resident((1, d))],
        out_specs=[tile(3 * SB_WIDTH), tile(2 * d), tile(d)],
        out_shape=[jax.ShapeDtypeStruct((bsz, seq, 3 * SB_WIDTH), BF16),
                   jax.ShapeDtypeStruct((bsz, seq, 2 * d), F32),
                   jax.ShapeDtypeStruct((bsz, seq, d), BF16)],
        scratch_shapes=[pltpu.VMEM((chunks, (tt + CONV_WIDTH - 1) * nb, LANES), F32),
                        pltpu.VMEM((nb * tt, d), F32),
                        pltpu.VMEM((chunks, tt * nb, LANES), F32),
                        pltpu.VMEM((chunks, tt * nb, LANES), F32),
                        pltpu.VMEM((chunks, nb, LANES), F32)],
        compiler_params=_params("parallel", "arbitrary"),
        name="inproj",
    )(x3, g, w_in_bf, conv_w, conv_b, wa_bd, b_a, wx_bd, b_x, lam)


def _attn_body(q_ref, k_ref, v_ref, o_ref, o_acc, c_acc, *, tq, npair):
    qi = pl.program_id(2)
    lane = lax.broadcasted_iota(I32, (1, LANES), 1)
    first = lane < SB_HEAD_DIM
    q_pairs = []
    for p in range(npair):
        q = q_ref[0, :, p * LANES:(p + 1) * LANES] * (SB_HEAD_DIM ** -0.5)
        zq = jnp.zeros_like(q)
        q_pairs.append(jnp.concatenate([jnp.where(first, q, zq), jnp.where(first, zq, q)], axis=0))
    row = lax.broadcasted_iota(I32, (2 * tq, tq), 0)
    row = jnp.where(row >= tq, row - tq, row)
    col = lax.broadcasted_iota(I32, (2 * tq, tq), 1)
    causal = col < row
    tri2 = jnp.where(row > col, 1.0, 0.0).astype(BF16)

    def block(j, diag):
        start = pl.multiple_of(j * tq, tq)
        zs, sps, sufs = [], [], []
        for p in range(npair):
            kj = k_ref[0, pl.ds(start, tq), p * LANES:(p + 1) * LANES]
            zs.append(lax.dot_general(q_pairs[p], kj, (((1,), (1,)), ((), ())), preferred_element_type=F32))
        for p in range(npair):
            sp = _softplus(zs[p])
            if diag:
                sp = jnp.where(causal, sp, 0.0)
            sps.append(sp)
            sp_hi = sp.astype(BF16)
            sp_lo = (sp - sp_hi.astype(F32)).astype(BF16)
            sufs.append(_dot(jnp.concatenate([sp_hi, sp_lo], axis=1), tri2))
        low = None
        for p in range(npair):
            vj = v_ref[0, pl.ds(start, tq), p * LANES:(p + 1) * LANES]
            zv = jnp.zeros_like(vj)
            v2 = jnp.concatenate([jnp.where(first, vj, zv), jnp.where(first, zv, vj)], axis=0)
            arg = zs[p] - sps[p] - sufs[p]
            if not diag:
                arg = arg - c_acc[p]
            att = jnp.exp(arg)
            if diag:
                att = jnp.where(causal, att, 0.0)
            att = att.astype(BF16)
            pv = _dot(jnp.concatenate([att[0:tq], att[tq:2 * tq]], axis=1), v2)
            if not diag:
                pv = pv + o_acc[:, p * LANES:(p + 1) * LANES]
            o_acc[:, p * LANES:(p + 1) * LANES] = pv
            c = jnp.sum(sps[p], axis=-1, keepdims=True)
            if not diag:
                c = c + c_acc[p]
            c_acc[p] = c
            m = jnp.min(c)
            low = m if low is None else jnp.minimum(low, m)
        return low

    def more(state):
        j, low = state
        return jnp.logical_and(j >= 0, low < ATTN_DEAD_CARRY)

    def step(state):
        j, _ = state
        return j - 1, block(j, False)

    lax.while_loop(more, step, (qi - 1, block(qi, True)))
    o_ref[0] = o_acc[...].astype(o_ref.dtype)


def _attention(qkv, *, bsz, seq, tq, npair=ATTN_PAIRS):
    qkv3 = qkv.reshape(bsz, seq, 3 * SB_WIDTH)
    width = npair * LANES
    groups = SB_WIDTH // width
    return pl.pallas_call(
        functools.partial(_attn_body, tq=tq, npair=npair),
        grid=(bsz, groups, seq // tq),
        in_specs=[pl.BlockSpec((1, tq, width), lambda b, p, i: (b, i, p)),
                  pl.BlockSpec((1, seq, width), lambda b, p, i: (b, 0, groups + p)),
                  pl.BlockSpec((1, seq, width), lambda b, p, i: (b, 0, 2 * groups + p))],
        out_specs=pl.BlockSpec((1, tq, width), lambda b, p, i: (b, i, p)),
        out_shape=jax.ShapeDtypeStruct((bsz, seq, SB_WIDTH), BF16),
        scratch_shapes=[pltpu.VMEM((tq, width), F32), pltpu.VMEM((npair, 2 * tq, 1), F32)],
        compiler_params=_params("parallel", "parallel", "arbitrary"),
        name="attn",
    )(qkv3, qkv3, qkv3)


def _block_diag(w, group):
    n, k, _ = w.shape
    w = w.reshape(n // group, group, k, k)
    eye = jnp.eye(group, dtype=w.dtype)
    return jnp.einsum('gakl,ab->gakbl', w, eye).reshape(n // group, group * k, group * k)


ROW_WORDS = 4


def _sorted_rows(tm):
    cap = -(-(TOP_K * tm + N_EXPERTS) // LANES) * LANES
    assert (cap // TOP_K) % 16 == 0
    return cap


def _to_row_tiles(ref, value, first_row=0):
    rows, d = value.shape
    bits = lax.bitcast_convert_type(value, I32)
    for s in range(ROW_WORDS):
        low = lax.shift_right_logical(bits[:, s * LANES:(s + 1) * LANES], 16)
        high = bits[:, d // 2 + s * LANES:d // 2 + (s + 1) * LANES]
        ref[pl.ds(first_row * ROW_WORDS + s, rows, stride=ROW_WORDS), :] = high | low


def _from_row_tiles(ref, rows):
    words = [ref[pl.ds(s, rows, stride=ROW_WORDS), :] for s in range(ROW_WORDS)]
    low = [lax.bitcast_convert_type(lax.shift_left(w, 16), F32).astype(BF16) for w in words]
    high = [lax.bitcast_convert_type(w & jnp.int32(-65536), F32).astype(BF16) for w in words]
    return jnp.concatenate(low + high, axis=1)


def _merge_body(x_ref, osb_ref, hg_ref, gg_ref, gbias_ref, wsb_ref, wlru_ref, wout_ref, g2_ref,
                wrt_ref, br_ref, h1_ref, xs_ref, pos_ref, gate_ref, cnt_ref):
    tm, d = x_ref.shape

    y_sb = _dot(osb_ref[...], wsb_ref[...])
    y_lru = _dot(hg_ref[...], wlru_ref[...])
    g_a = _sigmoid(gg_ref[:, 0:d] + gbias_ref[:, 0:d])
    g_b = _sigmoid(gg_ref[:, d:2 * d] + gbias_ref[:, d:2 * d])
    merged = (g_a * y_sb + g_b * y_lru).astype(BF16)
    h1 = x_ref[...] + _dot(merged, wout_ref[...])
    h1_ref[...] = h1
    u2 = _rms(h1, g2_ref[...]).astype(BF16)
    logits = lax.dot_general(wrt_ref[...], u2, (((1,), (1,)), ((), ())),
                             preferred_element_type=F32) + br_ref[...]

    e_iota = lax.broadcasted_iota(I32, (N_EXPERTS, tm), 0)
    k_iota = lax.broadcasted_iota(I32, (TOP_K, tm), 0)
    vals = logits
    chosen = jnp.zeros((N_EXPERTS, tm), F32)
    top_v, hots = [], []
    for _ in range(TOP_K):
        m = jnp.max(vals, axis=0, keepdims=True)
        sel = jnp.min(jnp.where(vals == m, e_iota, N_EXPERTS), axis=0, keepdims=True)
        hot = e_iota == sel
        top_v.append(m)
        hots.append(hot)
        chosen = jnp.where(hot, 1.0, chosen)
        vals = jnp.where(hot, -jnp.inf, vals)
    exps = [jnp.exp(v - top_v[0]) for v in top_v]
    denom = exps[0] + exps[1] + exps[2] + exps[3]

    chosen_b = chosen.astype(BF16)
    t_row = lax.broadcasted_iota(I32, (tm, tm), 0)
    t_col = lax.broadcasted_iota(I32, (tm, tm), 1)
    earlier = jnp.where(t_row < t_col, 1.0, 0.0).astype(BF16)
    same_before = _dot(chosen_b, earlier)
    e_row = lax.broadcasted_iota(I32, (N_EXPERTS, N_EXPERTS), 0)
    e_col = lax.broadcasted_iota(I32, (N_EXPERTS, N_EXPERTS), 1)
    smaller = jnp.where(e_col < e_row, 1.0, 0.0).astype(BF16)
    cnt = jnp.sum(chosen, axis=1, keepdims=True)
    odd = cnt - 2.0 * jnp.floor(0.5 * cnt)
    cnt_ref[...] = cnt + odd
    odd_smaller = _dot(smaller, jnp.broadcast_to(odd, (N_EXPERTS, LANES)).astype(BF16))[:, 0:1]
    first = jnp.sum(_dot(smaller, chosen_b), axis=1, keepdims=True) + odd_smaller

    where_to = first + same_before
    pos = jnp.zeros((TOP_K, tm), I32)
    gate = jnp.zeros((TOP_K, tm), F32)
    pos_k = []
    for k in range(TOP_K):
        pk = jnp.sum(jnp.where(hots[k], where_to, 0.0), axis=0, keepdims=True).astype(I32)
        pos_k.append(pk)
        pos = jnp.where(k_iota == k, pk, pos)
        gate = jnp.where(k_iota == k, exps[k] / denom, gate)
    pos_ref[...] = pos
    gate_ref[...] = gate

    cap = xs_ref.shape[0] // ROW_WORDS
    piece = cap // TOP_K
    for c in range(TOP_K):
        p_iota = lax.broadcasted_iota(I32, (piece, tm), 0) + c * piece
        hit = p_iota == pos_k[0]
        for k in range(1, TOP_K):
            hit = jnp.logical_or(hit, p_iota == pos_k[k])
        perm = jnp.where(hit, 1.0, 0.0).astype(BF16)
        _to_row_tiles(xs_ref, _dot(perm, u2), first_row=c * piece)


def _merge(x2, o_sb, hg, gg, gate_bias, wsb, wlru, wout, g2, wrt, br, *, tm):
    n, d = x2.shape
    tiles = n // tm
    assert d == 2 * ROW_WORDS * LANES, "row tile form packs 1024 columns into 4 x 128 words"
    cap = _sorted_rows(tm)
    row = lambda w: pl.BlockSpec((tm, w), lambda i: (i, 0))
    return pl.pallas_call(
        _merge_body,
        grid=(tiles,),
        in_specs=[row(d), row(SB_WIDTH), row(d), row(2 * d), _resident((1, 2 * d)),
                  _resident(wsb.shape), _resident(wlru.shape), _resident(wout.shape),
                  _resident((1, d)), _resident(wrt.shape), _resident((N_EXPERTS, 1))],
        out_specs=[row(d),
                   pl.BlockSpec((cap * ROW_WORDS, LANES), lambda i: (i, 0)),
                   pl.BlockSpec((TOP_K, tm), lambda i: (0, i)),
                   pl.BlockSpec((TOP_K, tm), lambda i: (0, i)),
                   pl.BlockSpec((N_EXPERTS, 1), lambda i: (i, 0))],
        out_shape=[jax.ShapeDtypeStruct((n, d), F32),
                   jax.ShapeDtypeStruct((tiles * cap * ROW_WORDS, LANES), I32),
                   jax.ShapeDtypeStruct((TOP_K, n), I32),
                   jax.ShapeDtypeStruct((TOP_K, n), F32),
                   jax.ShapeDtypeStruct((tiles * N_EXPERTS, 1), F32)],
        compiler_params=_params("parallel"),
        name="merge",
    )(x2, o_sb, hg, gg, gate_bias, wsb, wlru, wout, g2, wrt, br)


def _experts_body(eb_ref, be_ref, r0_ref, t0_ref, nv_ref, cum_ref, off_ref,
                  xs_hbm, w1_ref, b1_ref, w2_ref, b2_ref, ys_hbm,
                  xbuf, ybuf, w1b, w2b, gsem, ssem, *, tiles, tile_rows):
    expert = pl.program_id(0)
    rows = xbuf.shape[1] // ROW_WORDS
    ff = w2_ref.shape[1]
    n_used = eb_ref[N_EXPERTS]
    last_entry = be_ref.shape[0] - 1

    def for_runs(b, nv, fn):
        e = be_ref[b]
        r0 = r0_ref[b]
        end = r0 + nv

        def more(t):
            return jnp.logical_and(t < tiles, cum_ref[jnp.minimum(t, tiles) * N_EXPERTS + e] < end)

        def run(t):
            c0 = cum_ref[t * N_EXPERTS + e]
            c1 = cum_ref[(t + 1) * N_EXPERTS + e]
            lo = jnp.maximum(c0, r0)
            n_rows = jnp.minimum(c1, end) - lo

            @pl.when(n_rows > 0)
            def _():
                src = t * tile_rows + off_ref[t * N_EXPERTS + e] + (lo - c0)
                fn(pl.multiple_of(src * ROW_WORDS, SUBLANES),
                   pl.multiple_of((lo - r0) * ROW_WORDS, SUBLANES), n_rows * ROW_WORDS)

            return t + 1

        @pl.when(nv > 0)
        def _():
            lax.while_loop(more, run, t0_ref[b])

    def fetch(b, nv, s):
        def one(src, dst, size):
            pltpu.make_async_copy(xs_hbm.at[pl.ds(src, size)], xbuf.at[s, pl.ds(dst, size)],
                                  gsem.at[s]).start()
        for_runs(b, nv, one)

    def write_back(b, nv, s, sem):
        def one(src, dst, size):
            pltpu.make_async_copy(ybuf.at[s, pl.ds(dst, size)], ys_hbm.at[pl.ds(src, size)], sem).start()
        for_runs(b, nv, one)

    def wait_rows(nv, sem):
        @pl.when(nv > 0)
        def _():
            size = nv * ROW_WORDS
            pltpu.make_async_copy(xs_hbm.at[pl.ds(0, size)], xbuf.at[0, pl.ds(0, size)], sem).wait()

    def neighbours(b):
        prev = jnp.maximum(b - 1, 0)
        nv_prev = jnp.where(b >= 1, nv_ref[prev], 0)
        nv_prev2 = jnp.where(b >= 2, nv_ref[jnp.maximum(b - 2, 0)], 0)
        return prev, nv_prev, nv_prev2

    @pl.when(expert == 0)
    def _():
        xbuf[...] = jnp.zeros_like(xbuf)
        fetch(0, nv_ref[0], 0)

    @pl.when(eb_ref[expert + 1] > eb_ref[expert])
    def _():
        w1b[...] = w1_ref[0].astype(BF16)
        w2b[...] = w2_ref[0].astype(BF16)

    def run_block(b, carry):
        slot = b % 2
        prev, nv_prev, nv_prev2 = neighbours(b)
        nxt = jnp.minimum(b + 1, last_entry)
        write_back(prev, nv_prev, 1 - slot, ssem.at[slot])
        fetch(nxt, nv_ref[nxt], 1 - slot)
        wait_rows(nv_ref[b], gsem.at[slot])
        x = _from_row_tiles(xbuf.at[slot], rows)
        hid = _dot(x, w1b[...]) + b1_ref[0]
        x_glu = jnp.minimum(hid[:, 0:ff], SWIGLU_LIMIT)
        x_lin = jnp.clip(hid[:, ff:2 * ff], -SWIGLU_LIMIT, SWIGLU_LIMIT)
        act = x_glu * _sigmoid(SWIGLU_ALPHA * x_glu) * (x_lin + 1.0)
        y = _dot(act.astype(BF16), w2b[...]) + b2_ref[0]
        wait_rows(nv_prev2, ssem.at[1 - slot])
        _to_row_tiles(ybuf.at[slot], y.astype(BF16).astype(F32))
        return carry

    lax.fori_loop(eb_ref[expert], eb_ref[expert + 1], run_block, 0)

    @pl.when(expert == pl.num_programs(0) - 1)
    def _():
        slot = n_used % 2
        prev, nv_prev, nv_prev2 = neighbours(n_used)
        write_back(prev, nv_prev, 1 - slot, ssem.at[slot])
        wait_rows(nv_prev2, ssem.at[1 - slot])
        wait_rows(nv_prev, ssem.at[slot])


def _experts(expert_block, block_e, block_r0, block_t0, block_nv, cum, off, xs, w1, b1, w2, b2, *,
             rows, tiles, tile_rows):
    _, d, ff2 = w1.shape
    ff = ff2 // 2
    weights = lambda shape: pl.BlockSpec(shape, lambda e, *_: (e, 0, 0))
    grid_spec = pltpu.PrefetchScalarGridSpec(
        num_scalar_prefetch=7,
        grid=(N_EXPERTS,),
        in_specs=[pl.BlockSpec(memory_space=pl.ANY),
                  weights((1, d, ff2)), weights((1, 1, ff2)), weights((1, ff, d)), weights((1, 1, d))],
        out_specs=pl.BlockSpec(memory_space=pl.ANY),
        scratch_shapes=[pltpu.VMEM((2, rows * ROW_WORDS, LANES), I32),
                        pltpu.VMEM((2, rows * ROW_WORDS, LANES), I32),
                        pltpu.VMEM((d, ff2), BF16),
                        pltpu.VMEM((ff, d), BF16),
                        pltpu.SemaphoreType.DMA((2,)),
                        pltpu.SemaphoreType.DMA((2,))],
    )
    return pl.pallas_call(
        functools.partial(_experts_body, tiles=tiles, tile_rows=tile_rows),
        grid_spec=grid_spec,
        out_shape=jax.ShapeDtypeStruct(xs.shape, I32),
        input_output_aliases={7: 0},
        compiler_params=_params("arbitrary"),
        name="experts",
    )(expert_block, block_e, block_r0, block_t0, block_nv, cum, off, xs, w1,
      b1.reshape(N_EXPERTS, 1, ff2), w2, b2.reshape(N_EXPERTS, 1, d))


def _combine_body(pos_ref, gate_ref, h1_ref, g_ref, ys_ref, out_ref):
    tm = h1_ref.shape[0]
    cap = ys_ref.shape[0] // ROW_WORDS
    p_iota = lax.broadcasted_iota(I32, (tm, cap), 1)
    pos = pos_ref[...]
    gate = gate_ref[...]
    g_mat = jnp.zeros((tm, cap), F32)
    for k in range(TOP_K):
        g_mat = jnp.where(p_iota == pos[:, k:k + 1], gate[:, k:k + 1], g_mat)
    ys = _from_row_tiles(ys_ref, cap)
    out_ref[...] = _rms(h1_ref[...] + _dot(g_mat.astype(BF16), ys), g_ref[...])


def _combine(pos, gate, h1, g, ys, *, tm):
    n, d = h1.shape
    return pl.pallas_call(
        _combine_body,
        grid=(n // tm,),
        in_specs=[pl.BlockSpec((tm, TOP_K), lambda i: (i, 0)),
                  pl.BlockSpec((tm, TOP_K), lambda i: (i, 0)),
                  pl.BlockSpec((tm, d), lambda i: (i, 0)),
                  _resident((1, d)),
                  pl.BlockSpec((_sorted_rows(tm) * ROW_WORDS, LANES), lambda i: (i, 0))],
        out_specs=pl.BlockSpec((tm, d), lambda i: (i, 0)),
        out_shape=jax.ShapeDtypeStruct((n, d), F32),
        compiler_params=_params("parallel"),
        name="combine",
    )(pos, gate, h1, g, ys)


def _block_tables(cnt, rows):
    cum = jnp.concatenate([jnp.zeros((1, N_EXPERTS), I32), jnp.cumsum(cnt, axis=0)])
    off = jnp.cumsum(cnt, axis=1) - cnt
    total = cum[-1]
    blocks = (total + rows - 1) // rows
    block_end = jnp.cumsum(blocks)
    return cum, off, total, blocks, block_end


def _layer(h2, bsz, seq, norm_mix_g, w_in, gate_bias, conv_w, conv_b, w_rg_a, b_rg_a, w_rg_x, b_rg_x,
           lru_lambda, w_sb_o, w_lru_o, w_out, norm_ffn_g, w_router, b_router, w1, b1, w2, b2, out_g):
    n, d = h2.shape
    tm = min(ROW_TILE, n)
    tiles = n // tm
    group = MXU_DIM // (d // LRU_BLOCKS)
    qkv, gg, hg = _inproj(h2.reshape(bsz, seq, d), norm_mix_g.reshape(1, d), w_in.astype(BF16),
                          conv_w, conv_b.reshape(1, d),
                          _block_diag(w_rg_a, group).astype(BF16), b_rg_a.reshape(1, d),
                          _block_diag(w_rg_x, group).astype(BF16), b_rg_x.reshape(1, d),
                          lru_lambda.reshape(1, d), tt=min(INPROJ_TIME_BLOCK, seq))
    o_sb = _attention(qkv, bsz=bsz, seq=seq, tq=min(ATTN_BLOCK, seq)).reshape(n, SB_WIDTH)
    h1, xs, pos, gate, cnt = _merge(
        h2, o_sb, hg.reshape(n, d), gg.reshape(n, 2 * d), gate_bias.reshape(1, 2 * d),
        w_sb_o.astype(BF16), w_lru_o.astype(BF16),
        w_out.astype(BF16), norm_ffn_g.reshape(1, d), w_router.T.astype(BF16),
        b_router.reshape(N_EXPERTS, 1), tm=tm)

    rows = MOE_ROWS
    n_steps = (n * TOP_K + tiles * N_EXPERTS + N_EXPERTS * (rows - 1)) // rows + 1
    cnt = cnt.reshape(tiles, N_EXPERTS).astype(I32)
    cum, off, total, blocks, block_end = _block_tables(cnt, rows)
    step = jnp.arange(n_steps, dtype=I32)
    block_e = jnp.minimum(jnp.sum((step[:, None] >= block_end[None, :]).astype(I32), axis=1), N_EXPERTS - 1)
    block_r0 = (step - (block_end - blocks)[block_e]) * rows
    block_nv = jnp.clip(total[block_e] - block_r0, 0, rows)
    block_nv = jnp.where(step < block_end[-1], block_nv, 0)
    run_ends = jnp.sum(jnp.where(block_e[:, None, None] == jnp.arange(N_EXPERTS, dtype=I32)[None, None, :],
                                 cum[None, 1:, :], 0), axis=2)
    block_t0 = jnp.sum((run_ends <= block_r0[:, None]).astype(I32), axis=1)
    block_t0 = jnp.minimum(block_t0, tiles - 1)
    expert_block = jnp.concatenate([jnp.zeros((1,), I32), block_end.astype(I32)])

    ys = _experts(expert_block, block_e, block_r0, block_t0, block_nv, cum.reshape(-1), off.reshape(-1),
                  xs, w1, b1, w2, b2, rows=rows, tiles=tiles, tile_rows=_sorted_rows(tm))
    return _combine(pos.T, gate.T, h1, out_g.reshape(1, d), ys, tm=tm)


def kernel(x, norm_mix_g, w_in, gate_bias, conv_w, conv_b, w_rg_a, b_rg_a, w_rg_x, b_rg_x, lru_lambda,
           w_sb_o, w_lru_o, w_out, norm_ffn_g, w_router, b_router, w1, b1, w2, b2, norm_final_g):
    bsz, seq, d = x.shape
    depth = w_in.shape[0]
    assert depth == 1, "the final norm is fused into the last layer's combine"
    assert bsz % SUBLANES == 0 and d % MXU_DIM == 0
    h2 = x.reshape(bsz * seq, d)
    out = _layer(h2, bsz, seq, norm_mix_g[0], w_in[0], gate_bias[0], conv_w[0], conv_b[0], w_rg_a[0],
                 b_rg_a[0], w_rg_x[0], b_rg_x[0], lru_lambda[0], w_sb_o[0], w_lru_o[0], w_out[0],
                 norm_ffn_g[0], w_router[0], b_router[0], w1[0], b1[0], w2[0], b2[0], norm_final_g)
    return out.reshape(bsz, seq, d)
```

```python
import functools

import jax
import jax.numpy as jnp
from jax import lax
from jax.experimental import pallas as pl
from jax.experimental.pallas import tpu as pltpu

F32 = jnp.float32
BF16 = jnp.bfloat16
I32 = jnp.int32

SB_HEADS = 8
SB_HEAD_DIM = 64
SB_WIDTH = SB_HEADS * SB_HEAD_DIM
LRU_BLOCKS = 16
CONV_WIDTH = 4
RG_C = 8.0
N_EXPERTS = 32
TOP_K = 4
SWIGLU_ALPHA = 1.702
SWIGLU_LIMIT = 7.0
NORM_EPS = 1e-6

LANES = 128
SUBLANES = 8
MXU_DIM = 256
VMEM_LIMIT = 56 * 1024 * 1024

ROW_TILE = 512
ATTN_PAIRS = 4
ATTN_BLOCK = 256
ATTN_DEAD_CARRY = 104.0
INPROJ_TIME_BLOCK = 64
MOE_ROWS = 512


def _rms(x, g):
    return x * lax.rsqrt(jnp.mean(x * x, axis=-1, keepdims=True) + NORM_EPS) * g


def _sigmoid_t(x):
    return 0.5 * jnp.tanh(0.5 * x) + 0.5


def _softplus(x):
    return jnp.maximum(x, 0.0) + jnp.log(1.0 + jnp.exp(-jnp.abs(x)))


def _dot(a, b):
    return jnp.dot(a, b, preferred_element_type=F32)


def _params(*sem):
    return pltpu.CompilerParams(dimension_semantics=sem, vmem_limit_bytes=VMEM_LIMIT)


def _resident(shape):
    return pl.BlockSpec(shape, lambda *_: (0,) * len(shape), pipeline_mode=pl.Buffered(1))


def _inproj_body(x_ref, g_ref, w_ref, cw_ref, cb_ref, wa_ref, ba_ref, wx_ref, bx_ref, lam_ref,
                 qkv_ref, gg_ref, hg_ref, xpad, y_s, a_s, b_s, h_s, *, chunk):
    s = pl.program_id(1)
    nb, tt, d = x_ref.shape
    chunks = d // LANES
    n_rows = tt * nb
    halo = (CONV_WIDTH - 1) * nb
    u = _rms(x_ref[...].reshape(n_rows, d), g_ref[...]).astype(BF16)

    def project(col, width, store):
        for c in range(0, width, chunk):
            store(c, _dot(u, w_ref[:, col + c:col + c + chunk]))

    @pl.when(s == 0)
    def _():
        xpad[:, 0:halo, :] = jnp.zeros((chunks, halo, LANES), F32)
        h_s[...] = jnp.zeros_like(h_s)

    @pl.when(s > 0)
    def _():
        xpad[:, 0:halo, :] = xpad[:, n_rows:n_rows + halo, :]

    def store_x(c, val):
        for j in range(chunk // LANES):
            for b in range(nb):
                xpad[c // LANES + j, pl.ds(halo + b, tt, stride=nb), :] = (
                    val[b * tt:(b + 1) * tt, j * LANES:(j + 1) * LANES])

    def store_y(c, val):
        y_s[:, c:c + chunk] = val

    def store_to(ref):
        def store(c, val):
            ref[:, :, c:c + chunk] = val.astype(ref.dtype).reshape(nb, tt, chunk)
        return store

    x_col = qkv_ref.shape[2]
    project(x_col, d, store_x)
    todo = ([(x_col + d + c, store_y, c) for c in range(0, d, chunk)]
            + [(c, store_to(qkv_ref), c) for c in range(0, x_col, chunk)]
            + [(x_col + 2 * d + c, store_to(gg_ref), c) for c in range(0, 2 * d, chunk)])

    def project_next(count=1):
        for _ in range(count):
            if todo:
                col, store, c = todo.pop(0)
                store(c, _dot(u, w_ref[:, col:col + chunk]))

    def conv(j):
        lanes = slice(j * LANES, (j + 1) * LANES)
        xc = jnp.zeros((n_rows, LANES), F32) + cb_ref[:, lanes]
        for i in range(CONV_WIDTH):
            xc = xc + cw_ref[i:i + 1, lanes] * xpad[j, i * nb:i * nb + n_rows, :]
        return xc

    neg_sp_lam = -RG_C * _softplus(-lam_ref[...])
    per_group = MXU_DIM // LANES
    for g in range(chunks // per_group):
        project_next(2)
        sl = slice(g * MXU_DIM, (g + 1) * MXU_DIM)
        xc = jnp.concatenate([conv(g * per_group + j) for j in range(per_group)], axis=1)
        xb = xc.astype(BF16)
        r = _sigmoid_t(_dot(xb, wa_ref[g]) + ba_ref[:, sl])
        gate_i = _sigmoid_t(_dot(xb, wx_ref[g]) + bx_ref[:, sl])
        a = jnp.exp(r * neg_sp_lam[:, sl])
        var = (1.0 - a) * (1.0 + a)
        mult = jnp.where(var > 0.0, var * lax.rsqrt(var), 0.0)
        b = mult * (gate_i * xc)
        for j in range(per_group):
            a_s[g * per_group + j] = a[:, j * LANES:(j + 1) * LANES]
            b_s[g * per_group + j] = b[:, j * LANES:(j + 1) * LANES]

    def step(t, hs):
        rows = pl.ds(pl.multiple_of(t * nb, nb), nb)
        out = []
        for j in range(chunks):
            h = a_s[j, rows, :] * hs[j] + b_s[j, rows, :]
            b_s[j, rows, :] = h
            out.append(h)
        return tuple(out)

    hs = lax.fori_loop(0, tt, step, tuple(h_s[j] for j in range(chunks)))
    for j in range(chunks):
        h_s[j] = hs[j]

    for j in range(chunks):
        project_next(len(todo) if j == chunks - 1 else 1)
        for bi in range(nb):
            y = y_s[bi * tt:(bi + 1) * tt, j * LANES:(j + 1) * LANES]
            gelu = 0.5 * y * (1.0 + jnp.tanh(0.7978845608028654 * (y + 0.044715 * (y * y * y))))
            h = b_s[j, pl.ds(bi, tt, stride=nb), :]
            hg_ref[bi, :, j * LANES:(j + 1) * LANES] = (h * gelu).astype(hg_ref.dtype)


def _inproj(x3, g, w_in_bf, conv_w, conv_b, wa_bd, b_a, wx_bd, b_x, lam, *, tt):
    bsz, seq, d = x3.shape
    nb = SUBLANES
    chunks = d // LANES
    tile = lambda w: pl.BlockSpec((nb, tt, w), lambda b, s: (b, s, 0))
    return pl.pallas_call(
        functools.partial(_inproj_body, chunk=MXU_DIM),
        grid=(bsz // nb, seq // tt),
        in_specs=[tile(d), _resident((1, d)), _resident(w_in_bf.shape),
                  _resident(conv_w.shape), _resident((1, d)),
                  _resident(wa_bd.shape), _resident((1, d)),
                  _resident(wx_bd.shape), _resident((1, d)),
                  _resident((1, d))],
        out_specs=[tile(3 * SB_WIDTH), tile(2 * d), tile(d)],
        out_shape=[jax.ShapeDtypeStruct((bsz, seq, 3 * SB_WIDTH), BF16),
                   jax.ShapeDtypeStruct((bsz, seq, 2 * d), F32),
                   jax.ShapeDtypeStruct((bsz, seq, d), BF16)],
        scratch_shapes=[pltpu.VMEM((chunks, (tt + CONV_WIDTH - 1) * nb, LANES), F32),
                        pltpu.VMEM((nb * tt, d), F32),
                        pltpu.VMEM((chunks, tt * nb, LANES), F32),
                        pltpu.VMEM((chunks, tt * nb, LANES), F32),
                        pltpu.VMEM((chunks, nb, LANES), F32)],
        compiler_params=_params("parallel", "arbitrary"),
        name="inproj",
    )(x3, g, w_in_bf, conv_w, conv_b, wa_bd, b_a, wx_bd, b_x, lam)


def _attn_body(q_ref, k_ref, v_ref, o_ref, o_acc, c_acc, *, tq, npair):
    qi = pl.program_id(2)
    lane = lax.broadcasted_iota(I32, (1, LANES), 1)
    first = lane < SB_HEAD_DIM
    q_pairs = []
    for p in range(npair):
        q = q_ref[0, :, p * LANES:(p + 1) * LANES] * (SB_HEAD_DIM ** -0.5)
        zq = jnp.zeros_like(q)
        q_pairs.append(jnp.concatenate([jnp.where(first, q, zq), jnp.where(first, zq, q)], axis=0))
    row = lax.broadcasted_iota(I32, (2 * tq, tq), 0)
    row = jnp.where(row >= tq, row - tq, row)
    col = lax.broadcasted_iota(I32, (2 * tq, tq), 1)
    causal = col < row
    tri2 = jnp.where(row > col, 1.0, 0.0).astype(BF16)

    def block(j, diag):
        start = pl.multiple_of(j * tq, tq)
        zs, sps, sufs = [], [], []
        for p in range(npair):
            kj = k_ref[0, pl.ds(start, tq), p * LANES:(p + 1) * LANES]
            zs.append(lax.dot_general(q_pairs[p], kj, (((1,), (1,)), ((), ())), preferred_element_type=F32))
        for p in range(npair):
            sp = _softplus(zs[p])
            if diag:
                sp = jnp.where(causal, sp, 0.0)
            sps.append(sp)
            sp_hi = sp.astype(BF16)
            sp_lo = (sp - sp_hi.astype(F32)).astype(BF16)
            sufs.append(_dot(jnp.concatenate([sp_hi, sp_lo], axis=1), tri2))
        low = None
        for p in range(npair):
            vj = v_ref[0, pl.ds(start, tq), p * LANES:(p + 1) * LANES]
            zv = jnp.zeros_like(vj)
            v2 = jnp.concatenate([jnp.where(first, vj, zv), jnp.where(first, zv, vj)], axis=0)
            arg = zs[p] - sps[p] - sufs[p]
            if not diag:
                arg = arg - c_acc[p]
            att = jnp.exp(arg)
            if diag:
                att = jnp.where(causal, att, 0.0)
            att = att.astype(BF16)
            pv = _dot(jnp.concatenate([att[0:tq], att[tq:2 * tq]], axis=1), v2)
            if not diag:
                pv = pv + o_acc[:, p * LANES:(p + 1) * LANES]
            o_acc[:, p * LANES:(p + 1) * LANES] = pv
            c = jnp.sum(sps[p], axis=-1, keepdims=True)
            if not diag:
                c = c + c_acc[p]
            c_acc[p] = c
            m = jnp.min(c)
            low = m if low is None else jnp.minimum(low, m)
        return low

    def more(state):
        j, low = state
        return jnp.logical_and(j >= 0, low < ATTN_DEAD_CARRY)

    def step(state):
        j, _ = state
        return j - 1, block(j, False)

    lax.while_loop(more, step, (qi - 1, block(qi, True)))
    o_ref[0] = o_acc[...].astype(o_ref.dtype)


def _attention(qkv, *, bsz, seq, tq, npair=ATTN_PAIRS):
    qkv3 = qkv.reshape(bsz, seq, 3 * SB_WIDTH)
    width = npair * LANES
    groups = SB_WIDTH // width
    return pl.pallas_call(
        functools.partial(_attn_body, tq=tq, npair=npair),
        grid=(bsz, groups, seq // tq),
        in_specs=[pl.BlockSpec((1, tq, width), lambda b, p, i: (b, i, p)),
                  pl.BlockSpec((1, seq, width), lambda b, p, i: (b, 0, groups + p)),
                  pl.BlockSpec((1, seq, width), lambda b, p, i: (b, 0, 2 * groups + p))],
        out_specs=pl.BlockSpec((1, tq, width), lambda b, p, i: (b, i, p)),
        out_shape=jax.ShapeDtypeStruct((bsz, seq, SB_WIDTH), BF16),
        scratch_shapes=[pltpu.VMEM((tq, width), F32), pltpu.VMEM((npair, 2 * tq, 1), F32)],
        compiler_params=_params("parallel", "parallel", "arbitrary"),
        name="attn",
    )(qkv3, qkv3, qkv3)


def _block_diag(w, group):
    n, k, _ = w.shape
    w = w.reshape(n // group, group, k, k)
    eye = jnp.eye(group, dtype=w.dtype)
    return jnp.einsum('gakl,ab->gakbl', w, eye).reshape(n // group, group * k, group * k)


ROW_WORDS = 4


def _sorted_rows(tm):
    cap = -(-(TOP_K * tm + N_EXPERTS) // LANES) * LANES
    assert (cap // TOP_K) % 16 == 0
    return cap


def _to_row_tiles(ref, value, first_row=0):
    rows, d = value.shape
    bits = lax.bitcast_convert_type(value, I32)
    for s in range(ROW_WORDS):
        low = lax.shift_right_logical(bits[:, s * LANES:(s + 1) * LANES], 16)
        high = bits[:, d // 2 + s * LANES:d // 2 + (s + 1) * LANES]
        ref[pl.ds(first_row * ROW_WORDS + s, rows, stride=ROW_WORDS), :] = high | low


def _from_row_tiles(ref, rows):
    words = [ref[pl.ds(s, rows, stride=ROW_WORDS), :] for s in range(ROW_WORDS)]
    low = [lax.bitcast_convert_type(lax.shift_left(w, 16), F32).astype(BF16) for w in words]
    high = [lax.bitcast_convert_type(w & jnp.int32(-65536), F32).astype(BF16) for w in words]
    return jnp.concatenate(low + high, axis=1)


def _merge_body(x_ref, osb_ref, hg_ref, gg_ref, gbias_ref, wsb_ref, wlru_ref, wout_ref, g2_ref,
                wrt_ref, br_ref, h1_ref, xs_ref, pos_ref, gate_ref, cnt_ref):
    tm, d = x_ref.shape

    y_sb = _dot(osb_ref[...], wsb_ref[...])
    y_lru = _dot(hg_ref[...], wlru_ref[...])
    g_a = _sigmoid_t(gg_ref[:, 0:d] + gbias_ref[:, 0:d])
    g_b = _sigmoid_t(gg_ref[:, d:2 * d] + gbias_ref[:, d:2 * d])
    merged = (g_a * y_sb + g_b * y_lru).astype(BF16)
    h1 = x_ref[...] + _dot(merged, wout_ref[...])
    h1_ref[...] = h1
    u2 = _rms(h1, g2_ref[...]).astype(BF16)
    logits = lax.dot_general(wrt_ref[...], u2, (((1,), (1,)), ((), ())),
                             preferred_element_type=F32) + br_ref[...]

    e_iota = lax.broadcasted_iota(I32, (N_EXPERTS, tm), 0)
    k_iota = lax.broadcasted_iota(I32, (TOP_K, tm), 0)
    vals = logits
    chosen = jnp.zeros((N_EXPERTS, tm), F32)
    top_v, hots = [], []
    for _ in range(TOP_K):
        m = jnp.max(vals, axis=0, keepdims=True)
        sel = jnp.min(jnp.where(vals == m, e_iota, N_EXPERTS), axis=0, keepdims=True)
        hot = e_iota == sel
        top_v.append(m)
        hots.append(hot)
        chosen = jnp.where(hot, 1.0, chosen)
        vals = jnp.where(hot, -jnp.inf, vals)
    exps = [jnp.exp(v - top_v[0]) for v in top_v]
    denom = exps[0] + exps[1] + exps[2] + exps[3]

    chosen_b = chosen.astype(BF16)
    t_row = lax.broadcasted_iota(I32, (tm, tm), 0)
    t_col = lax.broadcasted_iota(I32, (tm, tm), 1)
    earlier = jnp.where(t_row < t_col, 1.0, 0.0).astype(BF16)
    same_before = _dot(chosen_b, earlier)
    e_row = lax.broadcasted_iota(I32, (N_EXPERTS, N_EXPERTS), 0)
    e_col = lax.broadcasted_iota(I32, (N_EXPERTS, N_EXPERTS), 1)
    smaller = jnp.where(e_col < e_row, 1.0, 0.0).astype(BF16)
    cnt = jnp.sum(chosen, axis=1, keepdims=True)
    odd = cnt - 2.0 * jnp.floor(0.5 * cnt)
    cnt_ref[...] = cnt + odd
    odd_smaller = _dot(smaller, jnp.broadcast_to(odd, (N_EXPERTS, LANES)).astype(BF16))[:, 0:1]
    first = jnp.sum(_dot(smaller, chosen_b), axis=1, keepdims=True) + odd_smaller

    where_to = first + same_before
    pos = jnp.zeros((TOP_K, tm), I32)
    gate = jnp.zeros((TOP_K, tm), F32)
    pos_k = []
    for k in range(TOP_K):
        pk = jnp.sum(jnp.where(hots[k], where_to, 0.0), axis=0, keepdims=True).astype(I32)
        pos_k.append(pk)
        pos = jnp.where(k_iota == k, pk, pos)
        gate = jnp.where(k_iota == k, exps[k] / denom, gate)
    pos_ref[...] = pos
    gate_ref[...] = gate

    cap = xs_ref.shape[0] // ROW_WORDS
    piece = cap // TOP_K
    for c in range(TOP_K):
        p_iota = lax.broadcasted_iota(I32, (piece, tm), 0) + c * piece
        hit = p_iota == pos_k[0]
        for k in range(1, TOP_K):
            hit = jnp.logical_or(hit, p_iota == pos_k[k])
        perm = jnp.where(hit, 1.0, 0.0).astype(BF16)
        _to_row_tiles(xs_ref, _dot(perm, u2), first_row=c * piece)


def _merge(x2, o_sb, hg, gg, gate_bias, wsb, wlru, wout, g2, wrt, br, *, tm):
    n, d = x2.shape
    tiles = n // tm
    assert d == 2 * ROW_WORDS * LANES, "row tile form packs 1024 columns into 4 x 128 words"
    cap = _sorted_rows(tm)
    row = lambda w: pl.BlockSpec((tm, w), lambda i: (i, 0))
    return pl.pallas_call(
        _merge_body,
        grid=(tiles,),
        in_specs=[row(d), row(SB_WIDTH), row(d), row(2 * d), _resident((1, 2 * d)),
                  _resident(wsb.shape), _resident(wlru.shape), _resident(wout.shape),
                  _resident((1, d)), _resident(wrt.shape), _resident((N_EXPERTS, 1))],
        out_specs=[row(d),
                   pl.BlockSpec((cap * ROW_WORDS, LANES), lambda i: (i, 0)),
                   pl.BlockSpec((TOP_K, tm), lambda i: (0, i)),
                   pl.BlockSpec((TOP_K, tm), lambda i: (0, i)),
                   pl.BlockSpec((N_EXPERTS, 1), lambda i: (i, 0))],
        out_shape=[jax.ShapeDtypeStruct((n, d), F32),
                   jax.ShapeDtypeStruct((tiles * cap * ROW_WORDS, LANES), I32),
                   jax.ShapeDtypeStruct((TOP_K, n), I32),
                   jax.ShapeDtypeStruct((TOP_K, n), F32),
                   jax.ShapeDtypeStruct((tiles * N_EXPERTS, 1), F32)],
        compiler_params=_params("parallel"),
        name="merge",
    )(x2, o_sb, hg, gg, gate_bias, wsb, wlru, wout, g2, wrt, br)


def _experts_body(eb_ref, be_ref, r0_ref, t0_ref, nv_ref, cum_ref, off_ref,
                  xs_hbm, w1_ref, b1_ref, w2_ref, b2_ref, ys_hbm,
                  xbuf, ybuf, w1b, w2b, gsem, ssem, *, tiles, tile_rows):
    expert = pl.program_id(0)
    rows = xbuf.shape[1] // ROW_WORDS
    ff = w2_ref.shape[1]
    n_used = eb_ref[N_EXPERTS]
    last_entry = be_ref.shape[0] - 1

    def for_runs(b, nv, fn):
        e = be_ref[b]
        r0 = r0_ref[b]
        end = r0 + nv

        def more(t):
            return jnp.logical_and(t < tiles, cum_ref[jnp.minimum(t, tiles) * N_EXPERTS + e] < end)

        def run(t):
            c0 = cum_ref[t * N_EXPERTS + e]
            c1 = cum_ref[(t + 1) * N_EXPERTS + e]
            lo = jnp.maximum(c0, r0)
            n_rows = jnp.minimum(c1, end) - lo

            @pl.when(n_rows > 0)
            def _():
                src = t * tile_rows + off_ref[t * N_EXPERTS + e] + (lo - c0)
                fn(pl.multiple_of(src * ROW_WORDS, SUBLANES),
                   pl.multiple_of((lo - r0) * ROW_WORDS, SUBLANES), n_rows * ROW_WORDS)

            return t + 1

        @pl.when(nv > 0)
        def _():
            lax.while_loop(more, run, t0_ref[b])

    def fetch(b, nv, s):
        def one(src, dst, size):
            pltpu.make_async_copy(xs_hbm.at[pl.ds(src, size)], xbuf.at[s, pl.ds(dst, size)],
                                  gsem.at[s]).start()
        for_runs(b, nv, one)

    def write_back(b, nv, s, sem):
        def one(src, dst, size):
            pltpu.make_async_copy(ybuf.at[s, pl.ds(dst, size)], ys_hbm.at[pl.ds(src, size)], sem).start()
        for_runs(b, nv, one)

    def wait_rows(nv, sem):
        @pl.when(nv > 0)
        def _():
            size = nv * ROW_WORDS
            pltpu.make_async_copy(xs_hbm.at[pl.ds(0, size)], xbuf.at[0, pl.ds(0, size)], sem).wait()

    def neighbours(b):
        prev = jnp.maximum(b - 1, 0)
        nv_prev = jnp.where(b >= 1, nv_ref[prev], 0)
        nv_prev2 = jnp.where(b >= 2, nv_ref[jnp.maximum(b - 2, 0)], 0)
        return prev, nv_prev, nv_prev2

    @pl.when(expert == 0)
    def _():
        xbuf[...] = jnp.zeros_like(xbuf)
        fetch(0, nv_ref[0], 0)

    @pl.when(eb_ref[expert + 1] > eb_ref[expert])
    def _():
        w1b[...] = w1_ref[0].astype(BF16)
        w2b[...] = w2_ref[0].astype(BF16)

    def run_block(b, carry):
        slot = b % 2
        prev, nv_prev, nv_prev2 = neighbours(b)
        nxt = jnp.minimum(b + 1, last_entry)
        write_back(prev, nv_prev, 1 - slot, ssem.at[slot])
        fetch(nxt, nv_ref[nxt], 1 - slot)
        wait_rows(nv_ref[b], gsem.at[slot])
        x = _from_row_tiles(xbuf.at[slot], rows)
        hid = _dot(x, w1b[...]) + b1_ref[0]
        x_glu = jnp.minimum(hid[:, 0:ff], SWIGLU_LIMIT)
        x_lin = jnp.clip(hid[:, ff:2 * ff], -SWIGLU_LIMIT, SWIGLU_LIMIT)
        act = x_glu * _sigmoid_t(SWIGLU_ALPHA * x_glu) * (x_lin + 1.0)
        y = _dot(act.astype(BF16), w2b[...]) + b2_ref[0]
        wait_rows(nv_prev2, ssem.at[1 - slot])
        _to_row_tiles(ybuf.at[slot], y.astype(BF16).astype(F32))
        return carry

    lax.fori_loop(eb_ref[expert], eb_ref[expert + 1], run_block, 0)

    @pl.when(expert == pl.num_programs(0) - 1)
    def _():
        slot = n_used % 2
        prev, nv_prev, nv_prev2 = neighbours(n_used)
        write_back(prev, nv_prev, 1 - slot, ssem.at[slot])
        wait_rows(nv_prev2, ssem.at[1 - slot])
        wait_rows(nv_prev, ssem.at[slot])


def _experts(expert_block, block_e, block_r0, block_t0, block_nv, cum, off, xs, w1, b1, w2, b2, *,
             rows, tiles, tile_rows):
    _, d, ff2 = w1.shape
    ff = ff2 // 2
    weights = lambda shape: pl.BlockSpec(shape, lambda e, *_: (e, 0, 0))
    grid_spec = pltpu.PrefetchScalarGridSpec(
        num_scalar_prefetch=7,
        grid=(N_EXPERTS,),
        in_specs=[pl.BlockSpec(memory_space=pl.ANY),
                  weights((1, d, ff2)), weights((1, 1, ff2)), weights((1, ff, d)), weights((1, 1, d))],
        out_specs=pl.BlockSpec(memory_space=pl.ANY),
        scratch_shapes=[pltpu.VMEM((2, rows * ROW_WORDS, LANES), I32),
                        pltpu.VMEM((2, rows * ROW_WORDS, LANES), I32),
                        pltpu.VMEM((d, ff2), BF16),
                        pltpu.VMEM((ff, d), BF16),
                        pltpu.SemaphoreType.DMA((2,)),
                        pltpu.SemaphoreType.DMA((2,))],
    )
    return pl.pallas_call(
        functools.partial(_experts_body, tiles=tiles, tile_rows=tile_rows),
        grid_spec=grid_spec,
        out_shape=jax.ShapeDtypeStruct(xs.shape, I32),
        input_output_aliases={7: 0},
        compiler_params=_params("arbitrary"),
        name="experts",
    )(expert_block, block_e, block_r0, block_t0, block_nv, cum, off, xs, w1,
      b1.reshape(N_EXPERTS, 1, ff2), w2, b2.reshape(N_EXPERTS, 1, d))


def _combine_body(pos_ref, gate_ref, h1_ref, g_ref, ys_ref, out_ref):
    tm = h1_ref.shape[0]
    cap = ys_ref.shape[0] // ROW_WORDS
    p_iota = lax.broadcasted_iota(I32, (tm, cap), 1)
    pos = pos_ref[...]
    gate = gate_ref[...]
    g_mat = jnp.zeros((tm, cap), F32)
    for k in range(TOP_K):
        g_mat = jnp.where(p_iota == pos[:, k:k + 1], gate[:, k:k + 1], g_mat)
    ys = _from_row_tiles(ys_ref, cap)
    out_ref[...] = _rms(h1_ref[...] + _dot(g_mat.astype(BF16), ys), g_ref[...])


def _combine(pos, gate, h1, g, ys, *, tm):
    n, d = h1.shape
    return pl.pallas_call(
        _combine_body,
        grid=(n // tm,),
        in_specs=[pl.BlockSpec((tm, TOP_K), lambda i: (i, 0)),
                  pl.BlockSpec((tm, TOP_K), lambda i: (i, 0)),
                  pl.BlockSpec((tm, d), lambda i: (i, 0)),
                  _resident((1, d)),
                  pl.BlockSpec((_sorted_rows(tm) * ROW_WORDS, LANES), lambda i: (i, 0))],
        out_specs=pl.BlockSpec((tm, d), lambda i: (i, 0)),
        out_shape=jax.ShapeDtypeStruct((n, d), F32),
        compiler_params=_params("parallel"),
        name="combine",
    )(pos, gate, h1, g, ys)


def _block_tables(cnt, rows):
    cum = jnp.concatenate([jnp.zeros((1, N_EXPERTS), I32), jnp.cumsum(cnt, axis=0)])
    off = jnp.cumsum(cnt, axis=1) - cnt
    total = cum[-1]
    blocks = (total + rows - 1) // rows
    block_end = jnp.cumsum(blocks)
    return cum, off, total, blocks, block_end


def _layer(h2, bsz, seq, norm_mix_g, w_in, gate_bias, conv_w, conv_b, w_rg_a, b_rg_a, w_rg_x, b_rg_x,
           lru_lambda, w_sb_o, w_lru_o, w_out, norm_ffn_g, w_router, b_router, w1, b1, w2, b2, out_g):
    n, d = h2.shape
    tm = min(ROW_TILE, n)
    tiles = n // tm
    group = MXU_DIM // (d // LRU_BLOCKS)
    qkv, gg, hg = _inproj(h2.reshape(bsz, seq, d), norm_mix_g.reshape(1, d), w_in.astype(BF16),
                          conv_w, conv_b.reshape(1, d),
                          _block_diag(w_rg_a, group).astype(BF16), b_rg_a.reshape(1, d),
                          _block_diag(w_rg_x, group).astype(BF16), b_rg_x.reshape(1, d),
                          lru_lambda.reshape(1, d), tt=min(INPROJ_TIME_BLOCK, seq))
    o_sb = _attention(qkv, bsz=bsz, seq=seq, tq=min(ATTN_BLOCK, seq)).reshape(n, SB_WIDTH)
    h1, xs, pos, gate, cnt = _merge(
        h2, o_sb, hg.reshape(n, d), gg.reshape(n, 2 * d), gate_bias.reshape(1, 2 * d),
        w_sb_o.astype(BF16), w_lru_o.astype(BF16),
        w_out.astype(BF16), norm_ffn_g.reshape(1, d), w_router.T.astype(BF16),
        b_router.reshape(N_EXPERTS, 1), tm=tm)

    rows = MOE_ROWS
    n_steps = (n * TOP_K + tiles * N_EXPERTS + N_EXPERTS * (rows - 1)) // rows + 1
    cnt = cnt.reshape(tiles, N_EXPERTS).astype(I32)
    cum, off, total, blocks, block_end = _block_tables(cnt, rows)
    step = jnp.arange(n_steps, dtype=I32)
    block_e = jnp.minimum(jnp.sum((step[:, None] >= block_end[None, :]).astype(I32), axis=1), N_EXPERTS - 1)
    block_r0 = (step - (block_end - blocks)[block_e]) * rows
    block_nv = jnp.clip(total[block_e] - block_r0, 0, rows)
    block_nv = jnp.where(step < block_end[-1], block_nv, 0)
    run_ends = jnp.sum(jnp.where(block_e[:, None, None] == jnp.arange(N_EXPERTS, dtype=I32)[None, None, :],
                                 cum[None, 1:, :], 0), axis=2)
    block_t0 = jnp.sum((run_ends <= block_r0[:, None]).astype(I32), axis=1)
    block_t0 = jnp.minimum(block_t0, tiles - 1)
    expert_block = jnp.concatenate([jnp.zeros((1,), I32), block_end.astype(I32)])

    ys = _experts(expert_block, block_e, block_r0, block_t0, block_nv, cum.reshape(-1), off.reshape(-1),
                  xs, w1, b1, w2, b2, rows=rows, tiles=tiles, tile_rows=_sorted_rows(tm))
    return _combine(pos.T, gate.T, h1, out_g.reshape(1, d), ys, tm=tm)


def kernel(x, norm_mix_g, w_in, gate_bias, conv_w, conv_b, w_rg_a, b_rg_a, w_rg_x, b_rg_x, lru_lambda,
           w_sb_o, w_lru_o, w_out, norm_ffn_g, w_router, b_router, w1, b1, w2, b2, norm_final_g):
    bsz, seq, d = x.shape
    depth = w_in.shape[0]
    assert depth == 1, "the final norm is fused into the last layer's combine"
    assert bsz % SUBLANES == 0 and d % MXU_DIM == 0
    h2 = x.reshape(bsz * seq, d)
    out = _layer(h2, bsz, seq, norm_mix_g[0], w_in[0], gate_bias[0], conv_w[0], conv_b[0], w_rg_a[0],
                 b_rg_a[0], w_rg_x[0], b_rg_x[0], lru_lambda[0], w_sb_o[0], w_lru_o[0], w_out[0],
                 norm_ffn_g[0], w_router[0], b_router[0], w1[0], b1[0], w2[0], b2[0], norm_final_g)
    return out.reshape(bsz, seq, d)
```

```python
import functools

import jax
import jax.numpy as jnp
from jax import lax
from jax.experimental import pallas as pl
from jax.experimental.pallas import tpu as pltpu

F32 = jnp.float32
BF16 = jnp.bfloat16
I32 = jnp.int32

SB_HEADS = 8
SB_HEAD_DIM = 64
SB_WIDTH = SB_HEADS * SB_HEAD_DIM
LRU_BLOCKS = 16
CONV_WIDTH = 4
RG_C = 8.0
N_EXPERTS = 32
TOP_K = 4
SWIGLU_ALPHA = 1.702
SWIGLU_LIMIT = 7.0
NORM_EPS = 1e-6

LANES = 128
SUBLANES = 8
MXU_DIM = 256
VMEM_LIMIT = 56 * 1024 * 1024

ROW_TILE = 512
ATTN_PAIRS = 4
ATTN_BLOCK = 256
ATTN_DEAD_CARRY = 104.0
INPROJ_TIME_BLOCK = 64
MOE_ROWS = 512


def _rms(x, g):
    return x * lax.rsqrt(jnp.mean(x * x, axis=-1, keepdims=True) + NORM_EPS) * g


def _sigmoid_t(x):
    return 0.5 * jnp.tanh(0.5 * x) + 0.5


def _softplus(x):
    return jnp.maximum(x, 0.0) + jnp.log(1.0 + jnp.exp(-jnp.abs(x)))


def _dot(a, b):
    return jnp.dot(a, b, preferred_element_type=F32)


def _params(*sem):
    return pltpu.CompilerParams(dimension_semantics=sem, vmem_limit_bytes=VMEM_LIMIT)


def _resident(shape):
    return pl.BlockSpec(shape, lambda *_: (0,) * len(shape), pipeline_mode=pl.Buffered(1))


def _inproj_body(x_ref, g_ref, w_ref, cw_ref, cb_ref, wa_ref, ba_ref, wx_ref, bx_ref, lam_ref,
                 qkv_ref, gg_ref, hg_ref, xpad, y_s, a_s, b_s, h_s, *, chunk):
    s = pl.program_id(1)
    nb, tt, d = x_ref.shape
    chunks = d // LANES
    n_rows = tt * nb
    halo = (CONV_WIDTH - 1) * nb
    u = _rms(x_ref[...].reshape(n_rows, d), g_ref[...]).astype(BF16)

    def project(col, width, store):
        for c in range(0, width, chunk):
            store(c, _dot(u, w_ref[:, col + c:col + c + chunk]))

    @pl.when(s == 0)
    def _():
        xpad[:, 0:halo, :] = jnp.zeros((chunks, halo, LANES), F32)
        h_s[...] = jnp.zeros_like(h_s)

    @pl.when(s > 0)
    def _():
        xpad[:, 0:halo, :] = xpad[:, n_rows:n_rows + halo, :]

    def store_x(c, val):
        for j in range(chunk // LANES):
            for b in range(nb):
                xpad[c // LANES + j, pl.ds(halo + b, tt, stride=nb), :] = (
                    val[b * tt:(b + 1) * tt, j * LANES:(j + 1) * LANES])

    def store_y(c, val):
        y_s[:, c:c + chunk] = val

    def store_to(ref):
        def store(c, val):
            ref[:, :, c:c + chunk] = val.astype(ref.dtype).reshape(nb, tt, chunk)
        return store

    x_col = qkv_ref.shape[2]
    project(x_col, d, store_x)
    todo = ([(x_col + d + c, store_y, c) for c in range(0, d, chunk)]
            + [(c, store_to(qkv_ref), c) for c in range(0, x_col, chunk)]
            + [(x_col + 2 * d + c, store_to(gg_ref), c) for c in range(0, 2 * d, chunk)])

    def project_next(count=1):
        for _ in range(count):
            if todo:
                col, store, c = todo.pop(0)
                store(c, _dot(u, w_ref[:, col:col + chunk]))

    def conv(j):
        lanes = slice(j * LANES, (j + 1) * LANES)
        xc = jnp.zeros((n_rows, LANES), F32) + cb_ref[:, lanes]
        for i in range(CONV_WIDTH):
            xc = xc + cw_ref[i:i + 1, lanes] * xpad[j, i * nb:i * nb + n_rows, :]
        return xc

    neg_sp_lam = -RG_C * _softplus(-lam_ref[...])
    per_group = MXU_DIM // LANES
    for g in range(chunks // per_group):
        project_next(2)
        sl = slice(g * MXU_DIM, (g + 1) * MXU_DIM)
        xc = jnp.concatenate([conv(g * per_group + j) for j in range(per_group)], axis=1)
        xb = xc.astype(BF16)
        r = _sigmoid_t(_dot(xb, wa_ref[g]) + ba_ref[:, sl])
        gate_i = _sigmoid_t(_dot(xb, wx_ref[g]) + bx_ref[:, sl])
        a = jnp.exp(r * neg_sp_lam[:, sl])
        var = (1.0 - a) * (1.0 + a)
        mult = jnp.where(var > 0.0, var * lax.rsqrt(var), 0.0)
        b = mult * (gate_i * xc)
        for j in range(per_group):
            a_s[g * per_group + j] = a[:, j * LANES:(j + 1) * LANES]
            b_s[g * per_group + j] = b[:, j * LANES:(j + 1) * LANES]

    def step(t, hs):
        rows = pl.ds(pl.multiple_of(t * nb, nb), nb)
        out = []
        for j in range(chunks):
            h = a_s[j, rows, :] * hs[j] + b_s[j, rows, :]
            b_s[j, rows, :] = h
            out.append(h)
        return tuple(out)

    hs = lax.fori_loop(0, tt, step, tuple(h_s[j] for j in range(chunks)))
    for j in range(chunks):
        h_s[j] = hs[j]

    for j in range(chunks):
        project_next(len(todo) if j == chunks - 1 else 1)
        for bi in range(nb):
            y = y_s[bi * tt:(bi + 1) * tt, j * LANES:(j + 1) * LANES]
            gelu = 0.5 * y * (1.0 + jnp.tanh(0.7978845608028654 * (y + 0.044715 * (y * y * y))))
            h = b_s[j, pl.ds(bi, tt, stride=nb), :]
            hg_ref[bi, :, j * LANES:(j + 1) * LANES] = (h * gelu).astype(hg_ref.dtype)


def _inproj(x3, g, w_in_bf, conv_w, conv_b, wa_bd, b_a, wx_bd, b_x, lam, *, tt):
    bsz, seq, d = x3.shape
    nb = SUBLANES
    chunks = d // LANES
    tile = lambda w: pl.BlockSpec((nb, tt, w), lambda b, s: (b, s, 0))
    return pl.pallas_call(
        functools.partial(_inproj_body, chunk=MXU_DIM),
        grid=(bsz // nb, seq // tt),
        in_specs=[tile(d), _resident((1, d)), _resident(w_in_bf.shape),
                  _resident(conv_w.shape), _resident((1, d)),
                  _resident(wa_bd.shape), _resident((1, d)),
                  _resident(wx_bd.shape), _resident((1, d)),
                  _resident((1, d))],
        out_specs=[tile(3 * SB_WIDTH), tile(2 * d), tile(d)],
        out_shape=[jax.ShapeDtypeStruct((bsz, seq, 3 * SB_WIDTH), BF16),
                   jax.ShapeDtypeStruct((bsz, seq, 2 * d), F32),
                   jax.ShapeDtypeStruct((bsz, seq, d), BF16)],
        scratch_shapes=[pltpu.VMEM((chunks, (tt + CONV_WIDTH - 1) * nb, LANES), F32),
                        pltpu.VMEM((nb * tt, d), F32),
                        pltpu.VMEM((chunks, tt * nb, LANES), F32),
                        pltpu.VMEM((chunks, tt * nb, LANES), F32),
                        pltpu.VMEM((chunks, nb, LANES), F32)],
        compiler_params=_params("parallel", "arbitrary"),
        name="inproj",
    )(x3, g, w_in_bf, conv_w, conv_b, wa_bd, b_a, wx_bd, b_x, lam)


def _attn_body(q_ref, k_ref, v_ref, o_ref, o_acc, c_acc, *, tq, npair):
    qi = pl.program_id(2)
    lane = lax.broadcasted_iota(I32, (1, LANES), 1)
    first = lane < SB_HEAD_DIM
    q_pairs = []
    for p in range(npair):
        q = q_ref[0, :, p * LANES:(p + 1) * LANES] * (SB_HEAD_DIM ** -0.5)
        zq = jnp.zeros_like(q)
        q_pairs.append(jnp.concatenate([jnp.where(first, q, zq), jnp.where(first, zq, q)], axis=0))
    row = lax.broadcasted_iota(I32, (2 * tq, tq), 0)
    row = jnp.where(row >= tq, row - tq, row)
    col = lax.broadcasted_iota(I32, (2 * tq, tq), 1)
    causal = col < row
    tri2 = jnp.where(row > col, 1.0, 0.0).astype(BF16)

    def block(j, diag):
        start = pl.multiple_of(j * tq, tq)
        zs, sps, sufs = [], [], []
        for p in range(npair):
            kj = k_ref[0, pl.ds(start, tq), p * LANES:(p + 1) * LANES]
            zs.append(lax.dot_general(q_pairs[p], kj, (((1,), (1,)), ((), ())), preferred_element_type=F32))
        for p in range(npair):
            sp = _softplus(zs[p])
            if diag:
                sp = jnp.where(causal, sp, 0.0)
            sps.append(sp)
            sp_hi = sp.astype(BF16)
            sp_lo = (sp - sp_hi.astype(F32)).astype(BF16)
            sufs.append(_dot(jnp.concatenate([sp_hi, sp_lo], axis=1), tri2))
        low = None
        for p in range(npair):
            vj = v_ref[0, pl.ds(start, tq), p * LANES:(p + 1) * LANES]
            zv = jnp.zeros_like(vj)
            v2 = jnp.concatenate([jnp.where(first, vj, zv), jnp.where(first, zv, vj)], axis=0)
            arg = zs[p] - sps[p] - sufs[p]
            if not diag:
                arg = arg - c_acc[p]
            att = jnp.exp(arg)
            if diag:
                att = jnp.where(causal, att, 0.0)
            att = att.astype(BF16)
            pv = _dot(jnp.concatenate([att[0:tq], att[tq:2 * tq]], axis=1), v2)
            if not diag:
                pv = pv + o_acc[:, p * LANES:(p + 1) * LANES]
            o_acc[:, p * LANES:(p + 1) * LANES] = pv
            c = jnp.sum(sps[p], axis=-1, keepdims=True)
            if not diag:
                c = c + c_acc[p]
            c_acc[p] = c
            m = jnp.min(c)
            low = m if low is None else jnp.minimum(low, m)
        return low

    def more(state):
        j, low = state
        return jnp.logical_and(j >= 0, low < ATTN_DEAD_CARRY)

    def step(state):
        j, _ = state
        return j - 1, block(j, False)

    lax.while_loop(more, step, (qi - 1, block(qi, True)))
    o_ref[0] = o_acc[...].astype(o_ref.dtype)


def _attention(qkv, *, bsz, seq, tq, npair=ATTN_PAIRS):
    qkv3 = qkv.reshape(bsz, seq, 3 * SB_WIDTH)
    width = npair * LANES
    groups = SB_WIDTH // width
    return pl.pallas_call(
        functools.partial(_attn_body, tq=tq, npair=npair),
        grid=(bsz, groups, seq // tq),
        in_specs=[pl.BlockSpec((1, tq, width), lambda b, p, i: (b, i, p)),
                  pl.BlockSpec((1, seq, width), lambda b, p, i: (b, 0, groups + p)),
                  pl.BlockSpec((1, seq, width), lambda b, p, i: (b, 0, 2 * groups + p))],
        out_specs=pl.BlockSpec((1, tq, width), lambda b, p, i: (b, i, p)),
        out_shape=jax.ShapeDtypeStruct((bsz, seq, SB_WIDTH), BF16),
        scratch_shapes=[pltpu.VMEM((tq, width), F32), pltpu.VMEM((npair, 2 * tq, 1), F32)],
        compiler_params=_params("parallel", "parallel", "arbitrary"),
        name="attn",
    )(qkv3, qkv3, qkv3)


def _block_diag(w, group):
    n, k, _ = w.shape
    w = w.reshape(n // group, group, k, k)
    eye = jnp.eye(group, dtype=w.dtype)
    return jnp.einsum('gakl,ab->gakbl', w, eye).reshape(n // group, group * k, group * k)


ROW_WORDS = 4


def _sorted_rows(tm):
    cap = -(-(TOP_K * tm + N_EXPERTS) // LANES) * LANES
    assert (cap // TOP_K) % 16 == 0
    return cap


def _to_row_tiles(ref, value, first_row=0):
    rows, d = value.shape
    bits = lax.bitcast_convert_type(value, I32)
    for s in range(ROW_WORDS):
        low = lax.shift_right_logical(bits[:, s * LANES:(s + 1) * LANES], 16)
        high = bits[:, d // 2 + s * LANES:d // 2 + (s + 1) * LANES]
        ref[pl.ds(first_row * ROW_WORDS + s, rows, stride=ROW_WORDS), :] = high | low


def _from_row_tiles(ref, rows):
    words = [ref[pl.ds(s, rows, stride=ROW_WORDS), :] for s in range(ROW_WORDS)]
    low = [lax.bitcast_convert_type(lax.shift_left(w, 16), F32).astype(BF16) for w in words]
    high = [lax.bitcast_convert_type(w & jnp.int32(-65536), F32).astype(BF16) for w in words]
    return jnp.concatenate(low + high, axis=1)


def _merge_body(x_ref, osb_ref, hg_ref, gg_ref, gbias_ref, wsb_ref, wlru_ref, wout_ref, g2_ref,
                wrt_ref, br_ref, h1_ref, xs_ref, pos_ref, gate_ref, cnt_ref):
    tm, d = x_ref.shape

    y_sb = _dot(osb_ref[...], wsb_ref[...])
    y_lru = _dot(hg_ref[...], wlru_ref[...])
    g_a = _sigmoid_t(gg_ref[:, 0:d] + gbias_ref[:, 0:d])
    g_b = _sigmoid_t(gg_ref[:, d:2 * d] + gbias_ref[:, d:2 * d])
    merged = (g_a * y_sb + g_b * y_lru).astype(BF16)
    h1 = x_ref[...] + _dot(merged, wout_ref[...])
    h1_ref[...] = h1
    u2 = _rms(h1, g2_ref[...]).astype(BF16)
    logits = lax.dot_general(wrt_ref[...], u2, (((1,), (1,)), ((), ())),
                             preferred_element_type=F32) + br_ref[...]

    e_iota = lax.broadcasted_iota(I32, (N_EXPERTS, tm), 0)
    k_iota = lax.broadcasted_iota(I32, (TOP_K, tm), 0)
    vals = logits
    chosen = jnp.zeros((N_EXPERTS, tm), F32)
    top_v, hots = [], []
    for _ in range(TOP_K):
        m = jnp.max(vals, axis=0, keepdims=True)
        sel = jnp.min(jnp.where(vals == m, e_iota, N_EXPERTS), axis=0, keepdims=True)
        hot = e_iota == sel
        top_v.append(m)
        hots.append(hot)
        chosen = jnp.where(hot, 1.0, chosen)
        vals = jnp.where(hot, -jnp.inf, vals)
    exps = [jnp.exp(v - top_v[0]) for v in top_v]
    denom = exps[0] + exps[1] + exps[2] + exps[3]

    chosen_b = chosen.astype(BF16)
    t_row = lax.broadcasted_iota(I32, (tm, tm), 0)
    t_col = lax.broadcasted_iota(I32, (tm, tm), 1)
    earlier = jnp.where(t_row < t_col, 1.0, 0.0).astype(BF16)
    same_before = _dot(chosen_b, earlier)
    e_row = lax.broadcasted_iota(I32, (N_EXPERTS, N_EXPERTS), 0)
    e_col = lax.broadcasted_iota(I32, (N_EXPERTS, N_EXPERTS), 1)
    smaller = jnp.where(e_col < e_row, 1.0, 0.0).astype(BF16)
    cnt = jnp.sum(chosen, axis=1, keepdims=True)
    odd = cnt - 2.0 * jnp.floor(0.5 * cnt)
    cnt_ref[...] = cnt + odd
    odd_smaller = _dot(smaller, jnp.broadcast_to(odd, (N_EXPERTS, LANES)).astype(BF16))[:, 0:1]
    first = jnp.sum(_dot(smaller, chosen_b), axis=1, keepdims=True) + odd_smaller

    where_to = first + same_before
    pos = jnp.zeros((TOP_K, tm), I32)
    gate = jnp.zeros((TOP_K, tm), F32)
    pos_k = []
    for k in range(TOP_K):
        pk = jnp.sum(jnp.where(hots[k], where_to, 0.0), axis=0, keepdims=True).astype(I32)
        pos_k.append(pk)
        pos = jnp.where(k_iota == k, pk, pos)
        gate = jnp.where(k_iota == k, exps[k] / denom, gate)
    pos_ref[...] = pos
    gate_ref[...] = gate

    cap = xs_ref.shape[0] // ROW_WORDS
    piece = cap // TOP_K
    for c in range(TOP_K):
        p_iota = lax.broadcasted_iota(I32, (piece, tm), 0) + c * piece
        hit = p_iota == pos_k[0]
        for k in range(1, TOP_K):
            hit = jnp.logical_or(hit, p_iota == pos_k[k])
        perm = jnp.where(hit, 1.0, 0.0).astype(BF16)
        _to_row_tiles(xs_ref, _dot(perm, u2), first_row=c * piece)


def _merge(x2, o_sb, hg, gg, gate_bias, wsb, wlru, wout, g2, wrt, br, *, tm):
    n, d = x2.shape
    tiles = n // tm
    assert d == 2 * ROW_WORDS * LANES, "row tile form packs 1024 columns into 4 x 128 words"
    cap = _sorted_rows(tm)
    row = lambda w: pl.BlockSpec((tm, w), lambda i: (i, 0))
    return pl.pallas_call(
        _merge_body,
        grid=(tiles,),
        in_specs=[row(d), row(SB_WIDTH), row(d), row(2 * d), _resident((1, 2 * d)),
                  _resident(wsb.shape), _resident(wlru.shape), _resident(wout.shape),
                  _resident((1, d)), _resident(wrt.shape), _resident((N_EXPERTS, 1))],
        out_specs=[row(d),
                   pl.BlockSpec((cap * ROW_WORDS, LANES), lambda i: (i, 0)),
                   pl.BlockSpec((TOP_K, tm), lambda i: (0, i)),
                   pl.BlockSpec((TOP_K, tm), lambda i: (0, i)),
                   pl.BlockSpec((N_EXPERTS, 1), lambda i: (i, 0))],
        out_shape=[jax.ShapeDtypeStruct((n, d), F32),
                   jax.ShapeDtypeStruct((tiles * cap * ROW_WORDS, LANES), I32),
                   jax.ShapeDtypeStruct((TOP_K, n), I32),
                   jax.ShapeDtypeStruct((TOP_K, n), F32),
                   jax.ShapeDtypeStruct((tiles * N_EXPERTS, 1), F32)],
        compiler_params=_params("parallel"),
        name="merge",
    )(x2, o_sb, hg, gg, gate_bias, wsb, wlru, wout, g2, wrt, br)


def _experts_body(eb_ref, be_ref, r0_ref, t0_ref, nv_ref, cum_ref, off_ref,
                  xs_hbm, w1_ref, b1_ref, w2_ref, b2_ref, ys_hbm,
                  xbuf, ybuf, w1b, w2b, gsem, ssem, *, tiles, tile_rows):
    expert = pl.program_id(0)
    rows = xbuf.shape[1] // ROW_WORDS
    ff = w2_ref.shape[1]
    n_used = eb_ref[N_EXPERTS]
    last_entry = be_ref.shape[0] - 1

    def for_runs(b, nv, fn):
        e = be_ref[b]
        r0 = r0_ref[b]
        end = r0 + nv

        def more(t):
            return jnp.logical_and(t < tiles, cum_ref[jnp.minimum(t, tiles) * N_EXPERTS + e] < end)

        def run(t):
            c0 = cum_ref[t * N_EXPERTS + e]
            c1 = cum_ref[(t + 1) * N_EXPERTS + e]
            lo = jnp.maximum(c0, r0)
            n_rows = jnp.minimum(c1, end) - lo

            @pl.when(n_rows > 0)
            def _():
                src = t * tile_rows + off_ref[t * N_EXPERTS + e] + (lo - c0)
                fn(pl.multiple_of(src * ROW_WORDS, SUBLANES),
                   pl.multiple_of((lo - r0) * ROW_WORDS, SUBLANES), n_rows * ROW_WORDS)

            return t + 1

        @pl.when(nv > 0)
        def _():
            lax.while_loop(more, run, t0_ref[b])

    def fetch(b, nv, s):
        def one(src, dst, size):
            pltpu.make_async_copy(xs_hbm.at[pl.ds(src, size)], xbuf.at[s, pl.ds(dst, size)],
                                  gsem.at[s]).start()
        for_runs(b, nv, one)

    def write_back(b, nv, s, sem):
        def one(src, dst, size):
            pltpu.make_async_copy(ybuf.at[s, pl.ds(dst, size)], ys_hbm.at[pl.ds(src, size)], sem).start()
        for_runs(b, nv, one)

    def wait_rows(nv, sem):
        @pl.when(nv > 0)
        def _():
            size = nv * ROW_WORDS
            pltpu.make_async_copy(xs_hbm.at[pl.ds(0, size)], xbuf.at[0, pl.ds(0, size)], sem).wait()

    def neighbours(b):
        prev = jnp.maximum(b - 1, 0)
        nv_prev = jnp.where(b >= 1, nv_ref[prev], 0)
        nv_prev2 = jnp.where(b >= 2, nv_ref[jnp.maximum(b - 2, 0)], 0)
        return prev, nv_prev, nv_prev2

    @pl.when(expert == 0)
    def _():
        xbuf[...] = jnp.zeros_like(xbuf)
        fetch(0, nv_ref[0], 0)

    @pl.when(eb_ref[expert + 1] > eb_ref[expert])
    def _():
        w1b[...] = w1_ref[0].astype(BF16)
        w2b[...] = w2_ref[0].astype(BF16)

    def run_block(b, carry):
        slot = b % 2
        prev, nv_prev, nv_prev2 = neighbours(b)
        nxt = jnp.minimum(b + 1, last_entry)
        write_back(prev, nv_prev, 1 - slot, ssem.at[slot])
        fetch(nxt, nv_ref[nxt], 1 - slot)
        wait_rows(nv_ref[b], gsem.at[slot])
        x = _from_row_tiles(xbuf.at[slot], rows)
        hid = _dot(x, w1b[...]) + b1_ref[0]
        x_glu = jnp.minimum(hid[:, 0:ff], SWIGLU_LIMIT)
        x_lin = lax.clamp(-SWIGLU_LIMIT, hid[:, ff:2 * ff], SWIGLU_LIMIT)
        swish = (0.5 * x_glu) * (jnp.tanh((0.5 * SWIGLU_ALPHA) * x_glu) + 1.0)
        act = swish * (x_lin + 1.0)
        y = _dot(act.astype(BF16), w2b[...]) + b2_ref[0]
        wait_rows(nv_prev2, ssem.at[1 - slot])
        _to_row_tiles(ybuf.at[slot], y.astype(BF16).astype(F32))
        return carry

    lax.fori_loop(eb_ref[expert], eb_ref[expert + 1], run_block, 0)

    @pl.when(expert == pl.num_programs(0) - 1)
    def _():
        slot = n_used % 2
        prev, nv_prev, nv_prev2 = neighbours(n_used)
        write_back(prev, nv_prev, 1 - slot, ssem.at[slot])
        wait_rows(nv_prev2, ssem.at[1 - slot])
        wait_rows(nv_prev, ssem.at[slot])


def _experts(expert_block, block_e, block_r0, block_t0, block_nv, cum, off, xs, w1, b1, w2, b2, *,
             rows, tiles, tile_rows):
    _, d, ff2 = w1.shape
    ff = ff2 // 2
    weights = lambda shape: pl.BlockSpec(shape, lambda e, *_: (e, 0, 0))
    grid_spec = pltpu.PrefetchScalarGridSpec(
        num_scalar_prefetch=7,
        grid=(N_EXPERTS,),
        in_specs=[pl.BlockSpec(memory_space=pl.ANY),
                  weights((1, d, ff2)), weights((1, 1, ff2)), weights((1, ff, d)), weights((1, 1, d))],
        out_specs=pl.BlockSpec(memory_space=pl.ANY),
        scratch_shapes=[pltpu.VMEM((2, rows * ROW_WORDS, LANES), I32),
                        pltpu.VMEM((2, rows * ROW_WORDS, LANES), I32),
                        pltpu.VMEM((d, ff2), BF16),
                        pltpu.VMEM((ff, d), BF16),
                        pltpu.SemaphoreType.DMA((2,)),
                        pltpu.SemaphoreType.DMA((2,))],
    )
    return pl.pallas_call(
        functools.partial(_experts_body, tiles=tiles, tile_rows=tile_rows),
        grid_spec=grid_spec,
        out_shape=jax.ShapeDtypeStruct(xs.shape, I32),
        input_output_aliases={7: 0},
        compiler_params=_params("arbitrary"),
        name="experts",
    )(expert_block, block_e, block_r0, block_t0, block_nv, cum, off, xs, w1,
      b1.reshape(N_EXPERTS, 1, ff2), w2, b2.reshape(N_EXPERTS, 1, d))


def _combine_body(pos_ref, gate_ref, h1_ref, g_ref, ys_ref, out_ref):
    tm = h1_ref.shape[0]
    cap = ys_ref.shape[0] // ROW_WORDS
    p_iota = lax.broadcasted_iota(I32, (tm, cap), 1)
    pos = pos_ref[...]
    gate = gate_ref[...]
    g_mat = jnp.zeros((tm, cap), F32)
    for k in range(TOP_K):
        g_mat = jnp.where(p_iota == pos[:, k:k + 1], gate[:, k:k + 1], g_mat)
    ys = _from_row_tiles(ys_ref, cap)
    out_ref[...] = _rms(h1_ref[...] + _dot(g_mat.astype(BF16), ys), g_ref[...])


def _combine(pos, gate, h1, g, ys, *, tm):
    n, d = h1.shape
    return pl.pallas_call(
        _combine_body,
        grid=(n // tm,),
        in_specs=[pl.BlockSpec((tm, TOP_K), lambda i: (i, 0)),
                  pl.BlockSpec((tm, TOP_K), lambda i: (i, 0)),
                  pl.BlockSpec((tm, d), lambda i: (i, 0)),
                  _resident((1, d)),
                  pl.BlockSpec((_sorted_rows(tm) * ROW_WORDS, LANES), lambda i: (i, 0))],
        out_specs=pl.BlockSpec((tm, d), lambda i: (i, 0)),
        out_shape=jax.ShapeDtypeStruct((n, d), F32),
        compiler_params=_params("parallel"),
        name="combine",
    )(pos, gate, h1, g, ys)


def _block_tables(cnt, rows):
    cum = jnp.concatenate([jnp.zeros((1, N_EXPERTS), I32), jnp.cumsum(cnt, axis=0)])
    off = jnp.cumsum(cnt, axis=1) - cnt
    total = cum[-1]
    blocks = (total + rows - 1) // rows
    block_end = jnp.cumsum(blocks)
    return cum, off, total, blocks, block_end


def _layer(h2, bsz, seq, norm_mix_g, w_in, gate_bias, conv_w, conv_b, w_rg_a, b_rg_a, w_rg_x, b_rg_x,
           lru_lambda, w_sb_o, w_lru_o, w_out, norm_ffn_g, w_router, b_router, w1, b1, w2, b2, out_g):
    n, d = h2.shape
    tm = min(ROW_TILE, n)
    tiles = n // tm
    group = MXU_DIM // (d // LRU_BLOCKS)
    qkv, gg, hg = _inproj(h2.reshape(bsz, seq, d), norm_mix_g.reshape(1, d), w_in.astype(BF16),
                          conv_w, conv_b.reshape(1, d),
                          _block_diag(w_rg_a, group).astype(BF16), b_rg_a.reshape(1, d),
                          _block_diag(w_rg_x, group).astype(BF16), b_rg_x.reshape(1, d),
                          lru_lambda.reshape(1, d), tt=min(INPROJ_TIME_BLOCK, seq))
    o_sb = _attention(qkv, bsz=bsz, seq=seq, tq=min(ATTN_BLOCK, seq)).reshape(n, SB_WIDTH)
    h1, xs, pos, gate, cnt = _merge(
        h2, o_sb, hg.reshape(n, d), gg.reshape(n, 2 * d), gate_bias.reshape(1, 2 * d),
        w_sb_o.astype(BF16), w_lru_o.astype(BF16),
        w_out.astype(BF16), norm_ffn_g.reshape(1, d), w_router.T.astype(BF16),
        b_router.reshape(N_EXPERTS, 1), tm=tm)

    rows = MOE_ROWS
    n_steps = (n * TOP_K + tiles * N_EXPERTS + N_EXPERTS * (rows - 1)) // rows + 1
    cnt = cnt.reshape(tiles, N_EXPERTS).astype(I32)
    cum, off, total, blocks, block_end = _block_tables(cnt, rows)
    step = jnp.arange(n_steps, dtype=I32)
    block_e = jnp.minimum(jnp.sum((step[:, None] >= block_end[None, :]).astype(I32), axis=1), N_EXPERTS - 1)
    block_r0 = (step - (block_end - blocks)[block_e]) * rows
    block_nv = jnp.clip(total[block_e] - block_r0, 0, rows)
    block_nv = jnp.where(step < block_end[-1], block_nv, 0)
    run_ends = jnp.sum(jnp.where(block_e[:, None, None] == jnp.arange(N_EXPERTS, dtype=I32)[None, None, :],
                                 cum[None, 1:, :], 0), axis=2)
    block_t0 = jnp.sum((run_ends <= block_r0[:, None]).astype(I32), axis=1)
    block_t0 = jnp.minimum(block_t0, tiles - 1)
    expert_block = jnp.concatenate([jnp.zeros((1,), I32), block_end.astype(I32)])

    ys = _experts(expert_block, block_e, block_r0, block_t0, block_nv, cum.reshape(-1), off.reshape(-1),
                  xs, w1, b1, w2, b2, rows=rows, tiles=tiles, tile_rows=_sorted_rows(tm))
    return _combine(pos.T, gate.T, h1, out_g.reshape(1, d), ys, tm=tm)


def kernel(x, norm_mix_g, w_in, gate_bias, conv_w, conv_b, w_rg_a, b_rg_a, w_rg_x, b_rg_x, lru_lambda,
           w_sb_o, w_lru_o, w_out, norm_ffn_g, w_router, b_router, w1, b1, w2, b2, norm_final_g):
    bsz, seq, d = x.shape
    depth = w_in.shape[0]
    assert depth == 1, "the final norm is fused into the last layer's combine"
    assert bsz % SUBLANES == 0 and d % MXU_DIM == 0
    h2 = x.reshape(bsz * seq, d)
    out = _layer(h2, bsz, seq, norm_mix_g[0], w_in[0], gate_bias[0], conv_w[0], conv_b[0], w_rg_a[0],
                 b_rg_a[0], w_rg_x[0], b_rg_x[0], lru_lambda[0], w_sb_o[0], w_lru_o[0], w_out[0],
                 norm_ffn_g[0], w_router[0], b_router[0], w1[0], b1[0], w2[0], b2[0], norm_final_g)
    return out.reshape(bsz, seq, d)
```

```python
import functools

import jax
import jax.numpy as jnp
from jax import lax
from jax.experimental import pallas as pl
from jax.experimental.pallas import tpu as pltpu

F32 = jnp.float32
BF16 = jnp.bfloat16
I32 = jnp.int32

SB_HEADS = 8
SB_HEAD_DIM = 64
SB_WIDTH = SB_HEADS * SB_HEAD_DIM
LRU_BLOCKS = 16
CONV_WIDTH = 4
RG_C = 8.0
N_EXPERTS = 32
TOP_K = 4
SWIGLU_ALPHA = 1.702
SWIGLU_LIMIT = 7.0
NORM_EPS = 1e-6

LANES = 128
SUBLANES = 8
MXU_DIM = 256
VMEM_LIMIT = 56 * 1024 * 1024

ROW_TILE = 512
ATTN_PAIRS = 4
ATTN_BLOCK = 256
ATTN_DEAD_CARRY = 104.0
INPROJ_TIME_BLOCK = 64
MOE_ROWS = 512


def _rms(x, g):
    return x * lax.rsqrt(jnp.mean(x * x, axis=-1, keepdims=True) + NORM_EPS) * g


def _sigmoid_t(x):
    return 0.5 * jnp.tanh(0.5 * x) + 0.5


def _softplus(x):
    return jnp.maximum(x, 0.0) + jnp.log(1.0 + jnp.exp(-jnp.abs(x)))


def _dot(a, b):
    return jnp.dot(a, b, preferred_element_type=F32)


def _params(*sem):
    return pltpu.CompilerParams(dimension_semantics=sem, vmem_limit_bytes=VMEM_LIMIT)


def _resident(shape):
    return pl.BlockSpec(shape, lambda *_: (0,) * len(shape), pipeline_mode=pl.Buffered(1))


def _inproj_body(x_ref, g_ref, w_ref, cw_ref, cb_ref, wa_ref, ba_ref, wx_ref, bx_ref, lam_ref,
                 qkv_ref, gg_ref, hg_ref, xpad, y_s, a_s, b_s, h_s, *, chunk):
    s = pl.program_id(1)
    nb, tt, d = x_ref.shape
    chunks = d // LANES
    n_rows = tt * nb
    halo = (CONV_WIDTH - 1) * nb
    u = _rms(x_ref[...].reshape(n_rows, d), g_ref[...]).astype(BF16)

    def project(col, width, store):
        for c in range(0, width, chunk):
            store(c, _dot(u, w_ref[:, col + c:col + c + chunk]))

    @pl.when(s == 0)
    def _():
        xpad[:, 0:halo, :] = jnp.zeros((chunks, halo, LANES), F32)
        h_s[...] = jnp.zeros_like(h_s)

    @pl.when(s > 0)
    def _():
        xpad[:, 0:halo, :] = xpad[:, n_rows:n_rows + halo, :]

    def store_x(c, val):
        for j in range(chunk // LANES):
            for b in range(nb):
                xpad[c // LANES + j, pl.ds(halo + b, tt, stride=nb), :] = (
                    val[b * tt:(b + 1) * tt, j * LANES:(j + 1) * LANES])

    def store_y(c, val):
        y_s[:, c:c + chunk] = val

    def store_to(ref):
        def store(c, val):
            ref[:, :, c:c + chunk] = val.astype(ref.dtype).reshape(nb, tt, chunk)
        return store

    x_col = qkv_ref.shape[2]
    project(x_col, d, store_x)
    todo = ([(x_col + d + c, store_y, c) for c in range(0, d, chunk)]
            + [(c, store_to(qkv_ref), c) for c in range(0, x_col, chunk)]
            + [(x_col + 2 * d + c, store_to(gg_ref), c) for c in range(0, 2 * d, chunk)])

    def project_next(count=1):
        for _ in range(count):
            if todo:
                col, store, c = todo.pop(0)
                store(c, _dot(u, w_ref[:, col:col + chunk]))

    def conv(j):
        lanes = slice(j * LANES, (j + 1) * LANES)
        xc = jnp.zeros((n_rows, LANES), F32) + cb_ref[:, lanes]
        for i in range(CONV_WIDTH):
            xc = xc + cw_ref[i:i + 1, lanes] * xpad[j, i * nb:i * nb + n_rows, :]
        return xc

    neg_sp_lam = -RG_C * _softplus(-lam_ref[...])
    per_group = MXU_DIM // LANES
    for g in range(chunks // per_group):
        project_next(2)
        sl = slice(g * MXU_DIM, (g + 1) * MXU_DIM)
        xc = jnp.concatenate([conv(g * per_group + j) for j in range(per_group)], axis=1)
        xb = xc.astype(BF16)
        r = _sigmoid_t(_dot(xb, wa_ref[g]) + ba_ref[:, sl])
        gate_i = _sigmoid_t(_dot(xb, wx_ref[g]) + bx_ref[:, sl])
        a = jnp.exp(r * neg_sp_lam[:, sl])
        var = (1.0 - a) * (1.0 + a)
        mult = jnp.where(var > 0.0, var * lax.rsqrt(var), 0.0)
        b = mult * (gate_i * xc)
        for j in range(per_group):
            a_s[g * per_group + j] = a[:, j * LANES:(j + 1) * LANES]
            b_s[g * per_group + j] = b[:, j * LANES:(j + 1) * LANES]

    def step(t, hs):
        rows = pl.ds(pl.multiple_of(t * nb, nb), nb)
        out = []
        for j in range(chunks):
            h = a_s[j, rows, :] * hs[j] + b_s[j, rows, :]
            b_s[j, rows, :] = h
            out.append(h)
        return tuple(out)

    hs = lax.fori_loop(0, tt, step, tuple(h_s[j] for j in range(chunks)))
    for j in range(chunks):
        h_s[j] = hs[j]

    for j in range(chunks):
        project_next(len(todo) if j == chunks - 1 else 1)
        for bi in range(nb):
            y = y_s[bi * tt:(bi + 1) * tt, j * LANES:(j + 1) * LANES]
            gelu = 0.5 * y * (1.0 + jnp.tanh(0.7978845608028654 * (y + 0.044715 * (y * y * y))))
            h = b_s[j, pl.ds(bi, tt, stride=nb), :]
            hg_ref[bi, :, j * LANES:(j + 1) * LANES] = (h * gelu).astype(hg_ref.dtype)


def _inproj(x3, g, w_in_bf, conv_w, conv_b, wa_bd, b_a, wx_bd, b_x, lam, *, tt):
    bsz, seq, d = x3.shape
    nb = SUBLANES
    chunks = d // LANES
    tile = lambda w: pl.BlockSpec((nb, tt, w), lambda b, s: (b, s, 0))
    return pl.pallas_call(
        functools.partial(_inproj_body, chunk=MXU_DIM),
        grid=(bsz // nb, seq // tt),
        in_specs=[tile(d), _resident((1, d)), _resident(w_in_bf.shape),
                  _resident(conv_w.shape), _resident((1, d)),
                  _resident(wa_bd.shape), _resident((1, d)),
                  _resident(wx_bd.shape), _resident((1, d)),
                  _resident((1, d))],
        out_specs=[tile(3 * SB_WIDTH), tile(2 * d), tile(d)],
        out_shape=[jax.ShapeDtypeStruct((bsz, seq, 3 * SB_WIDTH), BF16),
                   jax.ShapeDtypeStruct((bsz, seq, 2 * d), F32),
                   jax.ShapeDtypeStruct((bsz, seq, d), BF16)],
        scratch_shapes=[pltpu.VMEM((chunks, (tt + CONV_WIDTH - 1) * nb, LANES), F32),
                        pltpu.VMEM((nb * tt, d), F32),
                        pltpu.VMEM((chunks, tt * nb, LANES), F32),
                        pltpu.VMEM((chunks, tt * nb, LANES), F32),
                        pltpu.VMEM((chunks, nb, LANES), F32)],
        compiler_params=_params("parallel", "arbitrary"),
        name="inproj",
    )(x3, g, w_in_bf, conv_w, conv_b, wa_bd, b_a, wx_bd, b_x, lam)


def _attn_body(q_ref, k_ref, v_ref, o_ref, o_acc, c_acc, *, tq, npair):
    qi = pl.program_id(2)
    lane = lax.broadcasted_iota(I32, (1, LANES), 1)
    first = lane < SB_HEAD_DIM
    q_pairs = []
    for p in range(npair):
        q = q_ref[0, :, p * LANES:(p + 1) * LANES] * (SB_HEAD_DIM ** -0.5)
        zq = jnp.zeros_like(q)
        q_pairs.append(jnp.concatenate([jnp.where(first, q, zq), jnp.where(first, zq, q)], axis=0))
    row = lax.broadcasted_iota(I32, (2 * tq, tq), 0)
    row = jnp.where(row >= tq, row - tq, row)
    col = lax.broadcasted_iota(I32, (2 * tq, tq), 1)
    causal = col < row
    tri2 = jnp.where(row > col, 1.0, 0.0).astype(BF16)

    def block(j, diag):
        start = pl.multiple_of(j * tq, tq)
        zs, sps, sufs = [], [], []
        for p in range(npair):
            kj = k_ref[0, pl.ds(start, tq), p * LANES:(p + 1) * LANES]
            zs.append(lax.dot_general(q_pairs[p], kj, (((1,), (1,)), ((), ())), preferred_element_type=F32))
        for p in range(npair):
            sp = _softplus(zs[p])
            if diag:
                sp = jnp.where(causal, sp, 0.0)
            sps.append(sp)
            sp_hi = sp.astype(BF16)
            sp_lo = (sp - sp_hi.astype(F32)).astype(BF16)
            sufs.append(_dot(jnp.concatenate([sp_hi, sp_lo], axis=1), tri2))
        low = None
        for p in range(npair):
            vj = v_ref[0, pl.ds(start, tq), p * LANES:(p + 1) * LANES]
            zv = jnp.zeros_like(vj)
            v2 = jnp.concatenate([jnp.where(first, vj, zv), jnp.where(first, zv, vj)], axis=0)
            arg = zs[p] - sps[p] - sufs[p]
            if not diag:
                arg = arg - c_acc[p]
            att = jnp.exp(arg)
            if diag:
                att = jnp.where(causal, att, 0.0)
            att = att.astype(BF16)
            pv = _dot(jnp.concatenate([att[0:tq], att[tq:2 * tq]], axis=1), v2)
            if not diag:
                pv = pv + o_acc[:, p * LANES:(p + 1) * LANES]
            o_acc[:, p * LANES:(p + 1) * LANES] = pv
            c = jnp.sum(sps[p], axis=-1, keepdims=True)
            if not diag:
                c = c + c_acc[p]
            c_acc[p] = c
            m = jnp.min(c)
            low = m if low is None else jnp.minimum(low, m)
        return low

    def more(state):
        j, low = state
        return jnp.logical_and(j >= 0, low < ATTN_DEAD_CARRY)

    def step(state):
        j, _ = state
        return j - 1, block(j, False)

    lax.while_loop(more, step, (qi - 1, block(qi, True)))
    o_ref[0] = o_acc[...].astype(o_ref.dtype)


def _attention(qkv, *, bsz, seq, tq, npair=ATTN_PAIRS):
    qkv3 = qkv.reshape(bsz, seq, 3 * SB_WIDTH)
    width = npair * LANES
    groups = SB_WIDTH // width
    return pl.pallas_call(
        functools.partial(_attn_body, tq=tq, npair=npair),
        grid=(bsz, groups, seq // tq),
        in_specs=[pl.BlockSpec((1, tq, width), lambda b, p, i: (b, i, p)),
                  pl.BlockSpec((1, seq, width), lambda b, p, i: (b, 0, groups + p)),
                  pl.BlockSpec((1, seq, width), lambda b, p, i: (b, 0, 2 * groups + p))],
        out_specs=pl.BlockSpec((1, tq, width), lambda b, p, i: (b, i, p)),
        out_shape=jax.ShapeDtypeStruct((bsz, seq, SB_WIDTH), BF16),
        scratch_shapes=[pltpu.VMEM((tq, width), F32), pltpu.VMEM((npair, 2 * tq, 1), F32)],
        compiler_params=_params("parallel", "parallel", "arbitrary"),
        name="attn",
    )(qkv3, qkv3, qkv3)


def _block_diag(w, group):
    n, k, _ = w.shape
    w = w.reshape(n // group, group, k, k)
    eye = jnp.eye(group, dtype=w.dtype)
    return jnp.einsum('gakl,ab->gakbl', w, eye).reshape(n // group, group * k, group * k)


ROW_WORDS = 4


def _sorted_rows(tm):
    cap = -(-(TOP_K * tm + N_EXPERTS) // LANES) * LANES
    assert (cap // TOP_K) % 16 == 0
    return cap


def _to_row_tiles(ref, value, first_row=0):
    rows, d = value.shape
    bits = lax.bitcast_convert_type(value, I32)
    for s in range(ROW_WORDS):
        low = lax.shift_right_logical(bits[:, s * LANES:(s + 1) * LANES], 16)
        high = bits[:, d // 2 + s * LANES:d // 2 + (s + 1) * LANES]
        ref[pl.ds(first_row * ROW_WORDS + s, rows, stride=ROW_WORDS), :] = high | low


def _from_row_tiles(ref, rows):
    words = [ref[pl.ds(s, rows, stride=ROW_WORDS), :] for s in range(ROW_WORDS)]
    low = [lax.bitcast_convert_type(lax.shift_left(w, 16), F32).astype(BF16) for w in words]
    high = [lax.bitcast_convert_type(w & jnp.int32(-65536), F32).astype(BF16) for w in words]
    return jnp.concatenate(low + high, axis=1)


def _merge_body(x_ref, osb_ref, hg_ref, gg_ref, gbias_ref, wsb_ref, wlru_ref, wout_ref, g2_ref,
                wrt_ref, br_ref, h1_ref, xs_ref, pos_ref, gate_ref, cnt_ref):
    tm, d = x_ref.shape

    y_sb = _dot(osb_ref[...], wsb_ref[...])
    y_lru = _dot(hg_ref[...], wlru_ref[...])
    g_a = _sigmoid_t(gg_ref[:, 0:d] + gbias_ref[:, 0:d])
    g_b = _sigmoid_t(gg_ref[:, d:2 * d] + gbias_ref[:, d:2 * d])
    merged = (g_a * y_sb + g_b * y_lru).astype(BF16)
    h1 = x_ref[...] + _dot(merged, wout_ref[...])
    h1_ref[...] = h1
    u2 = _rms(h1, g2_ref[...]).astype(BF16)
    logits = lax.dot_general(wrt_ref[...], u2, (((1,), (1,)), ((), ())),
                             preferred_element_type=F32) + br_ref[...]

    e_iota = lax.broadcasted_iota(I32, (N_EXPERTS, tm), 0)
    k_iota = lax.broadcasted_iota(I32, (TOP_K, tm), 0)
    vals = logits
    chosen = jnp.zeros((N_EXPERTS, tm), F32)
    top_v, hots = [], []
    for _ in range(TOP_K):
        m = jnp.max(vals, axis=0, keepdims=True)
        sel = jnp.min(jnp.where(vals == m, e_iota, N_EXPERTS), axis=0, keepdims=True)
        hot = e_iota == sel
        top_v.append(m)
        hots.append(hot)
        chosen = jnp.where(hot, 1.0, chosen)
        vals = jnp.where(hot, -jnp.inf, vals)
    exps = [jnp.exp(v - top_v[0]) for v in top_v]
    denom = exps[0] + exps[1] + exps[2] + exps[3]

    chosen_b = chosen.astype(BF16)
    t_row = lax.broadcasted_iota(I32, (tm, tm), 0)
    t_col = lax.broadcasted_iota(I32, (tm, tm), 1)
    earlier = jnp.where(t_row < t_col, 1.0, 0.0).astype(BF16)
    same_before = _dot(chosen_b, earlier)
    e_row = lax.broadcasted_iota(I32, (N_EXPERTS, N_EXPERTS), 0)
    e_col = lax.broadcasted_iota(I32, (N_EXPERTS, N_EXPERTS), 1)
    smaller = jnp.where(e_col < e_row, 1.0, 0.0).astype(BF16)
    cnt = jnp.sum(chosen, axis=1, keepdims=True)
    odd = cnt - 2.0 * jnp.floor(0.5 * cnt)
    cnt_ref[...] = cnt + odd
    odd_smaller = _dot(smaller, jnp.broadcast_to(odd, (N_EXPERTS, LANES)).astype(BF16))[:, 0:1]
    first = jnp.sum(_dot(smaller, chosen_b), axis=1, keepdims=True) + odd_smaller

    where_to = first + same_before
    pos = jnp.zeros((TOP_K, tm), I32)
    gate = jnp.zeros((TOP_K, tm), F32)
    pos_k = []
    for k in range(TOP_K):
        pk = jnp.sum(jnp.where(hots[k], where_to, 0.0), axis=0, keepdims=True).astype(I32)
        pos_k.append(pk)
        pos = jnp.where(k_iota == k, pk, pos)
        gate = jnp.where(k_iota == k, exps[k] / denom, gate)
    pos_ref[...] = pos
    gate_ref[...] = gate

    cap = xs_ref.shape[0] // ROW_WORDS
    piece = cap // TOP_K
    for c in range(TOP_K):
        p_iota = lax.broadcasted_iota(I32, (piece, tm), 0) + c * piece
        hit = p_iota == pos_k[0]
        for k in range(1, TOP_K):
            hit = jnp.logical_or(hit, p_iota == pos_k[k])
        perm = jnp.where(hit, 1.0, 0.0).astype(BF16)
        _to_row_tiles(xs_ref, _dot(perm, u2), first_row=c * piece)


def _merge(x2, o_sb, hg, gg, gate_bias, wsb, wlru, wout, g2, wrt, br, *, tm):
    n, d = x2.shape
    tiles = n // tm
    assert d == 2 * ROW_WORDS * LANES, "row tile form packs 1024 columns into 4 x 128 words"
    cap = _sorted_rows(tm)
    row = lambda w: pl.BlockSpec((tm, w), lambda i: (i, 0))
    return pl.pallas_call(
        _merge_body,
        grid=(tiles,),
        in_specs=[row(d), row(SB_WIDTH), row(d), row(2 * d), _resident((1, 2 * d)),
                  _resident(wsb.shape), _resident(wlru.shape), _resident(wout.shape),
                  _resident((1, d)), _resident(wrt.shape), _resident((N_EXPERTS, 1))],
        out_specs=[row(d),
                   pl.BlockSpec((cap * ROW_WORDS, LANES), lambda i: (i, 0)),
                   pl.BlockSpec((TOP_K, tm), lambda i: (0, i)),
                   pl.BlockSpec((TOP_K, tm), lambda i: (0, i)),
                   pl.BlockSpec((N_EXPERTS, 1), lambda i: (i, 0))],
        out_shape=[jax.ShapeDtypeStruct((n, d), F32),
                   jax.ShapeDtypeStruct((tiles * cap * ROW_WORDS, LANES), I32),
                   jax.ShapeDtypeStruct((TOP_K, n), I32),
                   jax.ShapeDtypeStruct((TOP_K, n), F32),
                   jax.ShapeDtypeStruct((tiles * N_EXPERTS, 1), F32)],
        compiler_params=_params("parallel"),
        name="merge",
    )(x2, o_sb, hg, gg, gate_bias, wsb, wlru, wout, g2, wrt, br)


def _experts_body(eb_ref, be_ref, r0_ref, t0_ref, nv_ref, cum_ref, off_ref,
                  xs_hbm, w1_ref, b1_ref, w2_ref, b2_ref, ys_hbm,
                  xbuf, ybuf, w1b, w2b, gsem, ssem, *, tiles, tile_rows):
    expert = pl.program_id(0)
    rows = xbuf.shape[1] // ROW_WORDS
    ff = w2_ref.shape[1]
    n_used = eb_ref[N_EXPERTS]
    last_entry = be_ref.shape[0] - 1

    def for_runs(b, nv, fn):
        e = be_ref[b]
        r0 = r0_ref[b]
        end = r0 + nv

        def more(t):
            return jnp.logical_and(t < tiles, cum_ref[jnp.minimum(t, tiles) * N_EXPERTS + e] < end)

        def run(t):
            c0 = cum_ref[t * N_EXPERTS + e]
            c1 = cum_ref[(t + 1) * N_EXPERTS + e]
            lo = jnp.maximum(c0, r0)
            n_rows = jnp.minimum(c1, end) - lo

            @pl.when(n_rows > 0)
            def _():
                src = t * tile_rows + off_ref[t * N_EXPERTS + e] + (lo - c0)
                fn(pl.multiple_of(src * ROW_WORDS, SUBLANES),
                   pl.multiple_of((lo - r0) * ROW_WORDS, SUBLANES), n_rows * ROW_WORDS)

            return t + 1

        @pl.when(nv > 0)
        def _():
            lax.while_loop(more, run, t0_ref[b])

    def fetch(b, nv, s):
        def one(src, dst, size):
            pltpu.make_async_copy(xs_hbm.at[pl.ds(src, size)], xbuf.at[s, pl.ds(dst, size)],
                                  gsem.at[s]).start()
        for_runs(b, nv, one)

    def write_back(b, nv, s, sem):
        def one(src, dst, size):
            pltpu.make_async_copy(ybuf.at[s, pl.ds(dst, size)], ys_hbm.at[pl.ds(src, size)], sem).start()
        for_runs(b, nv, one)

    def wait_rows(nv, sem):
        @pl.when(nv > 0)
        def _():
            size = nv * ROW_WORDS
            pltpu.make_async_copy(xs_hbm.at[pl.ds(0, size)], xbuf.at[0, pl.ds(0, size)], sem).wait()

    def neighbours(b):
        prev = jnp.maximum(b - 1, 0)
        nv_prev = jnp.where(b >= 1, nv_ref[prev], 0)
        nv_prev2 = jnp.where(b >= 2, nv_ref[jnp.maximum(b - 2, 0)], 0)
        return prev, nv_prev, nv_prev2

    @pl.when(expert == 0)
    def _():
        xbuf[...] = jnp.zeros_like(xbuf)
        fetch(0, nv_ref[0], 0)

    @pl.when(eb_ref[expert + 1] > eb_ref[expert])
    def _():
        w1b[...] = w1_ref[0].astype(BF16)
        w2b[...] = w2_ref[0].astype(BF16)

    def run_block(b, carry):
        slot = b % 2
        prev, nv_prev, nv_prev2 = neighbours(b)
        nxt = jnp.minimum(b + 1, last_entry)
        write_back(prev, nv_prev, 1 - slot, ssem.at[slot])
        fetch(nxt, nv_ref[nxt], 1 - slot)
        wait_rows(nv_ref[b], gsem.at[slot])
        x = _from_row_tiles(xbuf.at[slot], rows)
        hid = _dot(x, w1b[...]) + b1_ref[0]
        x_glu = jnp.minimum(hid[:, 0:ff], SWIGLU_LIMIT)
        x_lin = lax.clamp(-SWIGLU_LIMIT, hid[:, ff:2 * ff], SWIGLU_LIMIT)
        swish = (0.5 * x_glu) * (jnp.tanh((0.5 * SWIGLU_ALPHA) * x_glu) + 1.0)
        act = swish * (x_lin + 1.0)
        y = _dot(act.astype(BF16), w2b[...]) + b2_ref[0]
        wait_rows(nv_prev2, ssem.at[1 - slot])
        _to_row_tiles(ybuf.at[slot], y.astype(BF16).astype(F32))
        return carry

    lax.fori_loop(eb_ref[expert], eb_ref[expert + 1], run_block, 0)

    @pl.when(expert == pl.num_programs(0) - 1)
    def _():
        slot = n_used % 2
        prev, nv_prev, nv_prev2 = neighbours(n_used)
        write_back(prev, nv_prev, 1 - slot, ssem.at[slot])
        wait_rows(nv_prev2, ssem.at[1 - slot])
        wait_rows(nv_prev, ssem.at[slot])


def _experts(expert_block, block_e, block_r0, block_t0, block_nv, cum, off, xs, w1, b1, w2, b2, *,
             rows, tiles, tile_rows):
    _, d, ff2 = w1.shape
    ff = ff2 // 2
    weights = lambda shape: pl.BlockSpec(shape, lambda e, *_: (e, 0, 0))
    grid_spec = pltpu.PrefetchScalarGridSpec(
        num_scalar_prefetch=7,
        grid=(N_EXPERTS,),
        in_specs=[pl.BlockSpec(memory_space=pl.ANY),
                  weights((1, d, ff2)), weights((1, 1, ff2)), weights((1, ff, d)), weights((1, 1, d))],
        out_specs=pl.BlockSpec(memory_space=pl.ANY),
        scratch_shapes=[pltpu.VMEM((2, rows * ROW_WORDS, LANES), I32),
                        pltpu.VMEM((2, rows * ROW_WORDS, LANES), I32),
                        pltpu.VMEM((d, ff2), BF16),
                        pltpu.VMEM((ff, d), BF16),
                        pltpu.SemaphoreType.DMA((2,)),
                        pltpu.SemaphoreType.DMA((2,))],
    )
    return pl.pallas_call(
        functools.partial(_experts_body, tiles=tiles, tile_rows=tile_rows),
        grid_spec=grid_spec,
        out_shape=jax.ShapeDtypeStruct(xs.shape, I32),
        input_output_aliases={7: 0},
        compiler_params=_params("arbitrary"),
        name="experts",
    )(expert_block, block_e, block_r0, block_t0, block_nv, cum, off, xs, w1,
      b1.reshape(N_EXPERTS, 1, ff2), w2, b2.reshape(N_EXPERTS, 1, d))


def _combine_body(pos_ref, gate_ref, h1_ref, g_ref, ys_ref, out_ref):
    tm = h1_ref.shape[0]
    cap = ys_ref.shape[0] // ROW_WORDS
    p_iota = lax.broadcasted_iota(I32, (tm, cap), 1)
    pos = pos_ref[...]
    gate = gate_ref[...]
    g_mat = jnp.zeros((tm, cap), F32)
    for k in range(TOP_K):
        g_mat = jnp.where(p_iota == pos[:, k:k + 1], gate[:, k:k + 1], g_mat)
    ys = _from_row_tiles(ys_ref, cap)
    out_ref[...] = _rms(h1_ref[...] + _dot(g_mat.astype(BF16), ys), g_ref[...])


def _combine(pos, gate, h1, g, ys, *, tm):
    n, d = h1.shape
    row = lambda w: pl.BlockSpec((tm, w), lambda i: (i, 0))
    in_specs = [row(TOP_K), row(TOP_K),
                pl.BlockSpec((tm, d), lambda i: (i, 0), pipeline_mode=pl.Buffered(3)),
                pl.BlockSpec((1, d), lambda i: (0, 0)),
                pl.BlockSpec((_sorted_rows(tm) * ROW_WORDS, LANES), lambda i: (i, 0),
                             pipeline_mode=pl.Buffered(3))]

    def outer(pos_hbm, gate_hbm, h1_hbm, g_hbm, ys_hbm, out_hbm):
        pltpu.emit_pipeline(_combine_body, grid=(n // tm,), in_specs=in_specs,
                            out_specs=[row(d)])(pos_hbm, gate_hbm, h1_hbm, g_hbm, ys_hbm, out_hbm)

    return pl.pallas_call(
        outer,
        in_specs=[pl.BlockSpec(memory_space=pl.ANY)] * 5,
        out_specs=pl.BlockSpec(memory_space=pl.ANY),
        out_shape=jax.ShapeDtypeStruct((n, d), F32),
        compiler_params=pltpu.CompilerParams(vmem_limit_bytes=VMEM_LIMIT),
        name="combine",
    )(pos, gate, h1, g, ys)


def _block_tables(cnt, rows):
    cum = jnp.concatenate([jnp.zeros((1, N_EXPERTS), I32), jnp.cumsum(cnt, axis=0)])
    off = jnp.cumsum(cnt, axis=1) - cnt
    total = cum[-1]
    blocks = (total + rows - 1) // rows
    block_end = jnp.cumsum(blocks)
    return cum, off, total, blocks, block_end


def _layer(h2, bsz, seq, norm_mix_g, w_in, gate_bias, conv_w, conv_b, w_rg_a, b_rg_a, w_rg_x, b_rg_x,
           lru_lambda, w_sb_o, w_lru_o, w_out, norm_ffn_g, w_router, b_router, w1, b1, w2, b2, out_g):
    n, d = h2.shape
    tm = min(ROW_TILE, n)
    tiles = n // tm
    group = MXU_DIM // (d // LRU_BLOCKS)
    qkv, gg, hg = _inproj(h2.reshape(bsz, seq, d), norm_mix_g.reshape(1, d), w_in.astype(BF16),
                          conv_w, conv_b.reshape(1, d),
                          _block_diag(w_rg_a, group).astype(BF16), b_rg_a.reshape(1, d),
                          _block_diag(w_rg_x, group).astype(BF16), b_rg_x.reshape(1, d),
                          lru_lambda.reshape(1, d), tt=min(INPROJ_TIME_BLOCK, seq))
    o_sb = _attention(qkv, bsz=bsz, seq=seq, tq=min(ATTN_BLOCK, seq)).reshape(n, SB_WIDTH)
    h1, xs, pos, gate, cnt = _merge(
        h2, o_sb, hg.reshape(n, d), gg.reshape(n, 2 * d), gate_bias.reshape(1, 2 * d),
        w_sb_o.astype(BF16), w_lru_o.astype(BF16),
        w_out.astype(BF16), norm_ffn_g.reshape(1, d), w_router.T.astype(BF16),
        b_router.reshape(N_EXPERTS, 1), tm=tm)

    rows = MOE_ROWS
    n_steps = (n * TOP_K + tiles * N_EXPERTS + N_EXPERTS * (rows - 1)) // rows + 1
    cnt = cnt.reshape(tiles, N_EXPERTS).astype(I32)
    cum, off, total, blocks, block_end = _block_tables(cnt, rows)
    step = jnp.arange(n_steps, dtype=I32)
    block_e = jnp.minimum(jnp.sum((step[:, None] >= block_end[None, :]).astype(I32), axis=1), N_EXPERTS - 1)
    block_r0 = (step - (block_end - blocks)[block_e]) * rows
    block_nv = jnp.clip(total[block_e] - block_r0, 0, rows)
    block_nv = jnp.where(step < block_end[-1], block_nv, 0)
    run_ends = jnp.sum(jnp.where(block_e[:, None, None] == jnp.arange(N_EXPERTS, dtype=I32)[None, None, :],
                                 cum[None, 1:, :], 0), axis=2)
    block_t0 = jnp.sum((run_ends <= block_r0[:, None]).astype(I32), axis=1)
    block_t0 = jnp.minimum(block_t0, tiles - 1)
    expert_block = jnp.concatenate([jnp.zeros((1,), I32), block_end.astype(I32)])

    ys = _experts(expert_block, block_e, block_r0, block_t0, block_nv, cum.reshape(-1), off.reshape(-1),
                  xs, w1, b1, w2, b2, rows=rows, tiles=tiles, tile_rows=_sorted_rows(tm))
    return _combine(pos.T, gate.T, h1, out_g.reshape(1, d), ys, tm=tm)


def kernel(x, norm_mix_g, w_in, gate_bias, conv_w, conv_b, w_rg_a, b_rg_a, w_rg_x, b_rg_x, lru_lambda,
           w_sb_o, w_lru_o, w_out, norm_ffn_g, w_router, b_router, w1, b1, w2, b2, norm_final_g):
    bsz, seq, d = x.shape
    depth = w_in.shape[0]
    assert depth == 1, "the final norm is fused into the last layer's combine"
    assert bsz % SUBLANES == 0 and d % MXU_DIM == 0
    h2 = x.reshape(bsz * seq, d)
    out = _layer(h2, bsz, seq, norm_mix_g[0], w_in[0], gate_bias[0], conv_w[0], conv_b[0], w_rg_a[0],
                 b_rg_a[0], w_rg_x[0], b_rg_x[0], lru_lambda[0], w_sb_o[0], w_lru_o[0], w_out[0],
                 norm_ffn_g[0], w_router[0], b_router[0], w1[0], b1[0], w2[0], b2[0], norm_final_g)
    return out.reshape(bsz, seq, d)
```
